```python
import jax, jax.numpy as jnp
from jax import lax
import numpy as np

D_MODEL = 1024
BATCH = 2
SEQ = 8192
DEPTH = 2

N_MEM = 256
MLSTM_HEADS = 4
MLSTM_HEAD_DIM = D_MODEL // 16
MLSTM_WIDTH = MLSTM_HEADS * MLSTM_HEAD_DIM
MLSTM_CHUNK = 64
MLSTM_CONV = 4
CONF_CHANNELS = D_MODEL // 4
CONF_KERNEL = 31
FOX_HEADS = 8
FOX_HEAD_DIM = D_MODEL // 16
FOX_WIDTH = FOX_HEADS * FOX_HEAD_DIM
FOX_BLOCK = 128
MIX_WIDTH = MLSTM_WIDTH + CONF_CHANNELS + FOX_WIDTH
XATTN_HEADS = 4
XATTN_HEAD_DIM = D_MODEL // 8
XATTN_WIDTH = XATTN_HEADS * XATTN_HEAD_DIM
D_FF = ((8 * D_MODEL // 3) + 255) // 256 * 256
N_EXPERTS = 8
TOP_K = 2
EXPERT_FF = D_FF
N_DENSE_LAYERS = (DEPTH + 1) // 2
N_MOE_LAYERS = DEPTH // 2
EPS = 1e-6

IN_SPLITS = (2 * MLSTM_WIDTH, MLSTM_WIDTH, MLSTM_WIDTH, MLSTM_HEADS, MLSTM_HEADS,
             2 * CONF_CHANNELS, FOX_WIDTH, FOX_WIDTH, FOX_WIDTH, FOX_HEADS)
IN_WIDTH = sum(IN_SPLITS)
IN_SPLIT_IDX = tuple(int(i) for i in np.cumsum(IN_SPLITS)[:-1])

kernel_name = "hybrid_mlstm_conformer_fox_moe_block"


def rmsnorm(x, w):
    xf = x.astype(jnp.float32)
    y = xf * lax.rsqrt(jnp.mean(xf * xf, axis=-1, keepdims=True) + EPS)
    return (y * w.astype(jnp.float32)).astype(x.dtype)


def layernorm_f32(x, w, b):
    mu = jnp.mean(x, axis=-1, keepdims=True)
    xc = x - mu
    var = jnp.mean(xc * xc, axis=-1, keepdims=True)
    return xc * lax.rsqrt(var + EPS) * w.astype(jnp.float32) + b.astype(jnp.float32)


def causal_dwconv(x, w, b):
    K, C = w.shape
    y = lax.conv_general_dilated(
        x, w.astype(jnp.float32)[:, None, :], window_strides=(1,), padding=[(K - 1, 0)],
        dimension_numbers=("NWC", "WIO", "NWC"), feature_group_count=C)
    return y + b.astype(jnp.float32)


def mlstm_chunkwise(q, k, v, i_pre, f_pre):
    B, S, H, Dh = q.shape
    L = MLSTM_CHUNK
    nc = S // L
    log_f = jax.nn.log_sigmoid(f_pre)
    log_i = i_pre

    def chunks(t):
        return t.reshape(B, nc, L, H, Dh).transpose(1, 0, 3, 2, 4)

    def gchunks(t):
        return t.reshape(B, nc, L, H).transpose(1, 0, 3, 2)

    causal = jnp.tril(jnp.ones((L, L), dtype=bool))

    def step(carry, xs):
        C, n, m = carry
        qc, kc, vc, lic, lfc = xs
        b = jnp.cumsum(lfc, axis=-1)
        d_log = b[..., :, None] - b[..., None, :] + lic[..., None, :]
        d_log = jnp.where(causal, d_log, -jnp.inf)
        inter = b + m[..., None]
        m_t = jnp.maximum(jnp.max(d_log, axis=-1), inter)
        s = jnp.einsum("bhtd,bhsd->bhts", qc, kc) * jnp.exp(d_log - m_t[..., None])
        w_inter = jnp.exp(inter - m_t)
        num = (jnp.einsum("bhts,bhsd->bhtd", s, vc)
               + w_inter[..., None] * jnp.einsum("bhtk,bhkv->bhtv", qc, C))
        den = jnp.sum(s, axis=-1) + w_inter * jnp.einsum("bhtk,bhk->bht", qc, n)
        h = num / jnp.maximum(jnp.abs(den), jnp.exp(-m_t))[..., None]
        b_tot = b[..., -1]
        g = b_tot[..., None] - b + lic
        m_new = jnp.maximum(b_tot + m, jnp.max(g, axis=-1))
        wg = jnp.exp(g - m_new[..., None])
        decay = jnp.exp(b_tot + m - m_new)
        C_new = decay[..., None, None] * C + jnp.einsum("bhsk,bhsv->bhkv", kc * wg[..., None], vc)
        n_new = decay[..., None] * n + jnp.einsum("bhs,bhsk->bhk", wg, kc)
        return (C_new, n_new, m_new), h

    init = (jnp.zeros((B, H, Dh, Dh), jnp.float32), jnp.zeros((B, H, Dh), jnp.float32),
            jnp.zeros((B, H), jnp.float32))
    _, hs = lax.scan(step, init, (chunks(q), chunks(k), chunks(v), gchunks(log_i), gchunks(log_f)))
    return hs.transpose(1, 0, 3, 2, 4).reshape(B, S, H, Dh)


def forgetting_attention(q, k, v, log_f):
    B, S, H, Dh = q.shape
    nb = S // FOX_BLOCK
    c = jnp.cumsum(log_f, axis=1).transpose(0, 2, 1)
    kh = k.transpose(0, 2, 1, 3)
    vh = v.transpose(0, 2, 1, 3)
    qb = (q * (Dh ** -0.5)).reshape(B, nb, FOX_BLOCK, H, Dh).transpose(1, 0, 3, 2, 4)
    cb = c.reshape(B, H, nb, FOX_BLOCK).transpose(2, 0, 1, 3)
    qpos = jnp.arange(S).reshape(nb, FOX_BLOCK)
    kpos = jnp.arange(S)

    def one_block(args):
        qi, ci, pi = args
        logits = jnp.einsum("bhqd,bhkd->bhqk", qi, kh) + ci[..., None] - c[:, :, None, :]
        logits = jnp.where(pi[:, None] >= kpos[None, :], logits, -jnp.inf)
        p = jax.nn.softmax(logits, axis=-1)
        return jnp.einsum("bhqk,bhkd->bhqd", p, vh)

    o = lax.map(one_block, (qb, cb, qpos))
    return o.transpose(1, 0, 3, 2, 4).reshape(B, S, H * Dh)


def hybrid_mixer(h, w_in, mlstm_conv_w, mlstm_conv_b, mlstm_b_i, mlstm_b_f, mlstm_norm_w,
                 conf_conv_w, conf_conv_b, conf_ln_w, conf_ln_b, fox_b_f, w_out):
    B, S, _ = h.shape
    z = jnp.einsum("bsd,de->bse", h, w_in).astype(jnp.float32)
    m_qk, m_v, m_o, m_i, m_f, c_glu, f_q, f_k, f_v, f_f = jnp.split(z, IN_SPLIT_IDX, axis=-1)

    qk = jax.nn.silu(causal_dwconv(m_qk, mlstm_conv_w, mlstm_conv_b))
    mq, mk = jnp.split(qk, 2, axis=-1)
    hd = (B, S, MLSTM_HEADS, MLSTM_HEAD_DIM)
    hm = mlstm_chunkwise(mq.reshape(hd), mk.reshape(hd) * (MLSTM_HEAD_DIM ** -0.5), m_v.reshape(hd),
                         m_i + mlstm_b_i.astype(jnp.float32), m_f + mlstm_b_f.astype(jnp.float32))
    hm = hm * lax.rsqrt(jnp.mean(hm * hm, axis=-1, keepdims=True) + EPS)
    hm = hm.reshape(B, S, MLSTM_WIDTH) * mlstm_norm_w.astype(jnp.float32) * jax.nn.sigmoid(m_o)

    a, g = jnp.split(c_glu, 2, axis=-1)
    hc = causal_dwconv(a * jax.nn.sigmoid(g), conf_conv_w, conf_conv_b)
    hc = jax.nn.silu(layernorm_f32(hc, conf_ln_w, conf_ln_b))

    fd = (B, S, FOX_HEADS, FOX_HEAD_DIM)
    log_f = jax.nn.log_sigmoid(f_f + fox_b_f.astype(jnp.float32))
    hf = forgetting_attention(f_q.reshape(fd), f_k.reshape(fd), f_v.reshape(fd), log_f)

    cat = jnp.concatenate([hm, hc, hf], axis=-1).astype(h.dtype)
    return jnp.einsum("bse,ed->bsd", cat, w_out)


def memory_cross_attention(h, mem_n, w_q, w_kv, w_o):
    B, S, _ = h.shape
    q = jnp.einsum("bsd,de->bse", h, w_q).reshape(B, S, XATTN_HEADS, XATTN_HEAD_DIM)
    kv = jnp.einsum("bmd,de->bme", mem_n, w_kv)
    k, v = jnp.split(kv, 2, axis=-1)
    k = k.reshape(B, N_MEM, XATTN_HEADS, XATTN_HEAD_DIM)
    v = v.reshape(B, N_MEM, XATTN_HEADS, XATTN_HEAD_DIM)
    logits = jnp.einsum("bshd,bmhd->bhsm", q.astype(jnp.float32), k.astype(jnp.float32))
    p = jax.nn.softmax(logits * (XATTN_HEAD_DIM ** -0.5), axis=-1)
    o = jnp.einsum("bhsm,bmhd->bshd", p, v.astype(jnp.float32)).reshape(B, S, XATTN_WIDTH)
    return jnp.einsum("bse,ed->bsd", o.astype(h.dtype), w_o)


def swiglu(h, w_gate, w_up, w_down):
    a = jnp.einsum("bsd,df->bsf", h, w_gate)
    u = jnp.einsum("bsd,df->bsf", h, w_up)
    return jnp.einsum("bsf,fd->bsd", jax.nn.silu(a) * u, w_down)


def moe_swiglu(h, router_w, w_gate, w_up, w_down):
    logits = jnp.einsum("bsd,de->bse", h, router_w).astype(jnp.float32)
    top_v, top_i = lax.top_k(logits, TOP_K)
    top_w = jax.nn.softmax(top_v, axis=-1)
    gates = jnp.sum(jax.nn.one_hot(top_i, N_EXPERTS, dtype=jnp.float32) * top_w[..., None], axis=-2)
    y = jnp.zeros(h.shape, jnp.float32)
    for e in range(N_EXPERTS):
        y = y + gates[..., e:e + 1] * swiglu(h, w_gate[e], w_up[e], w_down[e]).astype(jnp.float32)
    return y.astype(h.dtype)


def setup_inputs(seed: int = 0) -> dict:
    key = jax.random.key(seed)
    ks = jax.random.split(key, 32)

    def nrm(k, shape, scale):
        return jax.random.normal(k, shape, jnp.float32) * scale

    D, L = D_MODEL, DEPTH
    return {
        "x": nrm(ks[0], (BATCH, SEQ, D), 1.0),
        "mem": nrm(ks[1], (BATCH, N_MEM, D), 1.0),
        "norm_mix_w": 1.0 + nrm(ks[2], (L, D), 0.02),
        "w_in": nrm(ks[3], (L, D, IN_WIDTH), D ** -0.5),
        "mlstm_conv_w": nrm(ks[4], (L, MLSTM_CONV, 2 * MLSTM_WIDTH), MLSTM_CONV ** -0.5),
        "mlstm_conv_b": nrm(ks[5], (L, 2 * MLSTM_WIDTH), 0.02),
        "mlstm_b_i": nrm(ks[6], (L, MLSTM_HEADS), 0.1),
        "mlstm_b_f": 3.0 + nrm(ks[7], (L, MLSTM_HEADS), 0.5),
        "mlstm_norm_w": 1.0 + nrm(ks[8], (L, MLSTM_WIDTH), 0.02),
        "conf_conv_w": nrm(ks[9], (L, CONF_KERNEL, CONF_CHANNELS), CONF_KERNEL ** -0.5),
        "conf_conv_b": nrm(ks[10], (L, CONF_CHANNELS), 0.02),
        "conf_ln_w": 1.0 + nrm(ks[11], (L, CONF_CHANNELS), 0.02),
        "conf_ln_b": nrm(ks[12], (L, CONF_CHANNELS), 0.02),
        "fox_b_f": 2.0 + nrm(ks[13], (L, FOX_HEADS), 0.5),
        "w_out": nrm(ks[14], (L, MIX_WIDTH, D), MIX_WIDTH ** -0.5),
        "norm_xattn_w": 1.0 + nrm(ks[15], (L, D), 0.02),
        "norm_mem_w": 1.0 + nrm(ks[16], (L, D), 0.02),
        "xattn_w_q": nrm(ks[17], (L, D, XATTN_WIDTH), D ** -0.5),
        "xattn_w_kv": nrm(ks[18], (L, D, 2 * XATTN_WIDTH), D ** -0.5),
        "xattn_w_o": nrm(ks[19], (L, XATTN_WIDTH, D), XATTN_WIDTH ** -0.5),
        "norm_ffn_w": 1.0 + nrm(ks[20], (L, D), 0.02),
        "ffn_w_gate": nrm(ks[21], (N_DENSE_LAYERS, D, D_FF), D ** -0.5),
        "ffn_w_up": nrm(ks[22], (N_DENSE_LAYERS, D, D_FF), D ** -0.5),
        "ffn_w_down": nrm(ks[23], (N_DENSE_LAYERS, D_FF, D), D_FF ** -0.5),
        "router_w": nrm(ks[24], (N_MOE_LAYERS, D, N_EXPERTS), D ** -0.5),
        "moe_w_gate": nrm(ks[25], (N_MOE_LAYERS, N_EXPERTS, D, EXPERT_FF), D ** -0.5),
        "moe_w_up": nrm(ks[26], (N_MOE_LAYERS, N_EXPERTS, D, EXPERT_FF), D ** -0.5),
        "moe_w_down": nrm(ks[27], (N_MOE_LAYERS, N_EXPERTS, EXPERT_FF, D), EXPERT_FF ** -0.5),
        "norm_final_w": 1.0 + nrm(ks[28], (D,), 0.02),
    }


def reference(x, mem, norm_mix_w, w_in, mlstm_conv_w, mlstm_conv_b, mlstm_b_i, mlstm_b_f,
              mlstm_norm_w, conf_conv_w, conf_conv_b, conf_ln_w, conf_ln_b, fox_b_f, w_out,
              norm_xattn_w, norm_mem_w, xattn_w_q, xattn_w_kv, xattn_w_o, norm_ffn_w,
              ffn_w_gate, ffn_w_up, ffn_w_down, router_w, moe_w_gate, moe_w_up, moe_w_down,
              norm_final_w):
    for l in range(DEPTH):
        h = rmsnorm(x, norm_mix_w[l])
        x = x + hybrid_mixer(h, w_in[l], mlstm_conv_w[l], mlstm_conv_b[l], mlstm_b_i[l], mlstm_b_f[l],
                             mlstm_norm_w[l], conf_conv_w[l], conf_conv_b[l], conf_ln_w[l],
                             conf_ln_b[l], fox_b_f[l], w_out[l])
        h = rmsnorm(x, norm_xattn_w[l])
        mem_n = rmsnorm(mem, norm_mem_w[l])
        x = x + memory_cross_attention(h, mem_n, xattn_w_q[l], xattn_w_kv[l], xattn_w_o[l])
        h = rmsnorm(x, norm_ffn_w[l])
        if l % 2 == 0:
            j = l // 2
            x = x + swiglu(h, ffn_w_gate[j], ffn_w_up[j], ffn_w_down[j])
        else:
            j = l // 2
            x = x + moe_swiglu(h, router_w[j], moe_w_gate[j], moe_w_up[j], moe_w_down[j])
    return rmsnorm(x, norm_final_w)
```

```python
import functools

import jax
import jax.numpy as jnp
from jax import lax
from jax.experimental import pallas as pl
from jax.experimental.pallas import tpu as pltpu

F32 = jnp.float32
BF16 = jnp.bfloat16
EPS = 1e-6

MLSTM_HEADS = 4
HEAD_DIM = 64
MLSTM_WIDTH = MLSTM_HEADS * HEAD_DIM
MLSTM_CONV = 4
CONF_CHANNELS = 256
CONF_KERNEL = 31
FOX_HEADS = 8
FOX_WIDTH = FOX_HEADS * HEAD_DIM
XATTN_HEADS = 4
XATTN_HEAD_DIM = 128
XATTN_WIDTH = XATTN_HEADS * XATTN_HEAD_DIM
N_EXPERTS = 8
LANES = 128

Z_QK = 0
Z_V = 512
Z_O = 768
Z_GLU = 1024
Z_SEQ_WIDTH = 1536
Z_FQ = 1536
Z_FK = 2048
Z_FV = 2560
Z_WIDTH = 3072
G_I = 0
G_F = 4
G_FOX = 8

TM_PROJ = 512
SEQ_CHUNK = 256
CONV_TAIL = 8
CONF_TAIL = 32
FOX_TQ = 512
FOX_TK = 512
TM_FFN = 256
FF_CHUNKS = (1024, 1024, 768)
TD_ROUTE = 256
VMEM_LIMIT = 56 * 1024 * 1024


def _cparams(sem, vmem=VMEM_LIMIT):
    return pltpu.CompilerParams(dimension_semantics=sem, vmem_limit_bytes=vmem)


def _sigmoid(x):
    return 1.0 / (1.0 + jnp.exp(-x))


def _log_sigmoid(x):
    return jnp.minimum(x, 0.0) - jnp.log(1.0 + jnp.exp(-jnp.abs(x)))


def _rmsnorm(x, w):
    ms = jnp.mean(x * x, axis=-1, keepdims=True)
    return x * lax.rsqrt(ms + EPS) * w


def _in_proj_kernel(x_ref, nw_ref, w_ref, wg_ref, z_ref, g_ref):
    h = _rmsnorm(x_ref[...], nw_ref[...]).astype(BF16)
    for n in range(0, Z_WIDTH, 512):
        z_ref[:, n:n + 512] = jnp.dot(h, w_ref[:, n:n + 512],
                                      preferred_element_type=F32).astype(BF16)
    g_ref[...] = jnp.dot(h, wg_ref[...], preferred_element_type=F32)


def _in_proj(x, norm_w, w_main, w_gate):
    t, d = x.shape
    tm = min(TM_PROJ, t)
    return pl.pallas_call(
        _in_proj_kernel,
        grid=(t // tm,),
        in_specs=[
            pl.BlockSpec((tm, d), lambda i: (i, 0)),
            pl.BlockSpec((1, d), lambda i: (0, 0)),
            pl.BlockSpec((d, Z_WIDTH), lambda i: (0, 0)),
            pl.BlockSpec((d, LANES), lambda i: (0, 0)),
        ],
        out_specs=[
            pl.BlockSpec((tm, Z_WIDTH), lambda i: (i, 0)),
            pl.BlockSpec((tm, LANES), lambda i: (i, 0)),
        ],
        out_shape=[jax.ShapeDtypeStruct((t, Z_WIDTH), BF16),
                   jax.ShapeDtypeStruct((t, LANES), F32)],
        compiler_params=_cparams(("parallel",)),
        name="in_proj",
    )(x, norm_w, w_main, w_gate)


def _seq_kernel(z_ref, g_ref, gb_ref, cw_ref, cb_ref, mnw_ref, ccw_ref, ccb_ref, lnw_ref, lnb_ref,
                out_ref, c_ref,
                qk_buf, u_buf, cstate, nstate, mstate, carry):
    L = z_ref.shape[0]
    W = MLSTM_WIDTH

    @pl.when(pl.program_id(1) == 0)
    def _():
        qk_buf[0:CONV_TAIL, :] = jnp.zeros((CONV_TAIL, 2 * W), F32)
        u_buf[0:CONF_TAIL, :] = jnp.zeros((CONF_TAIL, CONF_CHANNELS), F32)
        cstate[...] = jnp.zeros_like(cstate)
        nstate[...] = jnp.zeros_like(nstate)
        mstate[...] = jnp.zeros_like(mstate)
        carry[...] = jnp.zeros_like(carry)

    lane_g = lax.broadcasted_iota(jnp.int32, (1, LANES), 1)
    lane_w = lax.broadcasted_iota(jnp.int32, (1, W), 1)

    g = g_ref[...] + gb_ref[...]
    is_forget = (lane_g >= G_F) & (lane_g < G_FOX + FOX_HEADS)
    lsg = jnp.where(is_forget, _log_sigmoid(g), 0.0)
    row = lax.broadcasted_iota(jnp.int32, (L, L), 0)
    col = lax.broadcasted_iota(jnp.int32, (L, L), 1)
    causal = row >= col
    tri = causal.astype(F32)
    cs = jnp.dot(tri, lsg, preferred_element_type=F32, precision=lax.Precision.HIGHEST)
    c_all = cs + carry[...]
    c_ref[...] = c_all
    carry[...] = c_all[L - 1:L, :]

    qk_buf[CONV_TAIL:CONV_TAIL + L, :] = z_ref[:, Z_QK:Z_QK + 2 * W].astype(F32)
    qk = jnp.zeros((L, 2 * W), F32) + cb_ref[...]
    for j in range(MLSTM_CONV):
        off = CONV_TAIL - (MLSTM_CONV - 1) + j
        qk = qk + qk_buf[off:off + L, :] * cw_ref[j:j + 1, :]
    qk_buf[0:CONV_TAIL, :] = qk_buf[L:L + CONV_TAIL, :]
    qk = qk * _sigmoid(qk)
    q = qk[:, 0:W]
    k = qk[:, W:2 * W] * (HEAD_DIM ** -0.5)
    v = z_ref[:, Z_V:Z_V + W]
    q_b = q.astype(BF16)
    k_b = k.astype(BF16)

    cs_t = cs.T
    g_t = g.T
    m_prev = mstate[...]
    n_prev = nstate[...]
    c_prev = cstate[...]
    qn = q * n_prev
    q_c = jnp.dot(q_b, c_prev.astype(BF16), preferred_element_type=F32)

    num = jnp.zeros((L, W), F32)
    w_inter_l = jnp.zeros((L, W), F32)
    denom_l = jnp.ones((L, W), F32)
    wg_l = jnp.zeros((L, W), F32)
    decay_l = jnp.zeros((1, W), F32)
    m_new_row = m_prev
    for h in range(MLSTM_HEADS):
        hmask = (lane_w // HEAD_DIM) == h
        b_col = cs[:, G_F + h:G_F + h + 1]
        b_row = cs_t[G_F + h:G_F + h + 1, :]
        li_row = g_t[G_I + h:G_I + h + 1, :]
        li_col = g[:, G_I + h:G_I + h + 1]
        m_h = m_prev[:, h:h + 1]
        d_log = jnp.where(causal, b_col - b_row + li_row, -jnp.inf)
        inter = b_col + m_h
        m_t = jnp.maximum(jnp.max(d_log, axis=1, keepdims=True), inter)
        dmat = jnp.exp(d_log - m_t)
        q_h = jnp.where(hmask, q_b, jnp.zeros_like(q_b))
        s = lax.dot_general(q_h, k_b, (((1,), (1,)), ((), ())), preferred_element_type=F32) * dmat
        w_inter = jnp.exp(inter - m_t)
        pv = jnp.dot(s.astype(BF16), v, preferred_element_type=F32)
        num = jnp.where(hmask, pv, num)
        qn_h = jnp.sum(jnp.where(hmask, qn, 0.0), axis=1, keepdims=True)
        den = jnp.sum(s, axis=1, keepdims=True) + w_inter * qn_h
        dn = jnp.maximum(jnp.abs(den), jnp.exp(-m_t))
        w_inter_l = jnp.where(hmask, w_inter, w_inter_l)
        denom_l = jnp.where(hmask, dn, denom_l)
        b_tot = cs[L - 1:L, G_F + h:G_F + h + 1]
        g_h = b_tot - b_col + li_col
        m_new = jnp.maximum(b_tot + m_h, jnp.max(g_h, axis=0, keepdims=True))
        wg_l = jnp.where(hmask, jnp.exp(g_h - m_new), wg_l)
        decay_l = jnp.where(hmask, jnp.exp(b_tot + m_h - m_new), decay_l)
        m_new_row = jnp.where(lane_g == h, m_new, m_new_row)

    hout = (num + w_inter_l * q_c) / denom_l
    hsq = hout * hout
    rs_l = jnp.zeros((L, W), F32)
    for h in range(MLSTM_HEADS):
        hmask = (lane_w // HEAD_DIM) == h
        ms = jnp.sum(jnp.where(hmask, hsq, 0.0), axis=1, keepdims=True) * (1.0 / HEAD_DIM)
        rs_l = jnp.where(hmask, lax.rsqrt(ms + EPS), rs_l)
    o_gate = _sigmoid(z_ref[:, Z_O:Z_O + W].astype(F32))
    out_ref[:, 0:W] = (hout * rs_l * mnw_ref[...] * o_gate).astype(BF16)

    kw = k * wg_l
    upd = lax.dot_general(kw.astype(BF16), v, (((0,), (0,)), ((), ())), preferred_element_type=F32)
    rk = lax.broadcasted_iota(jnp.int32, (W, W), 0) // HEAD_DIM
    rv = lax.broadcasted_iota(jnp.int32, (W, W), 1) // HEAD_DIM
    cstate[...] = decay_l * c_prev + jnp.where(rk == rv, upd, 0.0)
    nstate[...] = decay_l * n_prev + jnp.sum(kw, axis=0, keepdims=True)
    mstate[...] = m_new_row

    a = z_ref[:, Z_GLU:Z_GLU + CONF_CHANNELS].astype(F32)
    gg = z_ref[:, Z_GLU + CONF_CHANNELS:Z_GLU + 2 * CONF_CHANNELS].astype(F32)
    u_buf[CONF_TAIL:CONF_TAIL + L, :] = a * _sigmoid(gg)
    hc = jnp.zeros((L, CONF_CHANNELS), F32) + ccb_ref[...]
    for j in range(CONF_KERNEL):
        off = CONF_TAIL - (CONF_KERNEL - 1) + j
        hc = hc + u_buf[off:off + L, :] * ccw_ref[j:j + 1, :]
    u_buf[0:CONF_TAIL, :] = u_buf[L:L + CONF_TAIL, :]
    mu = jnp.mean(hc, axis=1, keepdims=True)
    xc = hc - mu
    var = jnp.mean(xc * xc, axis=1, keepdims=True)
    y = xc * lax.rsqrt(var + EPS) * lnw_ref[...] + lnb_ref[...]
    out_ref[:, W:W + CONF_CHANNELS] = (y * _sigmoid(y)).astype(BF16)


def _seq_mix(z, gates, gate_bias, conv_w, conv_b, mnorm_w, cconv_w, cconv_b, ln_w, ln_b, batch):
    t = z.shape[0]
    s = t // batch
    L = min(SEQ_CHUNK, s)
    nc = s // L
    W = MLSTM_WIDTH
    full = lambda a: pl.BlockSpec(a.shape, lambda b, c: (0,) * a.ndim)
    return pl.pallas_call(
        _seq_kernel,
        grid=(batch, nc),
        in_specs=[
            pl.BlockSpec((L, Z_SEQ_WIDTH), lambda b, c: (b * nc + c, 0)),
            pl.BlockSpec((L, LANES), lambda b, c: (b * nc + c, 0)),
            full(gate_bias), full(conv_w), full(conv_b), full(mnorm_w),
            full(cconv_w), full(cconv_b), full(ln_w), full(ln_b),
        ],
        out_specs=[
            pl.BlockSpec((L, W + CONF_CHANNELS), lambda b, c: (b * nc + c, 0)),
            pl.BlockSpec((L, LANES), lambda b, c: (b * nc + c, 0)),
        ],
        out_shape=[jax.ShapeDtypeStruct((t, W + CONF_CHANNELS), BF16),
                   jax.ShapeDtypeStruct((t, LANES), F32)],
        scratch_shapes=[
            pltpu.VMEM((CONV_TAIL + L + CONV_TAIL, 2 * W), F32),
            pltpu.VMEM((CONF_TAIL + L + CONF_TAIL, CONF_CHANNELS), F32),
            pltpu.VMEM((W, W), F32),
            pltpu.VMEM((1, W), F32),
            pltpu.VMEM((1, LANES), F32),
            pltpu.VMEM((1, LANES), F32),
        ],
        compiler_params=_cparams(("arbitrary", "arbitrary")),
        name="seq_mix",
    )(z, gates, gate_bias, conv_w, conv_b, mnorm_w, cconv_w, cconv_b, ln_w, ln_b)


def _fox_kernel(q_ref, k_ref, v_ref, cc_ref, ct_ref, o_ref, m_sc, l_sc, acc_sc):
    tq = q_ref.shape[0]
    tk = tq
    p = pl.program_id(1)
    i = pl.program_id(2)
    lane = lax.broadcasted_iota(jnp.int32, (1, LANES), 1)
    q = q_ref[...]
    cc = cc_ref[...]
    qs, cis = [], []
    for hh in range(2):
        hmask = (lane // HEAD_DIM) == hh
        qs.append(jnp.where(hmask, q, jnp.zeros_like(q)) * jnp.asarray(HEAD_DIM ** -0.5, BF16))
        sel = lane == (G_FOX + 2 * p + hh)
        cis.append(jnp.sum(jnp.where(sel, cc, 0.0), axis=1, keepdims=True))
        m_sc[hh] = jnp.full((tq, 1), -jnp.inf, F32)
        l_sc[hh] = jnp.zeros((tq, 1), F32)
        acc_sc[hh] = jnp.zeros((tq, LANES), F32)

    def tile(j, diagonal):
        start = pl.multiple_of(j * tk, tk)
        kj = k_ref[pl.ds(start, tk), :]
        vj = v_ref[pl.ds(start, tk), :]
        for hh in range(2):
            cj = ct_ref[0, pl.ds(2 * p + hh, 1), pl.ds(start, tk)]
            s = lax.dot_general(qs[hh], kj, (((1,), (1,)), ((), ())), preferred_element_type=F32)
            s = s + (cis[hh] - cj)
            if diagonal:
                r = lax.broadcasted_iota(jnp.int32, (tq, tk), 0)
                c = lax.broadcasted_iota(jnp.int32, (tq, tk), 1)
                s = jnp.where(r >= c, s, -jnp.inf)
            m_old = m_sc[hh]
            m_new = jnp.maximum(m_old, jnp.max(s, axis=1, keepdims=True))
            alpha = jnp.exp(m_old - m_new)
            pm = jnp.exp(s - m_new)
            l_sc[hh] = alpha * l_sc[hh] + jnp.sum(pm, axis=1, keepdims=True)
            acc_sc[hh] = alpha * acc_sc[hh] + jnp.dot(pm.astype(BF16), vj, preferred_element_type=F32)
            m_sc[hh] = m_new

    def body(j, carry):
        tile(j, False)
        return carry

    lax.fori_loop(0, i, body, 0)
    tile(i, True)
    o0 = acc_sc[0] / l_sc[0]
    o1 = acc_sc[1] / l_sc[1]
    o_ref[...] = jnp.where(lane < HEAD_DIM, o0, o1).astype(BF16)


def _fox_attn(z, c_col, c_t, batch):
    t = z.shape[0]
    s = t // batch
    tq = min(FOX_TQ, s)
    nq = s // tq
    npairs = FOX_HEADS // 2
    qb, kb, vb = Z_FQ // LANES, Z_FK // LANES, Z_FV // LANES
    return pl.pallas_call(
        _fox_kernel,
        grid=(batch, npairs, nq),
        in_specs=[
            pl.BlockSpec((tq, LANES), lambda b, p, i: (b * nq + i, qb + p)),
            pl.BlockSpec((s, LANES), lambda b, p, i: (b, kb + p)),
            pl.BlockSpec((s, LANES), lambda b, p, i: (b, vb + p)),
            pl.BlockSpec((tq, LANES), lambda b, p, i: (b * nq + i, 0)),
            pl.BlockSpec((1, FOX_HEADS, s), lambda b, p, i: (b, 0, 0)),
        ],
        out_specs=pl.BlockSpec((tq, LANES), lambda b, p, i: (b * nq + i, p)),
        out_shape=jax.ShapeDtypeStruct((t, FOX_WIDTH), BF16),
        scratch_shapes=[
            pltpu.VMEM((2, tq, 1), F32),
            pltpu.VMEM((2, tq, 1), F32),
            pltpu.VMEM((2, tq, LANES), F32),
        ],
        compiler_params=_cparams(("parallel", "parallel", "arbitrary")),
        name="fox_attn",
    )(z, z, z, c_col, c_t)


def _mem_kv_kernel(mem_ref, nw_ref, w_ref, o_ref):
    h = _rmsnorm(mem_ref[0], nw_ref[...]).astype(BF16)
    o_ref[0] = jnp.dot(h, w_ref[...], preferred_element_type=F32).astype(BF16)


def _mem_kv(mem, norm_w, w_kv):
    b, m, d = mem.shape
    return pl.pallas_call(
        _mem_kv_kernel,
        grid=(b,),
        in_specs=[
            pl.BlockSpec((1, m, d), lambda i: (i, 0, 0)),
            pl.BlockSpec((1, d), lambda i: (0, 0)),
            pl.BlockSpec(w_kv.shape, lambda i: (0, 0)),
        ],
        out_specs=pl.BlockSpec((1, m, w_kv.shape[1]), lambda i: (i, 0, 0)),
        out_shape=jax.ShapeDtypeStruct((b, m, w_kv.shape[1]), BF16),
        compiler_params=_cparams(("parallel",)),
        name="mem_kv",
    )(mem, norm_w, w_kv)


def _post_mix_kernel(with_router, hmc_ref, hf_ref, x_ref, wout_ref, nxw_ref, wq_ref, kv_ref, wo_ref,
                     nfw_ref, *rest):
    if with_router:
        rw_ref, x2_ref, h3_ref, route_ref = rest
    else:
        x2_ref, h3_ref = rest
    half = MLSTM_WIDTH + CONF_CHANNELS
    x1 = (x_ref[...]
          + jnp.dot(hmc_ref[...], wout_ref[0:half, :], preferred_element_type=F32)
          + jnp.dot(hf_ref[...], wout_ref[half:, :], preferred_element_type=F32))
    h2 = _rmsnorm(x1, nxw_ref[...]).astype(BF16)
    q = jnp.dot(h2, wq_ref[...], preferred_element_type=F32) * (XATTN_HEAD_DIM ** -0.5)
    q = q.astype(BF16)
    kv = kv_ref[0]
    outs = []
    for h in range(XATTN_HEADS):
        lo = h * XATTN_HEAD_DIM
        kh = kv[:, lo:lo + XATTN_HEAD_DIM]
        vh = kv[:, XATTN_WIDTH + lo:XATTN_WIDTH + lo + XATTN_HEAD_DIM]
        s = lax.dot_general(q[:, lo:lo + XATTN_HEAD_DIM], kh, (((1,), (1,)), ((), ())),
                            preferred_element_type=F32)
        s = s - jnp.max(s, axis=1, keepdims=True)
        e = jnp.exp(s)
        pm = e / jnp.sum(e, axis=1, keepdims=True)
        outs.append(jnp.dot(pm.astype(BF16), vh, preferred_element_type=F32).astype(BF16))
    o = jnp.concatenate(outs, axis=1)
    x2 = x1 + jnp.dot(o, wo_ref[...], preferred_element_type=F32)
    x2_ref[...] = x2
    h3 = _rmsnorm(x2, nfw_ref[...])
    h3_ref[...] = h3.astype(h3_ref.dtype)
    if with_router:
        lane = lax.broadcasted_iota(jnp.int32, (1, LANES), 1)
        lane_f = lane.astype(F32)
        logits = jnp.dot(h3, rw_ref[...], preferred_element_type=F32, precision=lax.Precision.HIGHEST)
        logits = jnp.where(lane < N_EXPERTS, logits, -jnp.inf)
        m1 = jnp.max(logits, axis=1, keepdims=True)
        i1 = jnp.min(jnp.where(logits == m1, lane_f, float(LANES)), axis=1, keepdims=True)
        rest_l = jnp.where(lane_f == i1, -jnp.inf, logits)
        m2 = jnp.max(rest_l, axis=1, keepdims=True)
        i2 = jnp.min(jnp.where(rest_l == m2, lane_f, float(LANES)), axis=1, keepdims=True)
        e2 = jnp.exp(m2 - m1)
        w1 = 1.0 / (1.0 + e2)
        w2 = e2 * w1
        route = jnp.where(lane == 0, i1, 0.0)
        route = jnp.where(lane == 1, i2, route)
        route = jnp.where(lane == 2, w1, route)
        route = jnp.where(lane == 3, w2, route)
        route_ref[...] = route


def _post_mix(hmc, hf, x, w_out, nx_w, w_q, kv, w_o, nf_w, router_w, batch):
    t, d = x.shape
    tm = min(TM_PROJ, t // batch)
    per_b = (t // batch) // tm
    with_router = router_w is not None
    const = lambda a: pl.BlockSpec(a.shape, lambda i: (0,) * a.ndim)
    in_specs = [
        pl.BlockSpec((tm, hmc.shape[1]), lambda i: (i, 0)),
        pl.BlockSpec((tm, hf.shape[1]), lambda i: (i, 0)),
        pl.BlockSpec((tm, d), lambda i: (i, 0)),
        const(w_out), const(nx_w), const(w_q),
        pl.BlockSpec((1,) + kv.shape[1:], lambda i: (i // per_b, 0, 0)),
        const(w_o), const(nf_w),
    ]
    args = [hmc, hf, x, w_out, nx_w, w_q, kv, w_o, nf_w]
    out_specs = [pl.BlockSpec((tm, d), lambda i: (i, 0)), pl.BlockSpec((tm, d), lambda i: (i, 0))]
    out_shape = [jax.ShapeDtypeStruct((t, d), F32),
                 jax.ShapeDtypeStruct((t, d), F32 if with_router else BF16)]
    if with_router:
        in_specs.append(const(router_w))
        args.append(router_w)
        out_specs.append(pl.BlockSpec((tm, LANES), lambda i: (i, 0)))
        out_shape.append(jax.ShapeDtypeStruct((t, LANES), F32))
    return pl.pallas_call(
        functools.partial(_post_mix_kernel, with_router),
        grid=(t // tm,),
        in_specs=in_specs, out_specs=out_specs, out_shape=out_shape,
        compiler_params=_cparams(("parallel",)),
        name="post_mix_router" if with_router else "post_mix",
    )(*args)


def _ffn_kernel(with_residual, te_ref, nt_ref, x_ref, wg_ref, wu_ref, wd_ref, *rest):
    if with_residual:
        res_ref, y_ref = rest
    else:
        (y_ref,) = rest

    @pl.when(pl.program_id(0) < nt_ref[0])
    def _():
        xb = x_ref[...].astype(BF16)
        acc = res_ref[...] if with_residual else None
        lo = 0
        for fc in FF_CHUNKS:
            a = jnp.dot(xb, wg_ref[0, :, lo:lo + fc], preferred_element_type=F32)
            u = jnp.dot(xb, wu_ref[0, :, lo:lo + fc], preferred_element_type=F32)
            hcur = (a * _sigmoid(a) * u).astype(BF16)
            part = jnp.dot(hcur, wd_ref[0, lo:lo + fc, :], preferred_element_type=F32)
            acc = part if acc is None else acc + part
            lo += fc
        y_ref[...] = acc

    @pl.when(pl.program_id(0) >= nt_ref[0])
    def _():
        y_ref[...] = jnp.zeros_like(y_ref)


def _ffn(x, w_gate, w_up, w_down, tile_expert, n_tiles, residual=None):
    rows, d = x.shape
    tm = min(TM_FFN, rows)
    nt = rows // tm
    ff = w_gate.shape[2]
    assert sum(FF_CHUNKS) == ff
    with_residual = residual is not None

    def row_map(i, te, ntl):
        return (jnp.minimum(i, ntl[0] - 1), 0)

    def w_map(i, te, ntl):
        return (te[jnp.minimum(i, ntl[0] - 1)], 0, 0)

    in_specs = [
        pl.BlockSpec((tm, d), row_map),
        pl.BlockSpec((1, d, ff), w_map),
        pl.BlockSpec((1, d, ff), w_map),
        pl.BlockSpec((1, ff, d), w_map),
    ]
    args = [x, w_gate, w_up, w_down]
    if with_residual:
        in_specs.append(pl.BlockSpec((tm, d), row_map))
        args.append(residual)
    return pl.pallas_call(
        functools.partial(_ffn_kernel, with_residual),
        grid_spec=pltpu.PrefetchScalarGridSpec(
            num_scalar_prefetch=2,
            grid=(nt,),
            in_specs=in_specs,
            out_specs=pl.BlockSpec((tm, d), lambda i, te, ntl: (i, 0)),
        ),
        out_shape=jax.ShapeDtypeStruct((rows, d), F32),
        compiler_params=_cparams(("arbitrary",)),
        name="ffn_dense" if with_residual else "ffn_experts",
    )(tile_expert, n_tiles, *args)


def _plan_kernel(route_ref, rank_ref, cnt_ref, carry):
    tp = route_ref.shape[0]

    @pl.when(pl.program_id(0) == 0)
    def _():
        carry[...] = jnp.zeros_like(carry)

    lane_i = lax.broadcasted_iota(jnp.int32, (1, LANES), 1)
    lane = lane_i.astype(F32)
    route = route_ref[...]
    i1 = route[:, 0:1]
    i2 = route[:, 1:2]
    onehot = (lane == i1).astype(F32) + (lane == i2).astype(F32)
    row = lax.broadcasted_iota(jnp.int32, (tp, tp), 0)
    col = lax.broadcasted_iota(jnp.int32, (tp, tp), 1)
    strict = (row > col).astype(BF16)
    before = jnp.dot(strict, onehot.astype(BF16), preferred_element_type=F32) + carry[...]
    r1 = jnp.sum(jnp.where(lane == i1, before, 0.0), axis=1, keepdims=True)
    r2 = jnp.sum(jnp.where(lane == i2, before, 0.0), axis=1, keepdims=True)
    rank_ref[...] = jnp.where(lane_i == 0, r1, jnp.where(lane_i == 1, r2, 0.0))
    total = carry[...] + jnp.sum(onehot, axis=0, keepdims=True)
    carry[...] = total
    cnt_ref[...] = total


def _plan(route):
    t = route.shape[0]
    tp = min(512, t)
    return pl.pallas_call(
        _plan_kernel,
        grid=(t // tp,),
        in_specs=[pl.BlockSpec((tp, LANES), lambda i: (i, 0))],
        out_specs=[pl.BlockSpec((tp, LANES), lambda i: (i, 0)),
                   pl.BlockSpec((1, LANES), lambda i: (0, 0))],
        out_shape=[jax.ShapeDtypeStruct((t, LANES), F32), jax.ShapeDtypeStruct((1, LANES), F32)],
        scratch_shapes=[pltpu.VMEM((1, LANES), F32)],
        compiler_params=_cparams(("arbitrary",)),
        name="route_plan",
    )(route)


def _row_copy(src, src_row, dst, dst_row, sem):
    return pltpu.make_async_copy(src.at[pl.ds(src_row, 1), :], dst.at[pl.ds(dst_row, 1), :], sem)


def _dispatch_kernel(pos_ref, h_ref, xs_in_ref, xs_ref, sem):
    del xs_in_ref
    td = h_ref.shape[0]

    def start(r, carry):
        _row_copy(h_ref, r, xs_ref, pos_ref[0, 0, 2 * r], sem).start()
        _row_copy(h_ref, r, xs_ref, pos_ref[0, 0, 2 * r + 1], sem).start()
        return carry

    def wait(r, carry):
        _row_copy(h_ref, r, xs_ref, pos_ref[0, 0, 2 * r], sem).wait()
        _row_copy(h_ref, r, xs_ref, pos_ref[0, 0, 2 * r + 1], sem).wait()
        return carry

    lax.fori_loop(0, td, start, 0)
    lax.fori_loop(0, td, wait, 0)


def _dispatch(h3, pos, xs_zero):
    t, d = h3.shape
    td = min(TD_ROUTE, t)
    nt = t // td
    pos3 = pos.reshape(nt, 1, 2 * td)
    return pl.pallas_call(
        _dispatch_kernel,
        grid=(nt,),
        in_specs=[
            pl.BlockSpec((1, 1, 2 * td), lambda i: (i, 0, 0), memory_space=pltpu.SMEM),
            pl.BlockSpec((td, d), lambda i: (i, 0)),
            pl.BlockSpec(memory_space=pl.ANY),
        ],
        out_specs=pl.BlockSpec(memory_space=pl.ANY),
        out_shape=jax.ShapeDtypeStruct(xs_zero.shape, xs_zero.dtype),
        scratch_shapes=[pltpu.SemaphoreType.DMA],
        input_output_aliases={2: 0},
        compiler_params=_cparams(("arbitrary",)),
        name="dispatch",
    )(pos3, h3, xs_zero)


def _combine_kernel(pos_ref, x_ref, route_ref, nw_ref, ys_ref, o_ref, buf, sem):
    td = x_ref.shape[0]

    def start(r, carry):
        _row_copy(ys_ref, pos_ref[0, 0, 2 * r], buf.at[0], r, sem).start()
        _row_copy(ys_ref, pos_ref[0, 0, 2 * r + 1], buf.at[1], r, sem).start()
        return carry

    def wait(r, carry):
        _row_copy(ys_ref, pos_ref[0, 0, 2 * r], buf.at[0], r, sem).wait()
        _row_copy(ys_ref, pos_ref[0, 0, 2 * r + 1], buf.at[1], r, sem).wait()
        return carry

    lax.fori_loop(0, td, start, 0)
    lax.fori_loop(0, td, wait, 0)
    route = route_ref[...]
    x = x_ref[...] + route[:, 2:3] * buf[0] + route[:, 3:4] * buf[1]
    o_ref[...] = _rmsnorm(x, nw_ref[...])


def _combine(x2, route, pos, ys, norm_w):
    t, d = x2.shape
    td = min(TD_ROUTE, t)
    nt = t // td
    pos3 = pos.reshape(nt, 1, 2 * td)
    return pl.pallas_call(
        _combine_kernel,
        grid=(nt,),
        in_specs=[
            pl.BlockSpec((1, 1, 2 * td), lambda i: (i, 0, 0), memory_space=pltpu.SMEM),
            pl.BlockSpec((td, d), lambda i: (i, 0)),
            pl.BlockSpec((td, LANES), lambda i: (i, 0)),
            pl.BlockSpec((1, d), lambda i: (0, 0)),
            pl.BlockSpec(memory_space=pl.ANY),
        ],
        out_specs=pl.BlockSpec((td, d), lambda i: (i, 0)),
        out_shape=jax.ShapeDtypeStruct((t, d), F32),
        scratch_shapes=[pltpu.VMEM((2, td, d), F32), pltpu.SemaphoreType.DMA],
        compiler_params=_cparams(("arbitrary",)),
        name="combine",
    )(pos3, x2, route, norm_w, ys)


def _split_w_in(w_in):
    mw, cw, fw = MLSTM_WIDTH, CONF_CHANNELS, FOX_WIDTH
    sizes = (2 * mw, mw, mw, MLSTM_HEADS, MLSTM_HEADS, 2 * cw, fw, fw, fw, FOX_HEADS)
    parts, off = [], 0
    for sz in sizes:
        parts.append(w_in[:, off:off + sz])
        off += sz
    m_qk, m_v, m_o, m_i, m_f, c_glu, f_q, f_k, f_v, f_f = parts
    w_main = jnp.concatenate([m_qk, m_v, m_o, c_glu, f_q, f_k, f_v], axis=1).astype(BF16)
    w_gate = jnp.concatenate([m_i, m_f, f_f], axis=1)
    w_gate = jnp.pad(w_gate, ((0, 0), (0, LANES - w_gate.shape[1]))).astype(BF16)
    return w_main, w_gate


def _gate_bias(b_i, b_f, fox_b):
    gb = jnp.concatenate([b_i, b_f, fox_b]).astype(F32)
    return jnp.pad(gb, (0, LANES - gb.shape[0])).reshape(1, LANES)


def _route_tables(route, rank, counts, tm, n_tiles_max):
    cnt = counts[0, :N_EXPERTS].astype(jnp.int32)
    tiles = (cnt + tm - 1) // tm
    tile_end = jnp.cumsum(tiles)
    row_off = (tile_end - tiles) * tm
    idx = route[:, 0:2].astype(jnp.int32)
    pos = row_off[idx] + rank[:, 0:2].astype(jnp.int32)
    tile_ids = jnp.arange(n_tiles_max, dtype=jnp.int32)
    tile_expert = jnp.sum(tile_ids[:, None] >= tile_end[None, :], axis=1).astype(jnp.int32)
    tile_expert = jnp.minimum(tile_expert, N_EXPERTS - 1)
    n_tiles = tile_end[-1:].astype(jnp.int32)
    return pos.reshape(-1), tile_expert, n_tiles


def kernel(x, mem, norm_mix_w, w_in, mlstm_conv_w, mlstm_conv_b, mlstm_b_i, mlstm_b_f, mlstm_norm_w,
           conf_conv_w, conf_conv_b, conf_ln_w, conf_ln_b, fox_b_f, w_out, norm_xattn_w, norm_mem_w,
           xattn_w_q, xattn_w_kv, xattn_w_o, norm_ffn_w, ffn_w_gate, ffn_w_up, ffn_w_down, router_w,
           moe_w_gate, moe_w_up, moe_w_down, norm_final_w):
    batch, seq, d = x.shape
    depth = w_in.shape[0]
    t = batch * seq
    xf = x.reshape(t, d)
    row = lambda a: a.reshape(1, -1).astype(F32)
    out = None
    for l in range(depth):
        w_main, w_gate = _split_w_in(w_in[l])
        z, gates = _in_proj(xf, row(norm_mix_w[l]), w_main, w_gate)
        hmc, c_col = _seq_mix(
            z, gates, _gate_bias(mlstm_b_i[l], mlstm_b_f[l], fox_b_f[l]),
            mlstm_conv_w[l].astype(F32), row(mlstm_conv_b[l]), row(mlstm_norm_w[l]),
            conf_conv_w[l].astype(F32), row(conf_conv_b[l]), row(conf_ln_w[l]), row(conf_ln_b[l]),
            batch)
        c_t = c_col.reshape(batch, seq, LANES)[:, :, G_FOX:G_FOX + FOX_HEADS].transpose(0, 2, 1)
        hf = _fox_attn(z, c_col, c_t, batch)
        kv = _mem_kv(mem, row(norm_mem_w[l]), xattn_w_kv[l].astype(BF16))
        dense = l % 2 == 0
        j = l // 2
        rw = None
        if not dense:
            rw = jnp.pad(router_w[j].astype(F32), ((0, 0), (0, LANES - N_EXPERTS)))
        res = _post_mix(hmc, hf, xf, w_out[l].astype(BF16), row(norm_xattn_w[l]),
                        xattn_w_q[l].astype(BF16), kv, xattn_w_o[l].astype(BF16),
                        row(norm_ffn_w[l]), rw, batch)
        if dense:
            x2, h3 = res
            nt = t // min(TM_FFN, t)
            xf = _ffn(h3, ffn_w_gate[j][None].astype(BF16), ffn_w_up[j][None].astype(BF16),
                      ffn_w_down[j][None].astype(BF16), jnp.zeros((nt,), jnp.int32),
                      jnp.full((1,), nt, jnp.int32), residual=x2)
            if l == depth - 1:
                raise NotImplementedError("final norm after a dense layer")
        else:
            x2, h3, route = res
            rank, counts = _plan(route)
            tm = min(TM_FFN, t)
            n_tiles_max = (2 * t) // tm + N_EXPERTS
            pos, tile_expert, n_tiles = _route_tables(route, rank, counts, tm, n_tiles_max)
            xs = _dispatch(h3, pos, jnp.zeros((n_tiles_max * tm, d), F32))
            ys = _ffn(xs, moe_w_gate[j].astype(BF16), moe_w_up[j].astype(BF16),
                      moe_w_down[j].astype(BF16), tile_expert, n_tiles)
            assert l == depth - 1
            out = _combine(x2, route, pos, ys, row(norm_final_w))
    return out.reshape(batch, seq, d)
```

```python
import functools

import jax
import jax.numpy as jnp
from jax import lax
from jax.experimental import pallas as pl
from jax.experimental.pallas import tpu as pltpu

F32 = jnp.float32
BF16 = jnp.bfloat16
EPS = 1e-6

MLSTM_HEADS = 4
HEAD_DIM = 64
MLSTM_WIDTH = MLSTM_HEADS * HEAD_DIM
MLSTM_CONV = 4
CONF_CHANNELS = 256
CONF_KERNEL = 31
FOX_HEADS = 8
FOX_WIDTH = FOX_HEADS * HEAD_DIM
XATTN_HEADS = 4
XATTN_HEAD_DIM = 128
XATTN_WIDTH = XATTN_HEADS * XATTN_HEAD_DIM
N_EXPERTS = 8
LANES = 128

Z_QK = 0
Z_V = 512
Z_O = 768
Z_GLU = 1024
Z_SEQ_WIDTH = 1536
Z_FQ = 1536
Z_FK = 2048
Z_FV = 2560
Z_WIDTH = 3072
G_I = 0
G_F = 4
G_FOX = 8

TM_PROJ = 512
SEQ_CHUNK = 256
CONV_TAIL = 8
CONF_TAIL = 32
FOX_TQ = 512
FOX_TK = 512
TM_FFN = 256
FF_CHUNKS = (1024, 1024, 768)
TD_ROUTE = 256
VMEM_LIMIT = 56 * 1024 * 1024


def _cparams(sem, vmem=VMEM_LIMIT):
    return pltpu.CompilerParams(dimension_semantics=sem, vmem_limit_bytes=vmem)


def _sigmoid(x):
    return 1.0 / (1.0 + jnp.exp(-x))


def _log_sigmoid(x):
    return jnp.minimum(x, 0.0) - jnp.log(1.0 + jnp.exp(-jnp.abs(x)))


def _rmsnorm(x, w):
    ms = jnp.mean(x * x, axis=-1, keepdims=True)
    return x * lax.rsqrt(ms + EPS) * w


def _in_proj_kernel(x_ref, nw_ref, w_ref, wg_ref, z_ref, g_ref):
    h = _rmsnorm(x_ref[...], nw_ref[...]).astype(BF16)
    for n in range(0, Z_WIDTH, 512):
        z_ref[:, n:n + 512] = jnp.dot(h, w_ref[:, n:n + 512],
                                      preferred_element_type=F32).astype(BF16)
    g_ref[...] = jnp.dot(h, wg_ref[...], preferred_element_type=F32)


def _in_proj(x, norm_w, w_main, w_gate):
    t, d = x.shape
    tm = min(TM_PROJ, t)
    return pl.pallas_call(
        _in_proj_kernel,
        grid=(t // tm,),
        in_specs=[
            pl.BlockSpec((tm, d), lambda i: (i, 0)),
            pl.BlockSpec((1, d), lambda i: (0, 0)),
            pl.BlockSpec((d, Z_WIDTH), lambda i: (0, 0)),
            pl.BlockSpec((d, LANES), lambda i: (0, 0)),
        ],
        out_specs=[
            pl.BlockSpec((tm, Z_WIDTH), lambda i: (i, 0)),
            pl.BlockSpec((tm, LANES), lambda i: (i, 0)),
        ],
        out_shape=[jax.ShapeDtypeStruct((t, Z_WIDTH), BF16),
                   jax.ShapeDtypeStruct((t, LANES), F32)],
        compiler_params=_cparams(("parallel",)),
        name="in_proj",
    )(x, norm_w, w_main, w_gate)


def _seq_kernel(z_ref, g_ref, gb_ref, cw_ref, cb_ref, mnw_ref, ccw_ref, ccb_ref, lnw_ref, lnb_ref,
                out_ref, c_ref,
                qk_buf, u_buf, cstate, nstate, mstate, carry):
    L = z_ref.shape[0]
    W = MLSTM_WIDTH

    @pl.when(pl.program_id(1) == 0)
    def _():
        qk_buf[0:CONV_TAIL, :] = jnp.zeros((CONV_TAIL, 2 * W), F32)
        u_buf[0:CONF_TAIL, :] = jnp.zeros((CONF_TAIL, CONF_CHANNELS), F32)
        cstate[...] = jnp.zeros_like(cstate)
        nstate[...] = jnp.zeros_like(nstate)
        mstate[...] = jnp.zeros_like(mstate)
        carry[...] = jnp.zeros_like(carry)

    lane_g = lax.broadcasted_iota(jnp.int32, (1, LANES), 1)
    lane_w = lax.broadcasted_iota(jnp.int32, (1, W), 1)

    g = g_ref[...] + gb_ref[...]
    is_forget = (lane_g >= G_F) & (lane_g < G_FOX + FOX_HEADS)
    lsg = jnp.where(is_forget, _log_sigmoid(g), 0.0)
    row = lax.broadcasted_iota(jnp.int32, (L, L), 0)
    col = lax.broadcasted_iota(jnp.int32, (L, L), 1)
    causal = row >= col
    tri = causal.astype(F32)
    cs = jnp.dot(tri, lsg, preferred_element_type=F32, precision=lax.Precision.HIGHEST)
    c_all = cs + carry[...]
    c_ref[...] = c_all * 1.4426950408889634
    carry[...] = c_all[L - 1:L, :]

    qk_buf[CONV_TAIL:CONV_TAIL + L, :] = z_ref[:, Z_QK:Z_QK + 2 * W].astype(F32)
    qk = jnp.zeros((L, 2 * W), F32) + cb_ref[...]
    for j in range(MLSTM_CONV):
        off = CONV_TAIL - (MLSTM_CONV - 1) + j
        qk = qk + qk_buf[off:off + L, :] * cw_ref[j:j + 1, :]
    qk_buf[0:CONV_TAIL, :] = qk_buf[L:L + CONV_TAIL, :]
    qk = qk * _sigmoid(qk)
    q = qk[:, 0:W]
    k = qk[:, W:2 * W] * (HEAD_DIM ** -0.5)
    v = z_ref[:, Z_V:Z_V + W]
    q_b = q.astype(BF16)
    k_b = k.astype(BF16)

    cs_t = cs.T
    g_t = g.T
    m_prev = mstate[...]
    n_prev = nstate[...]
    c_prev = cstate[...]
    qn = q * n_prev
    q_c = jnp.dot(q_b, c_prev.astype(BF16), preferred_element_type=F32)

    num = jnp.zeros((L, W), F32)
    w_inter_l = jnp.zeros((L, W), F32)
    denom_l = jnp.ones((L, W), F32)
    wg_l = jnp.zeros((L, W), F32)
    decay_l = jnp.zeros((1, W), F32)
    m_new_row = m_prev
    for h in range(MLSTM_HEADS):
        hmask = (lane_w // HEAD_DIM) == h
        b_col = cs[:, G_F + h:G_F + h + 1]
        b_row = cs_t[G_F + h:G_F + h + 1, :]
        li_row = g_t[G_I + h:G_I + h + 1, :]
        li_col = g[:, G_I + h:G_I + h + 1]
        m_h = m_prev[:, h:h + 1]
        d_log = jnp.where(causal, b_col - b_row + li_row, -jnp.inf)
        inter = b_col + m_h
        m_t = jnp.maximum(jnp.max(d_log, axis=1, keepdims=True), inter)
        dmat = jnp.exp(d_log - m_t)
        q_h = jnp.where(hmask, q_b, jnp.zeros_like(q_b))
        s = lax.dot_general(q_h, k_b, (((1,), (1,)), ((), ())), preferred_element_type=F32) * dmat
        w_inter = jnp.exp(inter - m_t)
        pv = jnp.dot(s.astype(BF16), v, preferred_element_type=F32)
        num = jnp.where(hmask, pv, num)
        qn_h = jnp.sum(jnp.where(hmask, qn, 0.0), axis=1, keepdims=True)
        den = jnp.sum(s, axis=1, keepdims=True) + w_inter * qn_h
        dn = jnp.maximum(jnp.abs(den), jnp.exp(-m_t))
        w_inter_l = jnp.where(hmask, w_inter, w_inter_l)
        denom_l = jnp.where(hmask, dn, denom_l)
        b_tot = cs[L - 1:L, G_F + h:G_F + h + 1]
        g_h = b_tot - b_col + li_col
        m_new = jnp.maximum(b_tot + m_h, jnp.max(g_h, axis=0, keepdims=True))
        wg_l = jnp.where(hmask, jnp.exp(g_h - m_new), wg_l)
        decay_l = jnp.where(hmask, jnp.exp(b_tot + m_h - m_new), decay_l)
        m_new_row = jnp.where(lane_g == h, m_new, m_new_row)

    hout = (num + w_inter_l * q_c) / denom_l
    hsq = hout * hout
    rs_l = jnp.zeros((L, W), F32)
    for h in range(MLSTM_HEADS):
        hmask = (lane_w // HEAD_DIM) == h
        ms = jnp.sum(jnp.where(hmask, hsq, 0.0), axis=1, keepdims=True) * (1.0 / HEAD_DIM)
        rs_l = jnp.where(hmask, lax.rsqrt(ms + EPS), rs_l)
    o_gate = _sigmoid(z_ref[:, Z_O:Z_O + W].astype(F32))
    out_ref[:, 0:W] = (hout * rs_l * mnw_ref[...] * o_gate).astype(BF16)

    kw = k * wg_l
    upd = lax.dot_general(kw.astype(BF16), v, (((0,), (0,)), ((), ())), preferred_element_type=F32)
    rk = lax.broadcasted_iota(jnp.int32, (W, W), 0) // HEAD_DIM
    rv = lax.broadcasted_iota(jnp.int32, (W, W), 1) // HEAD_DIM
    cstate[...] = decay_l * c_prev + jnp.where(rk == rv, upd, 0.0)
    nstate[...] = decay_l * n_prev + jnp.sum(kw, axis=0, keepdims=True)
    mstate[...] = m_new_row

    a = z_ref[:, Z_GLU:Z_GLU + CONF_CHANNELS].astype(F32)
    gg = z_ref[:, Z_GLU + CONF_CHANNELS:Z_GLU + 2 * CONF_CHANNELS].astype(F32)
    u_buf[CONF_TAIL:CONF_TAIL + L, :] = a * _sigmoid(gg)
    hc = jnp.zeros((L, CONF_CHANNELS), F32) + ccb_ref[...]
    for j in range(CONF_KERNEL):
        off = CONF_TAIL - (CONF_KERNEL - 1) + j
        hc = hc + u_buf[off:off + L, :] * ccw_ref[j:j + 1, :]
    u_buf[0:CONF_TAIL, :] = u_buf[L:L + CONF_TAIL, :]
    mu = jnp.mean(hc, axis=1, keepdims=True)
    xc = hc - mu
    var = jnp.mean(xc * xc, axis=1, keepdims=True)
    y = xc * lax.rsqrt(var + EPS) * lnw_ref[...] + lnb_ref[...]
    out_ref[:, W:W + CONF_CHANNELS] = (y * _sigmoid(y)).astype(BF16)


def _seq_mix(z, gates, gate_bias, conv_w, conv_b, mnorm_w, cconv_w, cconv_b, ln_w, ln_b, batch):
    t = z.shape[0]
    s = t // batch
    L = min(SEQ_CHUNK, s)
    nc = s // L
    W = MLSTM_WIDTH
    full = lambda a: pl.BlockSpec(a.shape, lambda b, c: (0,) * a.ndim)
    return pl.pallas_call(
        _seq_kernel,
        grid=(batch, nc),
        in_specs=[
            pl.BlockSpec((L, Z_SEQ_WIDTH), lambda b, c: (b * nc + c, 0)),
            pl.BlockSpec((L, LANES), lambda b, c: (b * nc + c, 0)),
            full(gate_bias), full(conv_w), full(conv_b), full(mnorm_w),
            full(cconv_w), full(cconv_b), full(ln_w), full(ln_b),
        ],
        out_specs=[
            pl.BlockSpec((L, W + CONF_CHANNELS), lambda b, c: (b * nc + c, 0)),
            pl.BlockSpec((L, LANES), lambda b, c: (b * nc + c, 0)),
        ],
        out_shape=[jax.ShapeDtypeStruct((t, W + CONF_CHANNELS), BF16),
                   jax.ShapeDtypeStruct((t, LANES), F32)],
        scratch_shapes=[
            pltpu.VMEM((CONV_TAIL + L + CONV_TAIL, 2 * W), F32),
            pltpu.VMEM((CONF_TAIL + L + CONF_TAIL, CONF_CHANNELS), F32),
            pltpu.VMEM((W, W), F32),
            pltpu.VMEM((1, W), F32),
            pltpu.VMEM((1, LANES), F32),
            pltpu.VMEM((1, LANES), F32),
        ],
        compiler_params=_cparams(("arbitrary", "arbitrary")),
        name="seq_mix",
    )(z, gates, gate_bias, conv_w, conv_b, mnorm_w, cconv_w, cconv_b, ln_w, ln_b)


AUG_NEG_A = 0
AUG_ONE = 3


def _split3(x):
    hi = x.astype(BF16).astype(F32)
    mid = (x - hi).astype(BF16).astype(F32)
    lo = (x - hi - mid).astype(BF16).astype(F32)
    return hi, mid, lo


def _fox_prep_kernel(zq_ref, zk_ref, zv_ref, c2_ref, ka_ref, qta_ref, vta_ref):
    tp = zq_ref.shape[0]
    c2 = c2_ref[...]
    a = c2 - c2[0:1, :]
    c2_t = c2.T
    b_t = c2_t - c2_t[:, 0:1]
    lane = lax.broadcasted_iota(jnp.int32, (1, LANES), 1)
    row = lax.broadcasted_iota(jnp.int32, (LANES, 1), 0)
    q_scale = (HEAD_DIM ** -0.5) * 1.4426950408889634
    for p in range(FOX_HEADS // 2):
        sl = slice(p * LANES, (p + 1) * LANES)
        kp = zk_ref[:, sl].astype(F32)
        q_t = (zq_ref[:, sl].astype(F32) * q_scale).T
        v_t = zv_ref[:, sl].astype(F32).T
        for hh in range(2):
            h = 2 * p + hh
            own_lo = hh * HEAD_DIM
            o = (1 - hh) * HEAD_DIM
            a_hi, a_mid, a_lo = _split3(a[:, G_FOX + h:G_FOX + h + 1])
            b_hi, b_mid, b_lo = _split3(b_t[G_FOX + h:G_FOX + h + 1, :])
            own_lane = (lane >= own_lo) & (lane < own_lo + HEAD_DIM)
            own_row = (row >= own_lo) & (row < own_lo + HEAD_DIM)
            ones_l = ((lane >= o + AUG_ONE) & (lane < o + AUG_ONE + 3)).astype(F32)
            ka = jnp.where(own_lane, kp, ones_l)
            ka = jnp.where(lane == o + AUG_NEG_A, -a_hi, ka)
            ka = jnp.where(lane == o + AUG_NEG_A + 1, -a_mid, ka)
            ka = jnp.where(lane == o + AUG_NEG_A + 2, -a_lo, ka)
            ones_r = ((row >= o + AUG_NEG_A) & (row < o + AUG_NEG_A + 3)).astype(F32)
            qa = jnp.where(own_row, q_t, ones_r)
            qa = jnp.where(row == o + AUG_ONE, b_hi, qa)
            qa = jnp.where(row == o + AUG_ONE + 1, b_mid, qa)
            qa = jnp.where(row == o + AUG_ONE + 2, b_lo, qa)
            va = jnp.where(own_row, v_t, (row == o).astype(F32))
            hs = slice(h * LANES, (h + 1) * LANES)
            ka_ref[:, hs] = ka.astype(BF16)
            qta_ref[hs, :] = qa.astype(BF16)
            vta_ref[hs, :] = va.astype(BF16)


def _fox_prep(z, c2):
    t = z.shape[0]
    tp = min(FOX_TQ, t)
    width = FOX_HEADS * LANES
    return pl.pallas_call(
        _fox_prep_kernel,
        grid=(t // tp,),
        in_specs=[
            pl.BlockSpec((tp, FOX_WIDTH), lambda i: (i, Z_FQ // FOX_WIDTH)),
            pl.BlockSpec((tp, FOX_WIDTH), lambda i: (i, Z_FK // FOX_WIDTH)),
            pl.BlockSpec((tp, FOX_WIDTH), lambda i: (i, Z_FV // FOX_WIDTH)),
            pl.BlockSpec((tp, LANES), lambda i: (i, 0)),
        ],
        out_specs=[
            pl.BlockSpec((tp, width), lambda i: (i, 0)),
            pl.BlockSpec((width, tp), lambda i: (0, i)),
            pl.BlockSpec((width, tp), lambda i: (0, i)),
        ],
        out_shape=[jax.ShapeDtypeStruct((t, width), BF16),
                   jax.ShapeDtypeStruct((width, t), BF16),
                   jax.ShapeDtypeStruct((width, t), BF16)],
        compiler_params=_cparams(("parallel",)),
        name="fox_prep",
    )(z, z, z, c2)


def _fox_kernel(cref_ref, qta_ref, ka_ref, vta_ref, o_ref, m_sc, acc_sc, s0_sc, cm0_sc, s1_sc, cm1_sc):
    tq = qta_ref.shape[1]
    tk = tq
    b = pl.program_id(0)
    p = pl.program_id(1)
    i = pl.program_id(2)
    nq = pl.num_programs(2)
    for hh in range(2):
        m_sc[hh] = jnp.full((1, tq), -jnp.inf, F32)
        acc_sc[hh] = jnp.zeros((LANES, tq), F32)

    slots = ((s0_sc, cm0_sc), (s1_sc, cm1_sc))

    def scores(j, diagonal, slot):
        s_sc, cm_sc = slots[slot]
        start = pl.multiple_of(j * tk, tk)
        for hh in range(2):
            hs = slice(hh * LANES, (hh + 1) * LANES)
            s = jnp.dot(ka_ref[pl.ds(start, tk), hs], qta_ref[hs, :],
                        preferred_element_type=F32)
            if diagonal:
                kr = lax.broadcasted_iota(jnp.int32, (tk, tq), 0)
                qc = lax.broadcasted_iota(jnp.int32, (tk, tq), 1)
                s = jnp.where(kr <= qc, s, -jnp.inf)
            s_sc[hh] = s
            cm_sc[hh] = jnp.max(s, axis=0, keepdims=True)

    def softmax_pv(j, slot):
        s_sc, cm_sc = slots[slot]
        start = pl.multiple_of(j * tk, tk)
        for hh in range(2):
            hs = slice(hh * LANES, (hh + 1) * LANES)
            base = (b * FOX_HEADS + 2 * p + hh) * nq
            delta = cref_ref[base + i] - cref_ref[base + j]
            m_old = m_sc[hh]
            m_new = jnp.maximum(m_old, cm_sc[hh] + delta)
            alpha = jnp.exp2(m_old - m_new)
            pm = jnp.exp2(s_sc[hh] - (m_new - delta)).astype(BF16)
            acc_sc[hh] = alpha * acc_sc[hh] + jnp.dot(vta_ref[hs, pl.ds(start, tk)], pm,
                                                      preferred_element_type=F32)
            m_sc[hh] = m_new

    def step(j, diagonal_next):
        for parity in range(2):
            @pl.when(j % 2 == parity)
            def _():
                scores(j + 1, diagonal_next, 1 - parity)
                softmax_pv(j, parity)

    def last(j):
        for parity in range(2):
            @pl.when(j % 2 == parity)
            def _():
                softmax_pv(j, parity)

    @pl.when(i > 0)
    def _():
        scores(0, False, 0)

        def body(j, carry):
            step(j, False)
            return carry

        lax.fori_loop(0, i - 1, body, 0)
        step(i - 1, True)

    @pl.when(i == 0)
    def _():
        scores(i, True, 0)

    last(i)
    acc0 = acc_sc[0]
    acc1 = acc_sc[1]
    row = lax.broadcasted_iota(jnp.int32, (LANES, 1), 0)
    o_t = jnp.where(row < HEAD_DIM, acc0 / acc0[HEAD_DIM:HEAD_DIM + 1, :], acc1 / acc1[0:1, :])
    o_ref[...] = o_t.T.astype(BF16)


def _fox_attn(ka, qta, vta, cref, batch):
    t = ka.shape[0]
    s = t // batch
    tq = min(FOX_TQ, s)
    nq = s // tq
    npairs = FOX_HEADS // 2
    return pl.pallas_call(
        _fox_kernel,
        grid_spec=pltpu.PrefetchScalarGridSpec(
            num_scalar_prefetch=1,
            grid=(batch, npairs, nq),
            in_specs=[
                pl.BlockSpec((2 * LANES, tq), lambda b, p, i, cr: (p, b * nq + i)),
                pl.BlockSpec((s, 2 * LANES), lambda b, p, i, cr: (b, p)),
                pl.BlockSpec((2 * LANES, s), lambda b, p, i, cr: (p, b)),
            ],
            out_specs=pl.BlockSpec((tq, LANES), lambda b, p, i, cr: (b * nq + i, p)),
            scratch_shapes=[
                pltpu.VMEM((2, 1, tq), F32),
                pltpu.VMEM((2, LANES, tq), F32),
                pltpu.VMEM((2, tq, tq), F32),
                pltpu.VMEM((2, 1, tq), F32),
                pltpu.VMEM((2, tq, tq), F32),
                pltpu.VMEM((2, 1, tq), F32),
            ],
        ),
        out_shape=jax.ShapeDtypeStruct((t, FOX_WIDTH), BF16),
        compiler_params=_cparams(("parallel", "parallel", "arbitrary")),
        name="fox_attn",
    )(cref, qta, ka, vta)


def _mem_kv_kernel(mem_ref, nw_ref, w_ref, o_ref):
    h = _rmsnorm(mem_ref[0], nw_ref[...]).astype(BF16)
    o_ref[0] = jnp.dot(h, w_ref[...], preferred_element_type=F32).astype(BF16)


def _mem_kv(mem, norm_w, w_kv):
    b, m, d = mem.shape
    return pl.pallas_call(
        _mem_kv_kernel,
        grid=(b,),
        in_specs=[
            pl.BlockSpec((1, m, d), lambda i: (i, 0, 0)),
            pl.BlockSpec((1, d), lambda i: (0, 0)),
            pl.BlockSpec(w_kv.shape, lambda i: (0, 0)),
        ],
        out_specs=pl.BlockSpec((1, m, w_kv.shape[1]), lambda i: (i, 0, 0)),
        out_shape=jax.ShapeDtypeStruct((b, m, w_kv.shape[1]), BF16),
        compiler_params=_cparams(("parallel",)),
        name="mem_kv",
    )(mem, norm_w, w_kv)


def _post_mix_kernel(with_router, hmc_ref, hf_ref, x_ref, wout_ref, nxw_ref, wq_ref, kv_ref, wo_ref,
                     nfw_ref, *rest):
    if with_router:
        rw_ref, x2_ref, h3_ref, route_ref = rest
    else:
        x2_ref, h3_ref = rest
    half = MLSTM_WIDTH + CONF_CHANNELS
    x1 = (x_ref[...]
          + jnp.dot(hmc_ref[...], wout_ref[0:half, :], preferred_element_type=F32)
          + jnp.dot(hf_ref[...], wout_ref[half:, :], preferred_element_type=F32))
    h2 = _rmsnorm(x1, nxw_ref[...]).astype(BF16)
    q = jnp.dot(h2, wq_ref[...], preferred_element_type=F32) * (XATTN_HEAD_DIM ** -0.5)
    q = q.astype(BF16)
    kv = kv_ref[0]
    outs = []
    for h in range(XATTN_HEADS):
        lo = h * XATTN_HEAD_DIM
        kh = kv[:, lo:lo + XATTN_HEAD_DIM]
        vh = kv[:, XATTN_WIDTH + lo:XATTN_WIDTH + lo + XATTN_HEAD_DIM]
        s = lax.dot_general(q[:, lo:lo + XATTN_HEAD_DIM], kh, (((1,), (1,)), ((), ())),
                            preferred_element_type=F32)
        s = s - jnp.max(s, axis=1, keepdims=True)
        e = jnp.exp(s)
        pm = e / jnp.sum(e, axis=1, keepdims=True)
        outs.append(jnp.dot(pm.astype(BF16), vh, preferred_element_type=F32).astype(BF16))
    o = jnp.concatenate(outs, axis=1)
    x2 = x1 + jnp.dot(o, wo_ref[...], preferred_element_type=F32)
    x2_ref[...] = x2
    h3 = _rmsnorm(x2, nfw_ref[...])
    h3_ref[...] = h3.astype(h3_ref.dtype)
    if with_router:
        lane = lax.broadcasted_iota(jnp.int32, (1, LANES), 1)
        lane_f = lane.astype(F32)
        logits = jnp.dot(h3, rw_ref[...], preferred_element_type=F32, precision=lax.Precision.HIGHEST)
        logits = jnp.where(lane < N_EXPERTS, logits, -jnp.inf)
        m1 = jnp.max(logits, axis=1, keepdims=True)
        i1 = jnp.min(jnp.where(logits == m1, lane_f, float(LANES)), axis=1, keepdims=True)
        rest_l = jnp.where(lane_f == i1, -jnp.inf, logits)
        m2 = jnp.max(rest_l, axis=1, keepdims=True)
        i2 = jnp.min(jnp.where(rest_l == m2, lane_f, float(LANES)), axis=1, keepdims=True)
        e2 = jnp.exp(m2 - m1)
        w1 = 1.0 / (1.0 + e2)
        w2 = e2 * w1
        route = jnp.where(lane == 0, i1, 0.0)
        route = jnp.where(lane == 1, i2, route)
        route = jnp.where(lane == 2, w1, route)
        route = jnp.where(lane == 3, w2, route)
        route_ref[...] = route


def _post_mix(hmc, hf, x, w_out, nx_w, w_q, kv, w_o, nf_w, router_w, batch):
    t, d = x.shape
    tm = min(TM_PROJ, t // batch)
    per_b = (t // batch) // tm
    with_router = router_w is not None
    const = lambda a: pl.BlockSpec(a.shape, lambda i: (0,) * a.ndim)
    in_specs = [
        pl.BlockSpec((tm, hmc.shape[1]), lambda i: (i, 0)),
        pl.BlockSpec((tm, hf.shape[1]), lambda i: (i, 0)),
        pl.BlockSpec((tm, d), lambda i: (i, 0)),
        const(w_out), const(nx_w), const(w_q),
        pl.BlockSpec((1,) + kv.shape[1:], lambda i: (i // per_b, 0, 0)),
        const(w_o), const(nf_w),
    ]
    args = [hmc, hf, x, w_out, nx_w, w_q, kv, w_o, nf_w]
    out_specs = [pl.BlockSpec((tm, d), lambda i: (i, 0)), pl.BlockSpec((tm, d), lambda i: (i, 0))]
    out_shape = [jax.ShapeDtypeStruct((t, d), F32),
                 jax.ShapeDtypeStruct((t, d), F32 if with_router else BF16)]
    if with_router:
        in_specs.append(const(router_w))
        args.append(router_w)
        out_specs.append(pl.BlockSpec((tm, LANES), lambda i: (i, 0)))
        out_shape.append(jax.ShapeDtypeStruct((t, LANES), F32))
    return pl.pallas_call(
        functools.partial(_post_mix_kernel, with_router),
        grid=(t // tm,),
        in_specs=in_specs, out_specs=out_specs, out_shape=out_shape,
        compiler_params=_cparams(("parallel",)),
        name="post_mix_router" if with_router else "post_mix",
    )(*args)


def _ffn_kernel(with_residual, te_ref, nt_ref, x_ref, wg_ref, wu_ref, wd_ref, *rest):
    if with_residual:
        res_ref, y_ref = rest
    else:
        (y_ref,) = rest

    @pl.when(pl.program_id(0) < nt_ref[0])
    def _():
        xb = x_ref[...].astype(BF16)
        acc = res_ref[...] if with_residual else None
        lo = 0
        for fc in FF_CHUNKS:
            a = jnp.dot(xb, wg_ref[0, :, lo:lo + fc], preferred_element_type=F32)
            u = jnp.dot(xb, wu_ref[0, :, lo:lo + fc], preferred_element_type=F32)
            hcur = (a * _sigmoid(a) * u).astype(BF16)
            part = jnp.dot(hcur, wd_ref[0, lo:lo + fc, :], preferred_element_type=F32)
            acc = part if acc is None else acc + part
            lo += fc
        y_ref[...] = acc

    @pl.when(pl.program_id(0) >= nt_ref[0])
    def _():
        y_ref[...] = jnp.zeros_like(y_ref)


def _ffn(x, w_gate, w_up, w_down, tile_expert, n_tiles, residual=None):
    rows, d = x.shape
    tm = min(TM_FFN, rows)
    nt = rows // tm
    ff = w_gate.shape[2]
    assert sum(FF_CHUNKS) == ff
    with_residual = residual is not None

    def row_map(i, te, ntl):
        return (jnp.minimum(i, ntl[0] - 1), 0)

    def w_map(i, te, ntl):
        return (te[jnp.minimum(i, ntl[0] - 1)], 0, 0)

    in_specs = [
        pl.BlockSpec((tm, d), row_map),
        pl.BlockSpec((1, d, ff), w_map),
        pl.BlockSpec((1, d, ff), w_map),
        pl.BlockSpec((1, ff, d), w_map),
    ]
    args = [x, w_gate, w_up, w_down]
    if with_residual:
        in_specs.append(pl.BlockSpec((tm, d), row_map))
        args.append(residual)
    return pl.pallas_call(
        functools.partial(_ffn_kernel, with_residual),
        grid_spec=pltpu.PrefetchScalarGridSpec(
            num_scalar_prefetch=2,
            grid=(nt,),
            in_specs=in_specs,
            out_specs=pl.BlockSpec((tm, d), lambda i, te, ntl: (i, 0)),
        ),
        out_shape=jax.ShapeDtypeStruct((rows, d), F32),
        compiler_params=_cparams(("arbitrary",)),
        name="ffn_dense" if with_residual else "ffn_experts",
    )(tile_expert, n_tiles, *args)


def _plan_kernel(route_ref, rank_ref, cnt_ref, carry):
    tp = route_ref.shape[0]

    @pl.when(pl.program_id(0) == 0)
    def _():
        carry[...] = jnp.zeros_like(carry)

    lane_i = lax.broadcasted_iota(jnp.int32, (1, LANES), 1)
    lane = lane_i.astype(F32)
    route = route_ref[...]
    i1 = route[:, 0:1]
    i2 = route[:, 1:2]
    onehot = (lane == i1).astype(F32) + (lane == i2).astype(F32)
    row = lax.broadcasted_iota(jnp.int32, (tp, tp), 0)
    col = lax.broadcasted_iota(jnp.int32, (tp, tp), 1)
    strict = (row > col).astype(BF16)
    before = jnp.dot(strict, onehot.astype(BF16), preferred_element_type=F32) + carry[...]
    r1 = jnp.sum(jnp.where(lane == i1, before, 0.0), axis=1, keepdims=True)
    r2 = jnp.sum(jnp.where(lane == i2, before, 0.0), axis=1, keepdims=True)
    rank_ref[...] = jnp.where(lane_i == 0, r1, jnp.where(lane_i == 1, r2, 0.0))
    total = carry[...] + jnp.sum(onehot, axis=0, keepdims=True)
    carry[...] = total
    cnt_ref[...] = total


def _plan(route):
    t = route.shape[0]
    tp = min(512, t)
    return pl.pallas_call(
        _plan_kernel,
        grid=(t // tp,),
        in_specs=[pl.BlockSpec((tp, LANES), lambda i: (i, 0))],
        out_specs=[pl.BlockSpec((tp, LANES), lambda i: (i, 0)),
                   pl.BlockSpec((1, LANES), lambda i: (0, 0))],
        out_shape=[jax.ShapeDtypeStruct((t, LANES), F32), jax.ShapeDtypeStruct((1, LANES), F32)],
        scratch_shapes=[pltpu.VMEM((1, LANES), F32)],
        compiler_params=_cparams(("arbitrary",)),
        name="route_plan",
    )(route)


def _row_copy(src, src_row, dst, dst_row, sem):
    return pltpu.make_async_copy(src.at[pl.ds(src_row, 1), :], dst.at[pl.ds(dst_row, 1), :], sem)


def _dispatch_kernel(pos_ref, h_ref, xs_in_ref, xs_ref, sem):
    del xs_in_ref
    td = h_ref.shape[0]

    def start(r, carry):
        _row_copy(h_ref, r, xs_ref, pos_ref[0, 0, 2 * r], sem).start()
        _row_copy(h_ref, r, xs_ref, pos_ref[0, 0, 2 * r + 1], sem).start()
        return carry

    def wait(r, carry):
        _row_copy(h_ref, r, xs_ref, pos_ref[0, 0, 2 * r], sem).wait()
        _row_copy(h_ref, r, xs_ref, pos_ref[0, 0, 2 * r + 1], sem).wait()
        return carry

    lax.fori_loop(0, td, start, 0)
    lax.fori_loop(0, td, wait, 0)


def _dispatch(h3, pos, xs_zero):
    t, d = h3.shape
    td = min(TD_ROUTE, t)
    nt = t // td
    pos3 = pos.reshape(nt, 1, 2 * td)
    return pl.pallas_call(
        _dispatch_kernel,
        grid=(nt,),
        in_specs=[
            pl.BlockSpec((1, 1, 2 * td), lambda i: (i, 0, 0), memory_space=pltpu.SMEM),
            pl.BlockSpec((td, d), lambda i: (i, 0)),
            pl.BlockSpec(memory_space=pl.ANY),
        ],
        out_specs=pl.BlockSpec(memory_space=pl.ANY),
        out_shape=jax.ShapeDtypeStruct(xs_zero.shape, xs_zero.dtype),
        scratch_shapes=[pltpu.SemaphoreType.DMA],
        input_output_aliases={2: 0},
        compiler_params=_cparams(("arbitrary",)),
        name="dispatch",
    )(pos3, h3, xs_zero)


def _combine_kernel(pos_ref, x_ref, route_ref, nw_ref, ys_ref, o_ref, buf, sem):
    td = x_ref.shape[0]

    def start(r, carry):
        _row_copy(ys_ref, pos_ref[0, 0, 2 * r], buf.at[0], r, sem).start()
        _row_copy(ys_ref, pos_ref[0, 0, 2 * r + 1], buf.at[1], r, sem).start()
        return carry

    def wait(r, carry):
        _row_copy(ys_ref, pos_ref[0, 0, 2 * r], buf.at[0], r, sem).wait()
        _row_copy(ys_ref, pos_ref[0, 0, 2 * r + 1], buf.at[1], r, sem).wait()
        return carry

    lax.fori_loop(0, td, start, 0)
    lax.fori_loop(0, td, wait, 0)
    route = route_ref[...]
    x = x_ref[...] + route[:, 2:3] * buf[0] + route[:, 3:4] * buf[1]
    o_ref[...] = _rmsnorm(x, nw_ref[...])


def _combine(x2, route, pos, ys, norm_w):
    t, d = x2.shape
    td = min(TD_ROUTE, t)
    nt = t // td
    pos3 = pos.reshape(nt, 1, 2 * td)
    return pl.pallas_call(
        _combine_kernel,
        grid=(nt,),
        in_specs=[
            pl.BlockSpec((1, 1, 2 * td), lambda i: (i, 0, 0), memory_space=pltpu.SMEM),
            pl.BlockSpec((td, d), lambda i: (i, 0)),
            pl.BlockSpec((td, LANES), lambda i: (i, 0)),
            pl.BlockSpec((1, d), lambda i: (0, 0)),
            pl.BlockSpec(memory_space=pl.ANY),
        ],
        out_specs=pl.BlockSpec((td, d), lambda i: (i, 0)),
        out_shape=jax.ShapeDtypeStruct((t, d), F32),
        scratch_shapes=[pltpu.VMEM((2, td, d), F32), pltpu.SemaphoreType.DMA],
        compiler_params=_cparams(("arbitrary",)),
        name="combine",
    )(pos3, x2, route, norm_w, ys)


def _split_w_in(w_in):
    mw, cw, fw = MLSTM_WIDTH, CONF_CHANNELS, FOX_WIDTH
    sizes = (2 * mw, mw, mw, MLSTM_HEADS, MLSTM_HEADS, 2 * cw, fw, fw, fw, FOX_HEADS)
    parts, off = [], 0
    for sz in sizes:
        parts.append(w_in[:, off:off + sz])
        off += sz
    m_qk, m_v, m_o, m_i, m_f, c_glu, f_q, f_k, f_v, f_f = parts
    w_main = jnp.concatenate([m_qk, m_v, m_o, c_glu, f_q, f_k, f_v], axis=1).astype(BF16)
    w_gate = jnp.concatenate([m_i, m_f, f_f], axis=1)
    w_gate = jnp.pad(w_gate, ((0, 0), (0, LANES - w_gate.shape[1]))).astype(BF16)
    return w_main, w_gate


def _gate_bias(b_i, b_f, fox_b):
    gb = jnp.concatenate([b_i, b_f, fox_b]).astype(F32)
    return jnp.pad(gb, (0, LANES - gb.shape[0])).reshape(1, LANES)


def _route_tables(route, rank, counts, tm, n_tiles_max):
    cnt = counts[0, :N_EXPERTS].astype(jnp.int32)
    tiles = (cnt + tm - 1) // tm
    tile_end = jnp.cumsum(tiles)
    row_off = (tile_end - tiles) * tm
    idx = route[:, 0:2].astype(jnp.int32)
    pos = row_off[idx] + rank[:, 0:2].astype(jnp.int32)
    tile_ids = jnp.arange(n_tiles_max, dtype=jnp.int32)
    tile_expert = jnp.sum(tile_ids[:, None] >= tile_end[None, :], axis=1).astype(jnp.int32)
    tile_expert = jnp.minimum(tile_expert, N_EXPERTS - 1)
    n_tiles = tile_end[-1:].astype(jnp.int32)
    return pos.reshape(-1), tile_expert, n_tiles


def kernel(x, mem, norm_mix_w, w_in, mlstm_conv_w, mlstm_conv_b, mlstm_b_i, mlstm_b_f, mlstm_norm_w,
           conf_conv_w, conf_conv_b, conf_ln_w, conf_ln_b, fox_b_f, w_out, norm_xattn_w, norm_mem_w,
           xattn_w_q, xattn_w_kv, xattn_w_o, norm_ffn_w, ffn_w_gate, ffn_w_up, ffn_w_down, router_w,
           moe_w_gate, moe_w_up, moe_w_down, norm_final_w):
    batch, seq, d = x.shape
    depth = w_in.shape[0]
    t = batch * seq
    xf = x.reshape(t, d)
    row = lambda a: a.reshape(1, -1).astype(F32)
    out = None
    for l in range(depth):
        w_main, w_gate = _split_w_in(w_in[l])
        z, gates = _in_proj(xf, row(norm_mix_w[l]), w_main, w_gate)
        hmc, c2 = _seq_mix(
            z, gates, _gate_bias(mlstm_b_i[l], mlstm_b_f[l], fox_b_f[l]),
            mlstm_conv_w[l].astype(F32), row(mlstm_conv_b[l]), row(mlstm_norm_w[l]),
            conf_conv_w[l].astype(F32), row(conf_conv_b[l]), row(conf_ln_w[l]), row(conf_ln_b[l]),
            batch)
        ka, qta, vta = _fox_prep(z, c2)
        tq = min(FOX_TQ, seq)
        cref = c2.reshape(batch, seq // tq, tq, LANES)[:, :, 0, G_FOX:G_FOX + FOX_HEADS]
        cref = cref.transpose(0, 2, 1).reshape(-1)
        hf = _fox_attn(ka, qta, vta, cref, batch)
        kv = _mem_kv(mem, row(norm_mem_w[l]), xattn_w_kv[l].astype(BF16))
        dense = l % 2 == 0
        j = l // 2
        rw = None
        if not dense:
            rw = jnp.pad(router_w[j].astype(F32), ((0, 0), (0, LANES - N_EXPERTS)))
        res = _post_mix(hmc, hf, xf, w_out[l].astype(BF16), row(norm_xattn_w[l]),
                        xattn_w_q[l].astype(BF16), kv, xattn_w_o[l].astype(BF16),
                        row(norm_ffn_w[l]), rw, batch)
        if dense:
            x2, h3 = res
            nt = t // min(TM_FFN, t)
            xf = _ffn(h3, ffn_w_gate[j][None].astype(BF16), ffn_w_up[j][None].astype(BF16),
                      ffn_w_down[j][None].astype(BF16), jnp.zeros((nt,), jnp.int32),
                      jnp.full((1,), nt, jnp.int32), residual=x2)
            if l == depth - 1:
                raise NotImplementedError("final norm after a dense layer")
        else:
            x2, h3, route = res
            rank, counts = _plan(route)
            tm = min(TM_FFN, t)
            n_tiles_max = (2 * t) // tm + N_EXPERTS
            pos, tile_expert, n_tiles = _route_tables(route, rank, counts, tm, n_tiles_max)
            xs = _dispatch(h3, pos, jnp.zeros((n_tiles_max * tm, d), F32))
            ys = _ffn(xs, moe_w_gate[j].astype(BF16), moe_w_up[j].astype(BF16),
                      moe_w_down[j].astype(BF16), tile_expert, n_tiles)
            assert l == depth - 1
            out = _combine(x2, route, pos, ys, row(norm_final_w))
    return out.reshape(batch, seq, d)
```

```python
import functools

import jax
import jax.numpy as jnp
from jax import lax
from jax.experimental import pallas as pl
from jax.experimental.pallas import tpu as pltpu

F32 = jnp.float32
BF16 = jnp.bfloat16
EPS = 1e-6

MLSTM_HEADS = 4
HEAD_DIM = 64
MLSTM_WIDTH = MLSTM_HEADS * HEAD_DIM
MLSTM_CONV = 4
CONF_CHANNELS = 256
CONF_KERNEL = 31
FOX_HEADS = 8
FOX_WIDTH = FOX_HEADS * HEAD_DIM
XATTN_HEADS = 4
XATTN_HEAD_DIM = 128
XATTN_WIDTH = XATTN_HEADS * XATTN_HEAD_DIM
N_EXPERTS = 8
LANES = 128
SUBLANES = 8

Z_QK = 0
Z_V = 512
Z_O = 768
Z_GLU = 1024
Z_SEQ_WIDTH = 1536
Z_FQ = 1536
Z_FK = 2048
Z_FV = 2560
Z_WIDTH = 3072
G_I = 0
G_F = 4
G_FOX = 8

TM_PROJ = 512
SEQ_CHUNK = 256
CONV_TAIL = 8
CONF_TAIL = 32
FOX_TQ = 512
FOX_TK = 512
TM_FFN = 256
FF_CHUNKS = (1024, 1024, 768)
TD_ROUTE = 256
DMA_UNROLL = 8
VMEM_LIMIT = 56 * 1024 * 1024


def _cparams(sem, vmem=VMEM_LIMIT):
    return pltpu.CompilerParams(dimension_semantics=sem, vmem_limit_bytes=vmem)


def _sigmoid(x):
    return 1.0 / (1.0 + jnp.exp(-x))


def _log_sigmoid(x):
    return jnp.minimum(x, 0.0) - jnp.log(1.0 + jnp.exp(-jnp.abs(x)))


def _rmsnorm(x, w):
    ms = jnp.mean(x * x, axis=-1, keepdims=True)
    return x * lax.rsqrt(ms + EPS) * w


def _in_proj_kernel(x_ref, nw_ref, w_ref, wg_ref, z_ref, g_ref):
    h = _rmsnorm(x_ref[...], nw_ref[...]).astype(BF16)
    for n in range(0, Z_WIDTH, 512):
        z_ref[:, n:n + 512] = jnp.dot(h, w_ref[:, n:n + 512],
                                      preferred_element_type=F32).astype(BF16)
    g_ref[...] = jnp.dot(h, wg_ref[...], preferred_element_type=F32)


def _in_proj(x, norm_w, w_main, w_gate):
    t, d = x.shape
    tm = min(TM_PROJ, t)
    return pl.pallas_call(
        _in_proj_kernel,
        grid=(t // tm,),
        in_specs=[
            pl.BlockSpec((tm, d), lambda i: (i, 0)),
            pl.BlockSpec((1, d), lambda i: (0, 0)),
            pl.BlockSpec((d, Z_WIDTH), lambda i: (0, 0)),
            pl.BlockSpec((d, LANES), lambda i: (0, 0)),
        ],
        out_specs=[
            pl.BlockSpec((tm, Z_WIDTH), lambda i: (i, 0)),
            pl.BlockSpec((tm, LANES), lambda i: (i, 0)),
        ],
        out_shape=[jax.ShapeDtypeStruct((t, Z_WIDTH), BF16),
                   jax.ShapeDtypeStruct((t, LANES), F32)],
        compiler_params=_cparams(("parallel",)),
        name="in_proj",
    )(x, norm_w, w_main, w_gate)


def _seq_kernel(z_ref, g_ref, gb_ref, cw_ref, cb_ref, mnw_ref, ccw_ref, ccb_ref, lnw_ref, lnb_ref,
                out_ref, c_ref,
                qk_buf, u_buf, ush_buf, cstate, nstate, mstate, carry):
    L = z_ref.shape[0]
    W = MLSTM_WIDTH

    @pl.when(pl.program_id(1) == 0)
    def _():
        qk_buf[0:CONV_TAIL, :] = jnp.zeros((CONV_TAIL, 2 * W), F32)
        u_buf[0:CONF_TAIL, :] = jnp.zeros((CONF_TAIL, CONF_CHANNELS), F32)
        cstate[...] = jnp.zeros_like(cstate)
        nstate[...] = jnp.zeros_like(nstate)
        mstate[...] = jnp.zeros_like(mstate)
        carry[...] = jnp.zeros_like(carry)

    lane_g = lax.broadcasted_iota(jnp.int32, (1, LANES), 1)
    lane_w = lax.broadcasted_iota(jnp.int32, (1, W), 1)

    g = g_ref[...] + gb_ref[...]
    is_forget = (lane_g >= G_F) & (lane_g < G_FOX + FOX_HEADS)
    lsg = jnp.where(is_forget, _log_sigmoid(g), 0.0)
    row = lax.broadcasted_iota(jnp.int32, (L, L), 0)
    col = lax.broadcasted_iota(jnp.int32, (L, L), 1)
    causal = row >= col
    tri = causal.astype(F32)
    cs = jnp.dot(tri, lsg, preferred_element_type=F32, precision=lax.Precision.HIGHEST)
    c_all = cs + carry[...]
    c_ref[...] = c_all * 1.4426950408889634
    carry[...] = c_all[L - 1:L, :]

    qk_buf[CONV_TAIL:CONV_TAIL + L, :] = z_ref[:, Z_QK:Z_QK + 2 * W].astype(F32)
    qk = jnp.zeros((L, 2 * W), F32) + cb_ref[...]
    for j in range(MLSTM_CONV):
        off = CONV_TAIL - (MLSTM_CONV - 1) + j
        qk = qk + qk_buf[off:off + L, :] * cw_ref[j:j + 1, :]
    qk_buf[0:CONV_TAIL, :] = qk_buf[L:L + CONV_TAIL, :]
    qk = qk * _sigmoid(qk)
    q = qk[:, 0:W]
    k = qk[:, W:2 * W] * (HEAD_DIM ** -0.5)
    v = z_ref[:, Z_V:Z_V + W]
    q_b = q.astype(BF16)
    k_b = k.astype(BF16)

    cs_t = cs.T
    g_t = g.T
    m_prev = mstate[...]
    n_prev = nstate[...]
    c_prev = cstate[...]
    qn = q * n_prev
    q_c = jnp.dot(q_b, c_prev.astype(BF16), preferred_element_type=F32)

    num = jnp.zeros((L, W), F32)
    w_inter_l = jnp.zeros((L, W), F32)
    denom_l = jnp.ones((L, W), F32)
    wg_l = jnp.zeros((L, W), F32)
    decay_l = jnp.zeros((1, W), F32)
    m_new_row = m_prev
    for h in range(MLSTM_HEADS):
        hmask = (lane_w // HEAD_DIM) == h
        b_col = cs[:, G_F + h:G_F + h + 1]
        b_row = cs_t[G_F + h:G_F + h + 1, :]
        li_row = g_t[G_I + h:G_I + h + 1, :]
        li_col = g[:, G_I + h:G_I + h + 1]
        m_h = m_prev[:, h:h + 1]
        d_log = jnp.where(causal, b_col - b_row + li_row, -jnp.inf)
        inter = b_col + m_h
        m_t = jnp.maximum(jnp.max(d_log, axis=1, keepdims=True), inter)
        dmat = jnp.exp(d_log - m_t)
        q_h = jnp.where(hmask, q_b, jnp.zeros_like(q_b))
        s = lax.dot_general(q_h, k_b, (((1,), (1,)), ((), ())), preferred_element_type=F32) * dmat
        w_inter = jnp.exp(inter - m_t)
        pv = jnp.dot(s.astype(BF16), v, preferred_element_type=F32)
        num = jnp.where(hmask, pv, num)
        qn_h = jnp.sum(jnp.where(hmask, qn, 0.0), axis=1, keepdims=True)
        den = jnp.sum(s, axis=1, keepdims=True) + w_inter * qn_h
        dn = jnp.maximum(jnp.abs(den), jnp.exp(-m_t))
        w_inter_l = jnp.where(hmask, w_inter, w_inter_l)
        denom_l = jnp.where(hmask, dn, denom_l)
        b_tot = cs[L - 1:L, G_F + h:G_F + h + 1]
        g_h = b_tot - b_col + li_col
        m_new = jnp.maximum(b_tot + m_h, jnp.max(g_h, axis=0, keepdims=True))
        wg_l = jnp.where(hmask, jnp.exp(g_h - m_new), wg_l)
        decay_l = jnp.where(hmask, jnp.exp(b_tot + m_h - m_new), decay_l)
        m_new_row = jnp.where(lane_g == h, m_new, m_new_row)

    hout = (num + w_inter_l * q_c) / denom_l
    hsq = hout * hout
    rs_l = jnp.zeros((L, W), F32)
    for h in range(MLSTM_HEADS):
        hmask = (lane_w // HEAD_DIM) == h
        ms = jnp.sum(jnp.where(hmask, hsq, 0.0), axis=1, keepdims=True) * (1.0 / HEAD_DIM)
        rs_l = jnp.where(hmask, lax.rsqrt(ms + EPS), rs_l)
    o_gate = _sigmoid(z_ref[:, Z_O:Z_O + W].astype(F32))
    out_ref[:, 0:W] = (hout * rs_l * mnw_ref[...] * o_gate).astype(BF16)

    kw = k * wg_l
    upd = lax.dot_general(kw.astype(BF16), v, (((0,), (0,)), ((), ())), preferred_element_type=F32)
    rk = lax.broadcasted_iota(jnp.int32, (W, W), 0) // HEAD_DIM
    rv = lax.broadcasted_iota(jnp.int32, (W, W), 1) // HEAD_DIM
    cstate[...] = decay_l * c_prev + jnp.where(rk == rv, upd, 0.0)
    nstate[...] = decay_l * n_prev + jnp.sum(kw, axis=0, keepdims=True)
    mstate[...] = m_new_row

    a = z_ref[:, Z_GLU:Z_GLU + CONF_CHANNELS].astype(F32)
    gg = z_ref[:, Z_GLU + CONF_CHANNELS:Z_GLU + 2 * CONF_CHANNELS].astype(F32)
    u_buf[CONF_TAIL:CONF_TAIL + L, :] = a * _sigmoid(gg)
    span = L + CONF_TAIL - SUBLANES
    for r in range(1, SUBLANES):
        ush_buf[r - 1] = u_buf[r:r + span, :]
    hc = jnp.zeros((L, CONF_CHANNELS), F32) + ccb_ref[...]
    for j in range(CONF_KERNEL):
        off = CONF_TAIL - (CONF_KERNEL - 1) + j
        base, r = off - off % SUBLANES, off % SUBLANES
        src = u_buf[base:base + L, :] if r == 0 else ush_buf[r - 1, base:base + L, :]
        hc = hc + src * ccw_ref[j:j + 1, :]
    u_buf[0:CONF_TAIL, :] = u_buf[L:L + CONF_TAIL, :]
    mu = jnp.mean(hc, axis=1, keepdims=True)
    xc = hc - mu
    var = jnp.mean(xc * xc, axis=1, keepdims=True)
    y = xc * lax.rsqrt(var + EPS) * lnw_ref[...] + lnb_ref[...]
    out_ref[:, W:W + CONF_CHANNELS] = (y * _sigmoid(y)).astype(BF16)


def _seq_mix(z, gates, gate_bias, conv_w, conv_b, mnorm_w, cconv_w, cconv_b, ln_w, ln_b, batch):
    t = z.shape[0]
    s = t // batch
    L = min(SEQ_CHUNK, s)
    nc = s // L
    W = MLSTM_WIDTH
    full = lambda a: pl.BlockSpec(a.shape, lambda b, c: (0,) * a.ndim)
    return pl.pallas_call(
        _seq_kernel,
        grid=(batch, nc),
        in_specs=[
            pl.BlockSpec((L, Z_SEQ_WIDTH), lambda b, c: (b * nc + c, 0)),
            pl.BlockSpec((L, LANES), lambda b, c: (b * nc + c, 0)),
            full(gate_bias), full(conv_w), full(conv_b), full(mnorm_w),
            full(cconv_w), full(cconv_b), full(ln_w), full(ln_b),
        ],
        out_specs=[
            pl.BlockSpec((L, W + CONF_CHANNELS), lambda b, c: (b * nc + c, 0)),
            pl.BlockSpec((L, LANES), lambda b, c: (b * nc + c, 0)),
        ],
        out_shape=[jax.ShapeDtypeStruct((t, W + CONF_CHANNELS), BF16),
                   jax.ShapeDtypeStruct((t, LANES), F32)],
        scratch_shapes=[
            pltpu.VMEM((CONV_TAIL + L + CONV_TAIL, 2 * W), F32),
            pltpu.VMEM((CONF_TAIL + L + CONF_TAIL, CONF_CHANNELS), F32),
            pltpu.VMEM((SUBLANES - 1, L + CONF_TAIL - SUBLANES, CONF_CHANNELS), F32),
            pltpu.VMEM((W, W), F32),
            pltpu.VMEM((1, W), F32),
            pltpu.VMEM((1, LANES), F32),
            pltpu.VMEM((1, LANES), F32),
        ],
        compiler_params=_cparams(("arbitrary", "arbitrary")),
        name="seq_mix",
    )(z, gates, gate_bias, conv_w, conv_b, mnorm_w, cconv_w, cconv_b, ln_w, ln_b)


AUG_NEG_A = 0
AUG_ONE = 3
ST_ROWS = 8
ST_NORMS = 0
ST_FIRST = 1
ST_LAST = 2
FOX_SKIP_LOG2 = 160.0
NORM_SLACK = 1.01


def _split3(x):
    hi = x.astype(BF16).astype(F32)
    mid = (x - hi).astype(BF16).astype(F32)
    lo = (x - hi - mid).astype(BF16).astype(F32)
    return hi, mid, lo


def _fox_prep_kernel(zq_ref, zk_ref, zv_ref, c2_ref, ka_ref, qta_ref, vta_ref, st_ref):
    tp = zq_ref.shape[0]
    c2 = c2_ref[...]
    a = c2 - c2[0:1, :]
    c2_t = c2.T
    b_t = c2_t - c2_t[:, 0:1]
    lane = lax.broadcasted_iota(jnp.int32, (1, LANES), 1)
    row = lax.broadcasted_iota(jnp.int32, (LANES, 1), 0)
    q_scale = (HEAD_DIM ** -0.5) * 1.4426950408889634
    norms = jnp.zeros((1, LANES), F32)
    for p in range(FOX_HEADS // 2):
        sl = slice(p * LANES, (p + 1) * LANES)
        kp = zk_ref[:, sl].astype(F32)
        q_r = (zq_ref[:, sl].astype(F32) * q_scale).astype(BF16).astype(F32)
        q_t = q_r.T
        v_t = zv_ref[:, sl].astype(F32).T
        sq = jnp.concatenate([q_r * q_r, kp * kp], axis=1).astype(BF16)
        grp = lax.broadcasted_iota(jnp.int32, (2 * LANES, LANES), 0) // HEAD_DIM
        dst = jnp.where(grp < 2, 2 * p + grp, FOX_HEADS + 2 * p + grp - 2)
        sel = (lax.broadcasted_iota(jnp.int32, (2 * LANES, LANES), 1) == dst).astype(BF16)
        n2 = jnp.max(jnp.dot(sq, sel, preferred_element_type=F32), axis=0, keepdims=True)
        norms = jnp.maximum(norms, jnp.sqrt(n2) * NORM_SLACK)
        for hh in range(2):
            h = 2 * p + hh
            own_lo = hh * HEAD_DIM
            o = (1 - hh) * HEAD_DIM
            a_hi, a_mid, a_lo = _split3(a[:, G_FOX + h:G_FOX + h + 1])
            b_hi, b_mid, b_lo = _split3(b_t[G_FOX + h:G_FOX + h + 1, :])
            own_lane = (lane >= own_lo) & (lane < own_lo + HEAD_DIM)
            own_row = (row >= own_lo) & (row < own_lo + HEAD_DIM)
            ones_l = ((lane >= o + AUG_ONE) & (lane < o + AUG_ONE + 3)).astype(F32)
            ka = jnp.where(own_lane, kp, ones_l)
            ka = jnp.where(lane == o + AUG_NEG_A, -a_hi, ka)
            ka = jnp.where(lane == o + AUG_NEG_A + 1, -a_mid, ka)
            ka = jnp.where(lane == o + AUG_NEG_A + 2, -a_lo, ka)
            ones_r = ((row >= o + AUG_NEG_A) & (row < o + AUG_NEG_A + 3)).astype(F32)
            qa = jnp.where(own_row, q_t, ones_r)
            qa = jnp.where(row == o + AUG_ONE, b_hi, qa)
            qa = jnp.where(row == o + AUG_ONE + 1, b_mid, qa)
            qa = jnp.where(row == o + AUG_ONE + 2, b_lo, qa)
            va = jnp.where(own_row, v_t, (row == o).astype(F32))
            hs = slice(h * LANES, (h + 1) * LANES)
            ka_ref[:, hs] = ka.astype(BF16)
            qta_ref[hs, :] = qa.astype(BF16)
            vta_ref[hs, :] = va.astype(BF16)
    st_ref[ST_NORMS:ST_NORMS + 1, :] = norms
    st_ref[ST_FIRST:ST_FIRST + 1, :] = c2[0:1, :]
    st_ref[ST_LAST:ST_LAST + 1, :] = c2[tp - 1:tp, :]
    st_ref[ST_LAST + 1:, :] = jnp.zeros((ST_ROWS - ST_LAST - 1, LANES), F32)


def _fox_prep(z, c2):
    t = z.shape[0]
    tp = min(FOX_TQ, t)
    width = FOX_HEADS * LANES
    return pl.pallas_call(
        _fox_prep_kernel,
        grid=(t // tp,),
        in_specs=[
            pl.BlockSpec((tp, FOX_WIDTH), lambda i: (i, Z_FQ // FOX_WIDTH)),
            pl.BlockSpec((tp, FOX_WIDTH), lambda i: (i, Z_FK // FOX_WIDTH)),
            pl.BlockSpec((tp, FOX_WIDTH), lambda i: (i, Z_FV // FOX_WIDTH)),
            pl.BlockSpec((tp, LANES), lambda i: (i, 0)),
        ],
        out_specs=[
            pl.BlockSpec((tp, width), lambda i: (i, 0)),
            pl.BlockSpec((width, tp), lambda i: (0, i)),
            pl.BlockSpec((width, tp), lambda i: (0, i)),
            pl.BlockSpec((ST_ROWS, LANES), lambda i: (i, 0)),
        ],
        out_shape=[jax.ShapeDtypeStruct((t, width), BF16),
                   jax.ShapeDtypeStruct((width, t), BF16),
                   jax.ShapeDtypeStruct((width, t), BF16),
                   jax.ShapeDtypeStruct((t // tp * ST_ROWS, LANES), F32)],
        compiler_params=_cparams(("parallel",)),
        name="fox_prep",
    )(z, z, z, c2)


def _fox_tile_plan(stats, batch):
    st = stats.reshape(batch, -1, ST_ROWS, LANES)
    nq = st.shape[1]
    qmax = st[:, :, ST_NORMS, 0:FOX_HEADS]
    kmax = st[:, :, ST_NORMS, FOX_HEADS:2 * FOX_HEADS]
    first = st[:, :, ST_FIRST, G_FOX:G_FOX + FOX_HEADS]
    last = st[:, :, ST_LAST, G_FOX:G_FOX + FOX_HEADS]
    ub = (qmax[:, :, None, :] * (kmax[:, None, :, :] + kmax[:, :, None, :])
          + first[:, :, None, :] - last[:, None, :, :])
    ti = jnp.arange(nq)[:, None]
    tj = jnp.arange(nq)[None, :]
    need = (tj == ti) | ((tj < ti) & ~(ub < -FOX_SKIP_LOG2).transpose(0, 3, 1, 2))
    jmin = jnp.min(jnp.where(need, tj, nq), axis=-1)
    count = jnp.arange(nq)[None, None, :] - jmin + 1
    count = jnp.max(count.reshape(batch, FOX_HEADS // 2, 2, nq), axis=2)
    cref = first.transpose(0, 2, 1).reshape(-1)
    return count.reshape(-1).astype(jnp.int32), cref


def _fox_kernel(cnt_ref, cref_ref, qta_ref, ka_ref, vta_ref, o_ref,
                m_sc, acc_sc, s0_sc, cm0_sc, s1_sc, cm1_sc):
    tq = qta_ref.shape[1]
    tk = tq
    b = pl.program_id(0)
    p = pl.program_id(1)
    i = pl.program_id(2)
    nq = pl.num_programs(2)
    for hh in range(2):
        m_sc[hh] = jnp.full((1, tq), -jnp.inf, F32)
        acc_sc[hh] = jnp.zeros((LANES, tq), F32)

    slots = ((s0_sc, cm0_sc), (s1_sc, cm1_sc))

    def scores(j, diagonal, slot):
        s_sc, cm_sc = slots[slot]
        start = pl.multiple_of(j * tk, tk)
        for hh in range(2):
            hs = slice(hh * LANES, (hh + 1) * LANES)
            s = jnp.dot(ka_ref[pl.ds(start, tk), hs], qta_ref[hs, :],
                        preferred_element_type=F32)
            if diagonal:
                kr = lax.broadcasted_iota(jnp.int32, (tk, tq), 0)
                qc = lax.broadcasted_iota(jnp.int32, (tk, tq), 1)
                s = jnp.where(kr <= qc, s, -jnp.inf)
            s_sc[hh] = s
            cm_sc[hh] = jnp.max(s, axis=0, keepdims=True)

    def softmax_pv(j, slot):
        s_sc, cm_sc = slots[slot]
        start = pl.multiple_of(j * tk, tk)
        for hh in range(2):
            hs = slice(hh * LANES, (hh + 1) * LANES)
            base = (b * FOX_HEADS + 2 * p + hh) * nq
            delta = cref_ref[base + i] - cref_ref[base + j]
            m_old = m_sc[hh]
            m_new = jnp.maximum(m_old, cm_sc[hh] + delta)
            alpha = jnp.exp2(m_old - m_new)
            pm = jnp.exp2(s_sc[hh] - (m_new - delta)).astype(BF16)
            acc_sc[hh] = alpha * acc_sc[hh] + jnp.dot(vta_ref[hs, pl.ds(start, tk)], pm,
                                                      preferred_element_type=F32)
            m_sc[hh] = m_new

    n_visits = cnt_ref[(b * pl.num_programs(1) + p) * nq + i]
    scores(i, True, 0)

    def body(t, carry):
        for parity in range(2):
            @pl.when(t % 2 == parity)
            def _():
                scores(i - t - 1, False, 1 - parity)
                softmax_pv(i - t, parity)
        return carry

    lax.fori_loop(0, n_visits - 1, body, 0)
    for parity in range(2):
        @pl.when((n_visits - 1) % 2 == parity)
        def _():
            softmax_pv(i - n_visits + 1, parity)
    acc0 = acc_sc[0]
    acc1 = acc_sc[1]
    row = lax.broadcasted_iota(jnp.int32, (LANES, 1), 0)
    o_t = jnp.where(row < HEAD_DIM, acc0 / acc0[HEAD_DIM:HEAD_DIM + 1, :], acc1 / acc1[0:1, :])
    o_ref[...] = o_t.T.astype(BF16)


def _fox_attn(ka, qta, vta, visits, cref, batch):
    t = ka.shape[0]
    s = t // batch
    tq = min(FOX_TQ, s)
    nq = s // tq
    npairs = FOX_HEADS // 2
    return pl.pallas_call(
        _fox_kernel,
        grid_spec=pltpu.PrefetchScalarGridSpec(
            num_scalar_prefetch=2,
            grid=(batch, npairs, nq),
            in_specs=[
                pl.BlockSpec((2 * LANES, tq), lambda b, p, i, nv, cr: (p, b * nq + i)),
                pl.BlockSpec((s, 2 * LANES), lambda b, p, i, nv, cr: (b, p)),
                pl.BlockSpec((2 * LANES, s), lambda b, p, i, nv, cr: (p, b)),
            ],
            out_specs=pl.BlockSpec((tq, LANES), lambda b, p, i, nv, cr: (b * nq + i, p)),
            scratch_shapes=[
                pltpu.VMEM((2, 1, tq), F32),
                pltpu.VMEM((2, LANES, tq), F32),
                pltpu.VMEM((2, tq, tq), F32),
                pltpu.VMEM((2, 1, tq), F32),
                pltpu.VMEM((2, tq, tq), F32),
                pltpu.VMEM((2, 1, tq), F32),
            ],
        ),
        out_shape=jax.ShapeDtypeStruct((t, FOX_WIDTH), BF16),
        compiler_params=_cparams(("parallel", "parallel", "arbitrary")),
        name="fox_attn",
    )(visits, cref, qta, ka, vta)


def _mem_kv_kernel(mem_ref, nw_ref, w_ref, o_ref):
    h = _rmsnorm(mem_ref[0], nw_ref[...]).astype(BF16)
    o_ref[0] = jnp.dot(h, w_ref[...], preferred_element_type=F32).astype(BF16)


def _mem_kv(mem, norm_w, w_kv):
    b, m, d = mem.shape
    return pl.pallas_call(
        _mem_kv_kernel,
        grid=(b,),
        in_specs=[
            pl.BlockSpec((1, m, d), lambda i: (i, 0, 0)),
            pl.BlockSpec((1, d), lambda i: (0, 0)),
            pl.BlockSpec(w_kv.shape, lambda i: (0, 0)),
        ],
        out_specs=pl.BlockSpec((1, m, w_kv.shape[1]), lambda i: (i, 0, 0)),
        out_shape=jax.ShapeDtypeStruct((b, m, w_kv.shape[1]), BF16),
        compiler_params=_cparams(("parallel",)),
        name="mem_kv",
    )(mem, norm_w, w_kv)


def _post_mix_kernel(with_router, hmc_ref, hf_ref, x_ref, wout_ref, nxw_ref, wq_ref, kv_ref, wo_ref,
                     nfw_ref, *rest):
    if with_router:
        rw_ref, x2_ref, h3_ref, route_ref = rest
    else:
        x2_ref, h3_ref = rest
    half = MLSTM_WIDTH + CONF_CHANNELS
    x1 = (x_ref[...]
          + jnp.dot(hmc_ref[...], wout_ref[0:half, :], preferred_element_type=F32)
          + jnp.dot(hf_ref[...], wout_ref[half:, :], preferred_element_type=F32))
    h2 = _rmsnorm(x1, nxw_ref[...]).astype(BF16)
    q = jnp.dot(h2, wq_ref[...], preferred_element_type=F32) * (XATTN_HEAD_DIM ** -0.5)
    q = q.astype(BF16)
    kv = kv_ref[0]
    outs = []
    for h in range(XATTN_HEADS):
        lo = h * XATTN_HEAD_DIM
        kh = kv[:, lo:lo + XATTN_HEAD_DIM]
        vh = kv[:, XATTN_WIDTH + lo:XATTN_WIDTH + lo + XATTN_HEAD_DIM]
        s = lax.dot_general(q[:, lo:lo + XATTN_HEAD_DIM], kh, (((1,), (1,)), ((), ())),
                            preferred_element_type=F32)
        s = s - jnp.max(s, axis=1, keepdims=True)
        e = jnp.exp(s)
        pm = e / jnp.sum(e, axis=1, keepdims=True)
        outs.append(jnp.dot(pm.astype(BF16), vh, preferred_element_type=F32).astype(BF16))
    o = jnp.concatenate(outs, axis=1)
    x2 = x1 + jnp.dot(o, wo_ref[...], preferred_element_type=F32)
    x2_ref[...] = x2
    h3 = _rmsnorm(x2, nfw_ref[...])
    h3_ref[...] = h3.astype(h3_ref.dtype)
    if with_router:
        lane = lax.broadcasted_iota(jnp.int32, (1, LANES), 1)
        lane_f = lane.astype(F32)
        h_hi = h3.astype(BF16)
        h_lo = (h3 - h_hi.astype(F32)).astype(BF16)
        rw = rw_ref[...]
        w_hi = rw.astype(BF16)
        w_lo = (rw - w_hi.astype(F32)).astype(BF16)
        logits = (jnp.dot(h_hi, w_hi, preferred_element_type=F32)
                  + jnp.dot(h_hi, w_lo, preferred_element_type=F32)
                  + jnp.dot(h_lo, w_hi, preferred_element_type=F32))
        logits = jnp.where(lane < N_EXPERTS, logits, -jnp.inf)
        m1 = jnp.max(logits, axis=1, keepdims=True)
        i1 = jnp.min(jnp.where(logits == m1, lane_f, float(LANES)), axis=1, keepdims=True)
        rest_l = jnp.where(lane_f == i1, -jnp.inf, logits)
        m2 = jnp.max(rest_l, axis=1, keepdims=True)
        i2 = jnp.min(jnp.where(rest_l == m2, lane_f, float(LANES)), axis=1, keepdims=True)
        e2 = jnp.exp(m2 - m1)
        w1 = 1.0 / (1.0 + e2)
        w2 = e2 * w1
        route = jnp.where(lane == 0, i1, 0.0)
        route = jnp.where(lane == 1, i2, route)
        route = jnp.where(lane == 2, w1, route)
        route = jnp.where(lane == 3, w2, route)
        route_ref[...] = route


def _post_mix(hmc, hf, x, w_out, nx_w, w_q, kv, w_o, nf_w, router_w, batch):
    t, d = x.shape
    tm = min(TM_PROJ, t // batch)
    per_b = (t // batch) // tm
    with_router = router_w is not None
    const = lambda a: pl.BlockSpec(a.shape, lambda i: (0,) * a.ndim)
    in_specs = [
        pl.BlockSpec((tm, hmc.shape[1]), lambda i: (i, 0)),
        pl.BlockSpec((tm, hf.shape[1]), lambda i: (i, 0)),
        pl.BlockSpec((tm, d), lambda i: (i, 0)),
        const(w_out), const(nx_w), const(w_q),
        pl.BlockSpec((1,) + kv.shape[1:], lambda i: (i // per_b, 0, 0)),
        const(w_o), const(nf_w),
    ]
    args = [hmc, hf, x, w_out, nx_w, w_q, kv, w_o, nf_w]
    out_specs = [pl.BlockSpec((tm, d), lambda i: (i, 0)), pl.BlockSpec((tm, d), lambda i: (i, 0))]
    out_shape = [jax.ShapeDtypeStruct((t, d), F32),
                 jax.ShapeDtypeStruct((t, d), F32 if with_router else BF16)]
    if with_router:
        in_specs.append(const(router_w))
        args.append(router_w)
        out_specs.append(pl.BlockSpec((tm, LANES), lambda i: (i, 0)))
        out_shape.append(jax.ShapeDtypeStruct((t, LANES), F32))
    return pl.pallas_call(
        functools.partial(_post_mix_kernel, with_router),
        grid=(t // tm,),
        in_specs=in_specs, out_specs=out_specs, out_shape=out_shape,
        compiler_params=_cparams(("parallel",)),
        name="post_mix_router" if with_router else "post_mix",
    )(*args)


def _ffn_kernel(with_residual, te_ref, nt_ref, x_ref, wg_ref, wu_ref, wd_ref, *rest):
    if with_residual:
        res_ref, y_ref = rest
    else:
        (y_ref,) = rest

    @pl.when(pl.program_id(0) < nt_ref[0])
    def _():
        xb = x_ref[...].astype(BF16)
        acc = res_ref[...] if with_residual else None
        lo = 0
        for fc in FF_CHUNKS:
            a = jnp.dot(xb, wg_ref[0, :, lo:lo + fc], preferred_element_type=F32)
            u = jnp.dot(xb, wu_ref[0, :, lo:lo + fc], preferred_element_type=F32)
            hcur = (a * _sigmoid(a) * u).astype(BF16)
            part = jnp.dot(hcur, wd_ref[0, lo:lo + fc, :], preferred_element_type=F32)
            acc = part if acc is None else acc + part
            lo += fc
        y_ref[...] = acc

    @pl.when(pl.program_id(0) >= nt_ref[0])
    def _():
        y_ref[...] = jnp.zeros_like(y_ref)


def _ffn(x, w_gate, w_up, w_down, tile_expert, n_tiles, residual=None):
    rows, d = x.shape
    tm = min(TM_FFN, rows)
    nt = rows // tm
    ff = w_gate.shape[2]
    assert sum(FF_CHUNKS) == ff
    with_residual = residual is not None

    def row_map(i, te, ntl):
        return (jnp.minimum(i, ntl[0] - 1), 0)

    def w_map(i, te, ntl):
        return (te[jnp.minimum(i, ntl[0] - 1)], 0, 0)

    in_specs = [
        pl.BlockSpec((tm, d), row_map),
        pl.BlockSpec((1, d, ff), w_map),
        pl.BlockSpec((1, d, ff), w_map),
        pl.BlockSpec((1, ff, d), w_map),
    ]
    args = [x, w_gate, w_up, w_down]
    if with_residual:
        in_specs.append(pl.BlockSpec((tm, d), row_map))
        args.append(residual)
    return pl.pallas_call(
        functools.partial(_ffn_kernel, with_residual),
        grid_spec=pltpu.PrefetchScalarGridSpec(
            num_scalar_prefetch=2,
            grid=(nt,),
            in_specs=in_specs,
            out_specs=pl.BlockSpec((tm, d), lambda i, te, ntl: (i, 0)),
        ),
        out_shape=jax.ShapeDtypeStruct((rows, d), F32),
        compiler_params=_cparams(("arbitrary",)),
        name="ffn_dense" if with_residual else "ffn_experts",
    )(tile_expert, n_tiles, *args)


def _plan_kernel(route_ref, rank_ref, cnt_ref, carry):
    tp = route_ref.shape[0]

    @pl.when(pl.program_id(0) == 0)
    def _():
        carry[...] = jnp.zeros_like(carry)

    lane_i = lax.broadcasted_iota(jnp.int32, (1, LANES), 1)
    lane = lane_i.astype(F32)
    route = route_ref[...]
    i1 = route[:, 0:1]
    i2 = route[:, 1:2]
    onehot = (lane == i1).astype(F32) + (lane == i2).astype(F32)
    row = lax.broadcasted_iota(jnp.int32, (tp, tp), 0)
    col = lax.broadcasted_iota(jnp.int32, (tp, tp), 1)
    strict = (row > col).astype(BF16)
    before = jnp.dot(strict, onehot.astype(BF16), preferred_element_type=F32) + carry[...]
    r1 = jnp.sum(jnp.where(lane == i1, before, 0.0), axis=1, keepdims=True)
    r2 = jnp.sum(jnp.where(lane == i2, before, 0.0), axis=1, keepdims=True)
    rank_ref[...] = jnp.where(lane_i == 0, r1, jnp.where(lane_i == 1, r2, 0.0))
    total = carry[...] + jnp.sum(onehot, axis=0, keepdims=True)
    carry[...] = total
    cnt_ref[...] = total


def _plan(route):
    t = route.shape[0]
    tp = min(512, t)
    return pl.pallas_call(
        _plan_kernel,
        grid=(t // tp,),
        in_specs=[pl.BlockSpec((tp, LANES), lambda i: (i, 0))],
        out_specs=[pl.BlockSpec((tp, LANES), lambda i: (i, 0)),
                   pl.BlockSpec((1, LANES), lambda i: (0, 0))],
        out_shape=[jax.ShapeDtypeStruct((t, LANES), F32), jax.ShapeDtypeStruct((1, LANES), F32)],
        scratch_shapes=[pltpu.VMEM((1, LANES), F32)],
        compiler_params=_cparams(("arbitrary",)),
        name="route_plan",
    )(route)


def _row_copy(src, src_row, dst, dst_row, sem):
    return pltpu.make_async_copy(src.at[pl.ds(src_row, 1), :], dst.at[pl.ds(dst_row, 1), :], sem)


def _dispatch_kernel(pos_ref, h_ref, xs_in_ref, xs_ref, sem):
    del xs_in_ref
    td = h_ref.shape[0]

    def start(r, carry):
        _row_copy(h_ref, r, xs_ref, pos_ref[0, 0, 2 * r], sem).start()
        _row_copy(h_ref, r, xs_ref, pos_ref[0, 0, 2 * r + 1], sem).start()
        return carry

    def wait(r, carry):
        _row_copy(h_ref, r, xs_ref, pos_ref[0, 0, 2 * r], sem).wait()
        _row_copy(h_ref, r, xs_ref, pos_ref[0, 0, 2 * r + 1], sem).wait()
        return carry

    lax.fori_loop(0, td, start, 0, unroll=DMA_UNROLL)
    lax.fori_loop(0, td, wait, 0, unroll=DMA_UNROLL)


def _dispatch(h3, pos, xs_zero):
    t, d = h3.shape
    td = min(TD_ROUTE, t)
    nt = t // td
    pos3 = pos.reshape(nt, 1, 2 * td)
    return pl.pallas_call(
        _dispatch_kernel,
        grid=(nt,),
        in_specs=[
            pl.BlockSpec((1, 1, 2 * td), lambda i: (i, 0, 0), memory_space=pltpu.SMEM),
            pl.BlockSpec((td, d), lambda i: (i, 0)),
            pl.BlockSpec(memory_space=pl.ANY),
        ],
        out_specs=pl.BlockSpec(memory_space=pl.ANY),
        out_shape=jax.ShapeDtypeStruct(xs_zero.shape, xs_zero.dtype),
        scratch_shapes=[pltpu.SemaphoreType.DMA],
        input_output_aliases={2: 0},
        compiler_params=_cparams(("arbitrary",)),
        name="dispatch",
    )(pos3, h3, xs_zero)


def _combine_kernel(pos_ref, x_ref, route_ref, nw_ref, ys_ref, o_ref, buf, sem):
    td = x_ref.shape[0]

    def start(r, carry):
        _row_copy(ys_ref, pos_ref[0, 0, 2 * r], buf.at[0], r, sem).start()
        _row_copy(ys_ref, pos_ref[0, 0, 2 * r + 1], buf.at[1], r, sem).start()
        return carry

    def wait(r, carry):
        _row_copy(ys_ref, pos_ref[0, 0, 2 * r], buf.at[0], r, sem).wait()
        _row_copy(ys_ref, pos_ref[0, 0, 2 * r + 1], buf.at[1], r, sem).wait()
        return carry

    lax.fori_loop(0, td, start, 0, unroll=DMA_UNROLL)
    lax.fori_loop(0, td, wait, 0, unroll=DMA_UNROLL)
    route = route_ref[...]
    x = x_ref[...] + route[:, 2:3] * buf[0] + route[:, 3:4] * buf[1]
    o_ref[...] = _rmsnorm(x, nw_ref[...])


def _combine(x2, route, pos, ys, norm_w):
    t, d = x2.shape
    td = min(TD_ROUTE, t)
    nt = t // td
    pos3 = pos.reshape(nt, 1, 2 * td)
    return pl.pallas_call(
        _combine_kernel,
        grid=(nt,),
        in_specs=[
            pl.BlockSpec((1, 1, 2 * td), lambda i: (i, 0, 0), memory_space=pltpu.SMEM),
            pl.BlockSpec((td, d), lambda i: (i, 0)),
            pl.BlockSpec((td, LANES), lambda i: (i, 0)),
            pl.BlockSpec((1, d), lambda i: (0, 0)),
            pl.BlockSpec(memory_space=pl.ANY),
        ],
        out_specs=pl.BlockSpec((td, d), lambda i: (i, 0)),
        out_shape=jax.ShapeDtypeStruct((t, d), F32),
        scratch_shapes=[pltpu.VMEM((2, td, d), F32), pltpu.SemaphoreType.DMA],
        compiler_params=_cparams(("arbitrary",)),
        name="combine",
    )(pos3, x2, route, norm_w, ys)


def _split_w_in(w_in):
    mw, cw, fw = MLSTM_WIDTH, CONF_CHANNELS, FOX_WIDTH
    sizes = (2 * mw, mw, mw, MLSTM_HEADS, MLSTM_HEADS, 2 * cw, fw, fw, fw, FOX_HEADS)
    parts, off = [], 0
    for sz in sizes:
        parts.append(w_in[:, off:off + sz])
        off += sz
    m_qk, m_v, m_o, m_i, m_f, c_glu, f_q, f_k, f_v, f_f = parts
    w_main = jnp.concatenate([m_qk, m_v, m_o, c_glu, f_q, f_k, f_v], axis=1).astype(BF16)
    w_gate = jnp.concatenate([m_i, m_f, f_f], axis=1)
    w_gate = jnp.pad(w_gate, ((0, 0), (0, LANES - w_gate.shape[1]))).astype(BF16)
    return w_main, w_gate


def _gate_bias(b_i, b_f, fox_b):
    gb = jnp.concatenate([b_i, b_f, fox_b]).astype(F32)
    return jnp.pad(gb, (0, LANES - gb.shape[0])).reshape(1, LANES)


def _route_tables(route, rank, counts, tm, n_tiles_max):
    cnt = counts[0, :N_EXPERTS].astype(jnp.int32)
    tiles = (cnt + tm - 1) // tm
    tile_end = jnp.cumsum(tiles)
    row_off = (tile_end - tiles) * tm
    idx = route[:, 0:2].astype(jnp.int32)
    pos = row_off[idx] + rank[:, 0:2].astype(jnp.int32)
    tile_ids = jnp.arange(n_tiles_max, dtype=jnp.int32)
    tile_expert = jnp.sum(tile_ids[:, None] >= tile_end[None, :], axis=1).astype(jnp.int32)
    tile_expert = jnp.minimum(tile_expert, N_EXPERTS - 1)
    n_tiles = tile_end[-1:].astype(jnp.int32)
    return pos.reshape(-1), tile_expert, n_tiles


def kernel(x, mem, norm_mix_w, w_in, mlstm_conv_w, mlstm_conv_b, mlstm_b_i, mlstm_b_f, mlstm_norm_w,
           conf_conv_w, conf_conv_b, conf_ln_w, conf_ln_b, fox_b_f, w_out, norm_xattn_w, norm_mem_w,
           xattn_w_q, xattn_w_kv, xattn_w_o, norm_ffn_w, ffn_w_gate, ffn_w_up, ffn_w_down, router_w,
           moe_w_gate, moe_w_up, moe_w_down, norm_final_w):
    batch, seq, d = x.shape
    depth = w_in.shape[0]
    t = batch * seq
    xf = x.reshape(t, d)
    row = lambda a: a.reshape(1, -1).astype(F32)
    out = None
    for l in range(depth):
        w_main, w_gate = _split_w_in(w_in[l])
        z, gates = _in_proj(xf, row(norm_mix_w[l]), w_main, w_gate)
        hmc, c2 = _seq_mix(
            z, gates, _gate_bias(mlstm_b_i[l], mlstm_b_f[l], fox_b_f[l]),
            mlstm_conv_w[l].astype(F32), row(mlstm_conv_b[l]), row(mlstm_norm_w[l]),
            conf_conv_w[l].astype(F32), row(conf_conv_b[l]), row(conf_ln_w[l]), row(conf_ln_b[l]),
            batch)
        ka, qta, vta, fox_stats = _fox_prep(z, c2)
        visits, cref = _fox_tile_plan(fox_stats, batch)
        hf = _fox_attn(ka, qta, vta, visits, cref, batch)
        kv = _mem_kv(mem, row(norm_mem_w[l]), xattn_w_kv[l].astype(BF16))
        dense = l % 2 == 0
        j = l // 2
        rw = None
        if not dense:
            rw = jnp.pad(router_w[j].astype(F32), ((0, 0), (0, LANES - N_EXPERTS)))
        res = _post_mix(hmc, hf, xf, w_out[l].astype(BF16), row(norm_xattn_w[l]),
                        xattn_w_q[l].astype(BF16), kv, xattn_w_o[l].astype(BF16),
                        row(norm_ffn_w[l]), rw, batch)
        if dense:
            x2, h3 = res
            nt = t // min(TM_FFN, t)
            xf = _ffn(h3, ffn_w_gate[j][None].astype(BF16), ffn_w_up[j][None].astype(BF16),
                      ffn_w_down[j][None].astype(BF16), jnp.zeros((nt,), jnp.int32),
                      jnp.full((1,), nt, jnp.int32), residual=x2)
            if l == depth - 1:
                raise NotImplementedError("final norm after a dense layer")
        else:
            x2, h3, route = res
            rank, counts = _plan(route)
            tm = min(TM_FFN, t)
            n_tiles_max = (2 * t) // tm + N_EXPERTS
            pos, tile_expert, n_tiles = _route_tables(route, rank, counts, tm, n_tiles_max)
            xs = _dispatch(h3, pos, jnp.zeros((n_tiles_max * tm, d), F32))
            ys = _ffn(xs, moe_w_gate[j].astype(BF16), moe_w_up[j].astype(BF16),
                      moe_w_down[j].astype(BF16), tile_expert, n_tiles)
            assert l == depth - 1
            out = _combine(x2, route, pos, ys, row(norm_final_w))
    return out.reshape(batch, seq, d)
```

```python
import functools

import jax
import jax.numpy as jnp
from jax import lax
from jax.experimental import pallas as pl
from jax.experimental.pallas import tpu as pltpu

F32 = jnp.float32
BF16 = jnp.bfloat16
EPS = 1e-6

MLSTM_HEADS = 4
HEAD_DIM = 64
MLSTM_WIDTH = MLSTM_HEADS * HEAD_DIM
MLSTM_CONV = 4
CONF_CHANNELS = 256
CONF_KERNEL = 31
FOX_HEADS = 8
FOX_WIDTH = FOX_HEADS * HEAD_DIM
XATTN_HEADS = 4
XATTN_HEAD_DIM = 128
XATTN_WIDTH = XATTN_HEADS * XATTN_HEAD_DIM
N_EXPERTS = 8
ROUTER_ROWS = 16
LANES = 128
SUBLANES = 8

Z_QK = 0
Z_V = 512
Z_O = 768
Z_GLU = 1024
Z_SEQ_WIDTH = 1536
Z_FQ = 1536
Z_FK = 2048
Z_FV = 2560
Z_WIDTH = 3072
G_I = 0
G_F = 4
G_FOX = 8

TM_PROJ = 512
SEQ_CHUNK = 256
CONV_TAIL = 8
CONF_TAIL = 32
FOX_TQ = 512
FOX_TK = 512
TM_FFN = 256
FF_CHUNKS = (1024, 1024, 768)
TD_ROUTE = 256
DMA_UNROLL = 8
VMEM_LIMIT = 56 * 1024 * 1024


def _cparams(sem, vmem=VMEM_LIMIT):
    return pltpu.CompilerParams(dimension_semantics=sem, vmem_limit_bytes=vmem)


def _sigmoid(x):
    return 1.0 / (1.0 + jnp.exp(-x))


def _log_sigmoid(x):
    return jnp.minimum(x, 0.0) - jnp.log(1.0 + jnp.exp(-jnp.abs(x)))


def _rmsnorm(x, w):
    ms = jnp.mean(x * x, axis=-1, keepdims=True)
    return x * lax.rsqrt(ms + EPS) * w


def _in_proj_kernel(x_ref, nw_ref, w_ref, wg_ref, z_ref, g_ref):
    h = _rmsnorm(x_ref[...], nw_ref[...]).astype(BF16)
    for n in range(0, Z_WIDTH, 512):
        z_ref[:, n:n + 512] = jnp.dot(h, w_ref[:, n:n + 512],
                                      preferred_element_type=F32).astype(BF16)
    g_ref[...] = jnp.dot(h, wg_ref[...], preferred_element_type=F32)


def _in_proj(x, norm_w, w_main, w_gate):
    t, d = x.shape
    tm = min(TM_PROJ, t)
    return pl.pallas_call(
        _in_proj_kernel,
        grid=(t // tm,),
        in_specs=[
            pl.BlockSpec((tm, d), lambda i: (i, 0)),
            pl.BlockSpec((1, d), lambda i: (0, 0)),
            pl.BlockSpec((d, Z_WIDTH), lambda i: (0, 0)),
            pl.BlockSpec((d, LANES), lambda i: (0, 0)),
        ],
        out_specs=[
            pl.BlockSpec((tm, Z_WIDTH), lambda i: (i, 0)),
            pl.BlockSpec((tm, LANES), lambda i: (i, 0)),
        ],
        out_shape=[jax.ShapeDtypeStruct((t, Z_WIDTH), BF16),
                   jax.ShapeDtypeStruct((t, LANES), F32)],
        compiler_params=_cparams(("parallel",)),
        name="in_proj",
    )(x, norm_w, w_main, w_gate)


def _seq_kernel(z_ref, g_ref, gb_ref, cw_ref, cb_ref, mnw_ref, ccw_ref, ccb_ref, lnw_ref, lnb_ref,
                out_ref, c_ref,
                qk_buf, u_buf, ush_buf, cstate, nstate, mstate, carry):
    L = z_ref.shape[0]
    W = MLSTM_WIDTH

    @pl.when(pl.program_id(1) == 0)
    def _():
        qk_buf[0:CONV_TAIL, :] = jnp.zeros((CONV_TAIL, 2 * W), F32)
        u_buf[0:CONF_TAIL, :] = jnp.zeros((CONF_TAIL, CONF_CHANNELS), F32)
        cstate[...] = jnp.zeros_like(cstate)
        nstate[...] = jnp.zeros_like(nstate)
        mstate[...] = jnp.zeros_like(mstate)
        carry[...] = jnp.zeros_like(carry)

    lane_g = lax.broadcasted_iota(jnp.int32, (1, LANES), 1)
    lane_w = lax.broadcasted_iota(jnp.int32, (1, W), 1)

    g = g_ref[...] + gb_ref[...]
    is_forget = (lane_g >= G_F) & (lane_g < G_FOX + FOX_HEADS)
    lsg = jnp.where(is_forget, _log_sigmoid(g), 0.0)
    row = lax.broadcasted_iota(jnp.int32, (L, L), 0)
    col = lax.broadcasted_iota(jnp.int32, (L, L), 1)
    causal = row >= col
    tri = causal.astype(F32)
    cs = jnp.dot(tri, lsg, preferred_element_type=F32, precision=lax.Precision.HIGHEST)
    c_all = cs + carry[...]
    c_ref[...] = c_all * 1.4426950408889634
    carry[...] = c_all[L - 1:L, :]

    qk_buf[CONV_TAIL:CONV_TAIL + L, :] = z_ref[:, Z_QK:Z_QK + 2 * W].astype(F32)
    qk = jnp.zeros((L, 2 * W), F32) + cb_ref[...]
    for j in range(MLSTM_CONV):
        off = CONV_TAIL - (MLSTM_CONV - 1) + j
        qk = qk + qk_buf[off:off + L, :] * cw_ref[j:j + 1, :]
    qk_buf[0:CONV_TAIL, :] = qk_buf[L:L + CONV_TAIL, :]
    qk = qk * _sigmoid(qk)
    q = qk[:, 0:W]
    k = qk[:, W:2 * W] * (HEAD_DIM ** -0.5)
    v = z_ref[:, Z_V:Z_V + W]
    q_b = q.astype(BF16)
    k_b = k.astype(BF16)

    cs_t = cs.T
    g_t = g.T
    m_prev = mstate[...]
    n_prev = nstate[...]
    c_prev = cstate[...]
    qn = q * n_prev
    q_c = jnp.dot(q_b, c_prev.astype(BF16), preferred_element_type=F32)

    num = jnp.zeros((L, W), F32)
    w_inter_l = jnp.zeros((L, W), F32)
    denom_l = jnp.ones((L, W), F32)
    wg_l = jnp.zeros((L, W), F32)
    decay_l = jnp.zeros((1, W), F32)
    m_new_row = m_prev
    for h in range(MLSTM_HEADS):
        hmask = (lane_w // HEAD_DIM) == h
        b_col = cs[:, G_F + h:G_F + h + 1]
        b_row = cs_t[G_F + h:G_F + h + 1, :]
        li_row = g_t[G_I + h:G_I + h + 1, :]
        li_col = g[:, G_I + h:G_I + h + 1]
        m_h = m_prev[:, h:h + 1]
        d_log = jnp.where(causal, b_col - b_row + li_row, -jnp.inf)
        inter = b_col + m_h
        m_t = jnp.maximum(jnp.max(d_log, axis=1, keepdims=True), inter)
        dmat = jnp.exp(d_log - m_t)
        q_h = jnp.where(hmask, q_b, jnp.zeros_like(q_b))
        s = lax.dot_general(q_h, k_b, (((1,), (1,)), ((), ())), preferred_element_type=F32) * dmat
        w_inter = jnp.exp(inter - m_t)
        pv = jnp.dot(s.astype(BF16), v, preferred_element_type=F32)
        num = jnp.where(hmask, pv, num)
        qn_h = jnp.sum(jnp.where(hmask, qn, 0.0), axis=1, keepdims=True)
        den = jnp.sum(s, axis=1, keepdims=True) + w_inter * qn_h
        dn = jnp.maximum(jnp.abs(den), jnp.exp(-m_t))
        w_inter_l = jnp.where(hmask, w_inter, w_inter_l)
        denom_l = jnp.where(hmask, dn, denom_l)
        b_tot = cs[L - 1:L, G_F + h:G_F + h + 1]
        g_h = b_tot - b_col + li_col
        m_new = jnp.maximum(b_tot + m_h, jnp.max(g_h, axis=0, keepdims=True))
        wg_l = jnp.where(hmask, jnp.exp(g_h - m_new), wg_l)
        decay_l = jnp.where(hmask, jnp.exp(b_tot + m_h - m_new), decay_l)
        m_new_row = jnp.where(lane_g == h, m_new, m_new_row)

    hout = (num + w_inter_l * q_c) / denom_l
    hsq = hout * hout
    rs_l = jnp.zeros((L, W), F32)
    for h in range(MLSTM_HEADS):
        hmask = (lane_w // HEAD_DIM) == h
        ms = jnp.sum(jnp.where(hmask, hsq, 0.0), axis=1, keepdims=True) * (1.0 / HEAD_DIM)
        rs_l = jnp.where(hmask, lax.rsqrt(ms + EPS), rs_l)
    o_gate = _sigmoid(z_ref[:, Z_O:Z_O + W].astype(F32))
    out_ref[:, 0:W] = (hout * rs_l * mnw_ref[...] * o_gate).astype(BF16)

    kw = k * wg_l
    upd = lax.dot_general(kw.astype(BF16), v, (((0,), (0,)), ((), ())), preferred_element_type=F32)
    rk = lax.broadcasted_iota(jnp.int32, (W, W), 0) // HEAD_DIM
    rv = lax.broadcasted_iota(jnp.int32, (W, W), 1) // HEAD_DIM
    cstate[...] = decay_l * c_prev + jnp.where(rk == rv, upd, 0.0)
    nstate[...] = decay_l * n_prev + jnp.sum(kw, axis=0, keepdims=True)
    mstate[...] = m_new_row

    a = z_ref[:, Z_GLU:Z_GLU + CONF_CHANNELS].astype(F32)
    gg = z_ref[:, Z_GLU + CONF_CHANNELS:Z_GLU + 2 * CONF_CHANNELS].astype(F32)
    u_buf[CONF_TAIL:CONF_TAIL + L, :] = a * _sigmoid(gg)
    span = L + CONF_TAIL - SUBLANES
    for r in range(1, SUBLANES):
        ush_buf[r - 1] = u_buf[r:r + span, :]
    hc = jnp.zeros((L, CONF_CHANNELS), F32) + ccb_ref[...]
    for j in range(CONF_KERNEL):
        off = CONF_TAIL - (CONF_KERNEL - 1) + j
        base, r = off - off % SUBLANES, off % SUBLANES
        src = u_buf[base:base + L, :] if r == 0 else ush_buf[r - 1, base:base + L, :]
        hc = hc + src * ccw_ref[j:j + 1, :]
    u_buf[0:CONF_TAIL, :] = u_buf[L:L + CONF_TAIL, :]
    mu = jnp.mean(hc, axis=1, keepdims=True)
    xc = hc - mu
    var = jnp.mean(xc * xc, axis=1, keepdims=True)
    y = xc * lax.rsqrt(var + EPS) * lnw_ref[...] + lnb_ref[...]
    out_ref[:, W:W + CONF_CHANNELS] = (y * _sigmoid(y)).astype(BF16)


def _seq_mix(z, gates, gate_bias, conv_w, conv_b, mnorm_w, cconv_w, cconv_b, ln_w, ln_b, batch):
    t = z.shape[0]
    s = t // batch
    L = min(SEQ_CHUNK, s)
    nc = s // L
    W = MLSTM_WIDTH
    full = lambda a: pl.BlockSpec(a.shape, lambda b, c: (0,) * a.ndim)
    return pl.pallas_call(
        _seq_kernel,
        grid=(batch, nc),
        in_specs=[
            pl.BlockSpec((L, Z_SEQ_WIDTH), lambda b, c: (b * nc + c, 0)),
            pl.BlockSpec((L, LANES), lambda b, c: (b * nc + c, 0)),
            full(gate_bias), full(conv_w), full(conv_b), full(mnorm_w),
            full(cconv_w), full(cconv_b), full(ln_w), full(ln_b),
        ],
        out_specs=[
            pl.BlockSpec((L, W + CONF_CHANNELS), lambda b, c: (b * nc + c, 0)),
            pl.BlockSpec((L, LANES), lambda b, c: (b * nc + c, 0)),
        ],
        out_shape=[jax.ShapeDtypeStruct((t, W + CONF_CHANNELS), BF16),
                   jax.ShapeDtypeStruct((t, LANES), F32)],
        scratch_shapes=[
            pltpu.VMEM((CONV_TAIL + L + CONV_TAIL, 2 * W), F32),
            pltpu.VMEM((CONF_TAIL + L + CONF_TAIL, CONF_CHANNELS), F32),
            pltpu.VMEM((SUBLANES - 1, L + CONF_TAIL - SUBLANES, CONF_CHANNELS), F32),
            pltpu.VMEM((W, W), F32),
            pltpu.VMEM((1, W), F32),
            pltpu.VMEM((1, LANES), F32),
            pltpu.VMEM((1, LANES), F32),
        ],
        compiler_params=_cparams(("arbitrary", "arbitrary")),
        name="seq_mix",
    )(z, gates, gate_bias, conv_w, conv_b, mnorm_w, cconv_w, cconv_b, ln_w, ln_b)


AUG_NEG_A = 0
AUG_ONE = 3
ST_ROWS = 8
ST_NORMS = 0
ST_FIRST = 1
ST_LAST = 2
FOX_SKIP_LOG2 = 160.0
NORM_SLACK = 1.01


def _split3(x):
    hi = x.astype(BF16).astype(F32)
    mid = (x - hi).astype(BF16).astype(F32)
    lo = (x - hi - mid).astype(BF16).astype(F32)
    return hi, mid, lo


def _fox_prep_kernel(zq_ref, zk_ref, zv_ref, c2_ref, ka_ref, qta_ref, vta_ref, st_ref):
    tp = zq_ref.shape[0]
    c2 = c2_ref[...]
    a = c2 - c2[0:1, :]
    c2_t = c2.T
    b_t = c2_t - c2_t[:, 0:1]
    lane = lax.broadcasted_iota(jnp.int32, (1, LANES), 1)
    row = lax.broadcasted_iota(jnp.int32, (LANES, 1), 0)
    q_scale = (HEAD_DIM ** -0.5) * 1.4426950408889634
    norms = jnp.zeros((1, LANES), F32)
    for p in range(FOX_HEADS // 2):
        sl = slice(p * LANES, (p + 1) * LANES)
        kp = zk_ref[:, sl].astype(F32)
        q_r = (zq_ref[:, sl].astype(F32) * q_scale).astype(BF16).astype(F32)
        q_t = q_r.T
        v_t = zv_ref[:, sl].astype(F32).T
        sq = jnp.concatenate([q_r * q_r, kp * kp], axis=1).astype(BF16)
        grp = lax.broadcasted_iota(jnp.int32, (2 * LANES, LANES), 0) // HEAD_DIM
        dst = jnp.where(grp < 2, 2 * p + grp, FOX_HEADS + 2 * p + grp - 2)
        sel = (lax.broadcasted_iota(jnp.int32, (2 * LANES, LANES), 1) == dst).astype(BF16)
        n2 = jnp.max(jnp.dot(sq, sel, preferred_element_type=F32), axis=0, keepdims=True)
        norms = jnp.maximum(norms, jnp.sqrt(n2) * NORM_SLACK)
        for hh in range(2):
            h = 2 * p + hh
            own_lo = hh * HEAD_DIM
            o = (1 - hh) * HEAD_DIM
            a_hi, a_mid, a_lo = _split3(a[:, G_FOX + h:G_FOX + h + 1])
            b_hi, b_mid, b_lo = _split3(b_t[G_FOX + h:G_FOX + h + 1, :])
            own_lane = (lane >= own_lo) & (lane < own_lo + HEAD_DIM)
            own_row = (row >= own_lo) & (row < own_lo + HEAD_DIM)
            ones_l = ((lane >= o + AUG_ONE) & (lane < o + AUG_ONE + 3)).astype(F32)
            ka = jnp.where(own_lane, kp, ones_l)
            ka = jnp.where(lane == o + AUG_NEG_A, -a_hi, ka)
            ka = jnp.where(lane == o + AUG_NEG_A + 1, -a_mid, ka)
            ka = jnp.where(lane == o + AUG_NEG_A + 2, -a_lo, ka)
            ones_r = ((row >= o + AUG_NEG_A) & (row < o + AUG_NEG_A + 3)).astype(F32)
            qa = jnp.where(own_row, q_t, ones_r)
            qa = jnp.where(row == o + AUG_ONE, b_hi, qa)
            qa = jnp.where(row == o + AUG_ONE + 1, b_mid, qa)
            qa = jnp.where(row == o + AUG_ONE + 2, b_lo, qa)
            va = jnp.where(own_row, v_t, (row == o).astype(F32))
            hs = slice(h * LANES, (h + 1) * LANES)
            ka_ref[:, hs] = ka.astype(BF16)
            qta_ref[hs, :] = qa.astype(BF16)
            vta_ref[hs, :] = va.astype(BF16)
    st_ref[ST_NORMS:ST_NORMS + 1, :] = norms
    st_ref[ST_FIRST:ST_FIRST + 1, :] = c2[0:1, :]
    st_ref[ST_LAST:ST_LAST + 1, :] = c2[tp - 1:tp, :]
    st_ref[ST_LAST + 1:, :] = jnp.zeros((ST_ROWS - ST_LAST - 1, LANES), F32)


def _fox_prep(z, c2):
    t = z.shape[0]
    tp = min(FOX_TQ, t)
    width = FOX_HEADS * LANES
    return pl.pallas_call(
        _fox_prep_kernel,
        grid=(t // tp,),
        in_specs=[
            pl.BlockSpec((tp, FOX_WIDTH), lambda i: (i, Z_FQ // FOX_WIDTH)),
            pl.BlockSpec((tp, FOX_WIDTH), lambda i: (i, Z_FK // FOX_WIDTH)),
            pl.BlockSpec((tp, FOX_WIDTH), lambda i: (i, Z_FV // FOX_WIDTH)),
            pl.BlockSpec((tp, LANES), lambda i: (i, 0)),
        ],
        out_specs=[
            pl.BlockSpec((tp, width), lambda i: (i, 0)),
            pl.BlockSpec((width, tp), lambda i: (0, i)),
            pl.BlockSpec((width, tp), lambda i: (0, i)),
            pl.BlockSpec((ST_ROWS, LANES), lambda i: (i, 0)),
        ],
        out_shape=[jax.ShapeDtypeStruct((t, width), BF16),
                   jax.ShapeDtypeStruct((width, t), BF16),
                   jax.ShapeDtypeStruct((width, t), BF16),
                   jax.ShapeDtypeStruct((t // tp * ST_ROWS, LANES), F32)],
        compiler_params=_cparams(("parallel",)),
        name="fox_prep",
    )(z, z, z, c2)


def _fox_tile_plan(stats, batch):
    st = stats.reshape(batch, -1, ST_ROWS, LANES)
    nq = st.shape[1]
    qmax = st[:, :, ST_NORMS, 0:FOX_HEADS]
    kmax = st[:, :, ST_NORMS, FOX_HEADS:2 * FOX_HEADS]
    first = st[:, :, ST_FIRST, G_FOX:G_FOX + FOX_HEADS]
    last = st[:, :, ST_LAST, G_FOX:G_FOX + FOX_HEADS]
    ub = (qmax[:, :, None, :] * (kmax[:, None, :, :] + kmax[:, :, None, :])
          + first[:, :, None, :] - last[:, None, :, :])
    ti = jnp.arange(nq)[:, None]
    tj = jnp.arange(nq)[None, :]
    need = (tj == ti) | ((tj < ti) & ~(ub < -FOX_SKIP_LOG2).transpose(0, 3, 1, 2))
    jmin = jnp.min(jnp.where(need, tj, nq), axis=-1)
    count = jnp.arange(nq)[None, None, :] - jmin + 1
    count = jnp.max(count.reshape(batch, FOX_HEADS // 2, 2, nq), axis=2)
    cref = first.transpose(0, 2, 1).reshape(-1)
    return count.reshape(-1).astype(jnp.int32), cref


def _fox_kernel(cnt_ref, cref_ref, qta_ref, ka_ref, vta_ref, o_ref,
                m_sc, acc_sc, s0_sc, cm0_sc, s1_sc, cm1_sc):
    tq = qta_ref.shape[1]
    tk = tq
    b = pl.program_id(0)
    p = pl.program_id(1)
    i = pl.program_id(2)
    nq = pl.num_programs(2)
    for hh in range(2):
        m_sc[hh] = jnp.full((1, tq), -jnp.inf, F32)
        acc_sc[hh] = jnp.zeros((LANES, tq), F32)

    slots = ((s0_sc, cm0_sc), (s1_sc, cm1_sc))

    def scores(j, diagonal, slot):
        s_sc, cm_sc = slots[slot]
        start = pl.multiple_of(j * tk, tk)
        for hh in range(2):
            hs = slice(hh * LANES, (hh + 1) * LANES)
            s = jnp.dot(ka_ref[pl.ds(start, tk), hs], qta_ref[hs, :],
                        preferred_element_type=F32)
            if diagonal:
                kr = lax.broadcasted_iota(jnp.int32, (tk, tq), 0)
                qc = lax.broadcasted_iota(jnp.int32, (tk, tq), 1)
                s = jnp.where(kr <= qc, s, -jnp.inf)
            s_sc[hh] = s
            cm_sc[hh] = jnp.max(s, axis=0, keepdims=True)

    def softmax_pv(j, slot):
        s_sc, cm_sc = slots[slot]
        start = pl.multiple_of(j * tk, tk)
        for hh in range(2):
            hs = slice(hh * LANES, (hh + 1) * LANES)
            base = (b * FOX_HEADS + 2 * p + hh) * nq
            delta = cref_ref[base + i] - cref_ref[base + j]
            m_old = m_sc[hh]
            m_new = jnp.maximum(m_old, cm_sc[hh] + delta)
            alpha = jnp.exp2(m_old - m_new)
            pm = jnp.exp2(s_sc[hh] - (m_new - delta)).astype(BF16)
            acc_sc[hh] = alpha * acc_sc[hh] + jnp.dot(vta_ref[hs, pl.ds(start, tk)], pm,
                                                      preferred_element_type=F32)
            m_sc[hh] = m_new

    n_visits = cnt_ref[(b * pl.num_programs(1) + p) * nq + i]
    scores(i, True, 0)

    def body(t, carry):
        for parity in range(2):
            @pl.when(t % 2 == parity)
            def _():
                scores(i - t - 1, False, 1 - parity)
                softmax_pv(i - t, parity)
        return carry

    lax.fori_loop(0, n_visits - 1, body, 0)
    for parity in range(2):
        @pl.when((n_visits - 1) % 2 == parity)
        def _():
            softmax_pv(i - n_visits + 1, parity)
    acc0 = acc_sc[0]
    acc1 = acc_sc[1]
    row = lax.broadcasted_iota(jnp.int32, (LANES, 1), 0)
    o_t = jnp.where(row < HEAD_DIM, acc0 / acc0[HEAD_DIM:HEAD_DIM + 1, :], acc1 / acc1[0:1, :])
    o_ref[...] = o_t.T.astype(BF16)


def _fox_attn(ka, qta, vta, visits, cref, batch):
    t = ka.shape[0]
    s = t // batch
    tq = min(FOX_TQ, s)
    nq = s // tq
    npairs = FOX_HEADS // 2
    return pl.pallas_call(
        _fox_kernel,
        grid_spec=pltpu.PrefetchScalarGridSpec(
            num_scalar_prefetch=2,
            grid=(batch, npairs, nq),
            in_specs=[
                pl.BlockSpec((2 * LANES, tq), lambda b, p, i, nv, cr: (p, b * nq + i)),
                pl.BlockSpec((s, 2 * LANES), lambda b, p, i, nv, cr: (b, p)),
                pl.BlockSpec((2 * LANES, s), lambda b, p, i, nv, cr: (p, b)),
            ],
            out_specs=pl.BlockSpec((tq, LANES), lambda b, p, i, nv, cr: (b * nq + i, p)),
            scratch_shapes=[
                pltpu.VMEM((2, 1, tq), F32),
                pltpu.VMEM((2, LANES, tq), F32),
                pltpu.VMEM((2, tq, tq), F32),
                pltpu.VMEM((2, 1, tq), F32),
                pltpu.VMEM((2, tq, tq), F32),
                pltpu.VMEM((2, 1, tq), F32),
            ],
        ),
        out_shape=jax.ShapeDtypeStruct((t, FOX_WIDTH), BF16),
        compiler_params=_cparams(("parallel", "parallel", "arbitrary")),
        name="fox_attn",
    )(visits, cref, qta, ka, vta)


def _mem_kv_kernel(mem_ref, nw_ref, w_ref, o_ref):
    h = _rmsnorm(mem_ref[0], nw_ref[...]).astype(BF16)
    o_ref[0] = jnp.dot(h, w_ref[...], preferred_element_type=F32).astype(BF16)


def _mem_kv(mem, norm_w, w_kv):
    b, m, d = mem.shape
    return pl.pallas_call(
        _mem_kv_kernel,
        grid=(b,),
        in_specs=[
            pl.BlockSpec((1, m, d), lambda i: (i, 0, 0)),
            pl.BlockSpec((1, d), lambda i: (0, 0)),
            pl.BlockSpec(w_kv.shape, lambda i: (0, 0)),
        ],
        out_specs=pl.BlockSpec((1, m, w_kv.shape[1]), lambda i: (i, 0, 0)),
        out_shape=jax.ShapeDtypeStruct((b, m, w_kv.shape[1]), BF16),
        compiler_params=_cparams(("parallel",)),
        name="mem_kv",
    )(mem, norm_w, w_kv)


def _post_mix_kernel(with_router, hmc_ref, hf_ref, x_ref, wout_ref, nxw_ref, wq_ref, kv_ref, wo_ref,
                     nfw_ref, *rest):
    if with_router:
        rw_ref, x2_ref, h3_ref, route_ref = rest
    else:
        x2_ref, h3_ref = rest
    half = MLSTM_WIDTH + CONF_CHANNELS
    x1 = (x_ref[...]
          + jnp.dot(hmc_ref[...], wout_ref[0:half, :], preferred_element_type=F32)
          + jnp.dot(hf_ref[...], wout_ref[half:, :], preferred_element_type=F32))
    h2 = _rmsnorm(x1, nxw_ref[...]).astype(BF16)
    q = jnp.dot(h2, wq_ref[...], preferred_element_type=F32) * (XATTN_HEAD_DIM ** -0.5)
    q = q.astype(BF16)
    kv = kv_ref[0]
    outs = []
    for h in range(XATTN_HEADS):
        lo = h * XATTN_HEAD_DIM
        kh = kv[:, lo:lo + XATTN_HEAD_DIM]
        vh = kv[:, XATTN_WIDTH + lo:XATTN_WIDTH + lo + XATTN_HEAD_DIM]
        s = lax.dot_general(q[:, lo:lo + XATTN_HEAD_DIM], kh, (((1,), (1,)), ((), ())),
                            preferred_element_type=F32)
        s = s - jnp.max(s, axis=1, keepdims=True)
        e = jnp.exp(s)
        pm = e / jnp.sum(e, axis=1, keepdims=True)
        outs.append(jnp.dot(pm.astype(BF16), vh, preferred_element_type=F32).astype(BF16))
    o = jnp.concatenate(outs, axis=1)
    x2 = x1 + jnp.dot(o, wo_ref[...], preferred_element_type=F32)
    x2_ref[...] = x2
    h3 = _rmsnorm(x2, nfw_ref[...])
    h3_ref[...] = h3.astype(h3_ref.dtype)
    if with_router:
        h_hi = h3.astype(BF16)
        h_lo = (h3 - h_hi.astype(F32)).astype(BF16)
        rw = rw_ref[...]
        w_hi = rw.astype(BF16)
        w_lo = (rw - w_hi.astype(F32)).astype(BF16)
        nt_dims = (((1,), (1,)), ((), ()))
        on_hi = lax.dot_general(jnp.concatenate([w_hi, w_lo], axis=0), h_hi, nt_dims,
                                preferred_element_type=F32)
        on_lo = lax.dot_general(w_hi, h_lo, nt_dims, preferred_element_type=F32)
        logits_t = on_hi[0:ROUTER_ROWS] + on_hi[ROUTER_ROWS:] + on_lo
        le = logits_t[0:N_EXPERTS, :]
        eid = lax.broadcasted_iota(jnp.int32, (N_EXPERTS, 1), 0).astype(F32)
        m1 = jnp.max(le, axis=0, keepdims=True)
        i1 = jnp.min(jnp.where(le == m1, eid, float(N_EXPERTS)), axis=0, keepdims=True)
        rest_l = jnp.where(eid == i1, -jnp.inf, le)
        m2 = jnp.max(rest_l, axis=0, keepdims=True)
        i2 = jnp.min(jnp.where(rest_l == m2, eid, float(N_EXPERTS)), axis=0, keepdims=True)
        e2 = jnp.exp(m2 - m1)
        w1 = 1.0 / (1.0 + e2)
        w2 = e2 * w1
        r8 = jnp.where(eid == 0, i1, jnp.where(eid == 1, i2, jnp.where(eid == 2, w1,
                       jnp.where(eid == 3, w2, 0.0))))
        route_t = jnp.concatenate([r8, jnp.zeros((LANES - N_EXPERTS, r8.shape[1]), F32)], axis=0)
        route_ref[...] = route_t.T


def _post_mix(hmc, hf, x, w_out, nx_w, w_q, kv, w_o, nf_w, router_w, batch):
    t, d = x.shape
    tm = min(TM_PROJ, t // batch)
    per_b = (t // batch) // tm
    with_router = router_w is not None
    const = lambda a: pl.BlockSpec(a.shape, lambda i: (0,) * a.ndim)
    in_specs = [
        pl.BlockSpec((tm, hmc.shape[1]), lambda i: (i, 0)),
        pl.BlockSpec((tm, hf.shape[1]), lambda i: (i, 0)),
        pl.BlockSpec((tm, d), lambda i: (i, 0)),
        const(w_out), const(nx_w), const(w_q),
        pl.BlockSpec((1,) + kv.shape[1:], lambda i: (i // per_b, 0, 0)),
        const(w_o), const(nf_w),
    ]
    args = [hmc, hf, x, w_out, nx_w, w_q, kv, w_o, nf_w]
    out_specs = [pl.BlockSpec((tm, d), lambda i: (i, 0)), pl.BlockSpec((tm, d), lambda i: (i, 0))]
    out_shape = [jax.ShapeDtypeStruct((t, d), F32),
                 jax.ShapeDtypeStruct((t, d), F32 if with_router else BF16)]
    if with_router:
        in_specs.append(const(router_w))
        args.append(router_w)
        out_specs.append(pl.BlockSpec((tm, LANES), lambda i: (i, 0)))
        out_shape.append(jax.ShapeDtypeStruct((t, LANES), F32))
    return pl.pallas_call(
        functools.partial(_post_mix_kernel, with_router),
        grid=(t // tm,),
        in_specs=in_specs, out_specs=out_specs, out_shape=out_shape,
        compiler_params=_cparams(("parallel",)),
        name="post_mix_router" if with_router else "post_mix",
    )(*args)


def _ffn_kernel(with_residual, te_ref, nt_ref, x_ref, wg_ref, wu_ref, wd_ref, *rest):
    if with_residual:
        res_ref, y_ref = rest
    else:
        (y_ref,) = rest

    @pl.when(pl.program_id(0) < nt_ref[0])
    def _():
        xb = x_ref[...].astype(BF16)
        acc = res_ref[...] if with_residual else None
        lo = 0
        for fc in FF_CHUNKS:
            a = jnp.dot(xb, wg_ref[0, :, lo:lo + fc], preferred_element_type=F32)
            u = jnp.dot(xb, wu_ref[0, :, lo:lo + fc], preferred_element_type=F32)
            hcur = (a * _sigmoid(a) * u).astype(BF16)
            part = jnp.dot(hcur, wd_ref[0, lo:lo + fc, :], preferred_element_type=F32)
            acc = part if acc is None else acc + part
            lo += fc
        y_ref[...] = acc

    @pl.when(pl.program_id(0) >= nt_ref[0])
    def _():
        y_ref[...] = jnp.zeros_like(y_ref)


def _ffn(x, w_gate, w_up, w_down, tile_expert, n_tiles, residual=None):
    rows, d = x.shape
    tm = min(TM_FFN, rows)
    nt = rows // tm
    ff = w_gate.shape[2]
    assert sum(FF_CHUNKS) == ff
    with_residual = residual is not None

    def row_map(i, te, ntl):
        return (jnp.minimum(i, ntl[0] - 1), 0)

    def w_map(i, te, ntl):
        return (te[jnp.minimum(i, ntl[0] - 1)], 0, 0)

    in_specs = [
        pl.BlockSpec((tm, d), row_map),
        pl.BlockSpec((1, d, ff), w_map),
        pl.BlockSpec((1, d, ff), w_map),
        pl.BlockSpec((1, ff, d), w_map),
    ]
    args = [x, w_gate, w_up, w_down]
    if with_residual:
        in_specs.append(pl.BlockSpec((tm, d), row_map))
        args.append(residual)
    return pl.pallas_call(
        functools.partial(_ffn_kernel, with_residual),
        grid_spec=pltpu.PrefetchScalarGridSpec(
            num_scalar_prefetch=2,
            grid=(nt,),
            in_specs=in_specs,
            out_specs=pl.BlockSpec((tm, d), lambda i, te, ntl: (i, 0)),
        ),
        out_shape=jax.ShapeDtypeStruct((rows, d), F32),
        compiler_params=_cparams(("arbitrary",)),
        name="ffn_dense" if with_residual else "ffn_experts",
    )(tile_expert, n_tiles, *args)


def _plan_kernel(route_ref, rank_ref, cnt_ref, carry):
    tp = route_ref.shape[0]

    @pl.when(pl.program_id(0) == 0)
    def _():
        carry[...] = jnp.zeros_like(carry)

    lane_i = lax.broadcasted_iota(jnp.int32, (1, LANES), 1)
    lane = lane_i.astype(F32)
    route = route_ref[...]
    i1 = route[:, 0:1]
    i2 = route[:, 1:2]
    onehot = (lane == i1).astype(F32) + (lane == i2).astype(F32)
    row = lax.broadcasted_iota(jnp.int32, (tp, tp), 0)
    col = lax.broadcasted_iota(jnp.int32, (tp, tp), 1)
    strict = (row > col).astype(BF16)
    before = jnp.dot(strict, onehot.astype(BF16), preferred_element_type=F32) + carry[...]
    r1 = jnp.sum(jnp.where(lane == i1, before, 0.0), axis=1, keepdims=True)
    r2 = jnp.sum(jnp.where(lane == i2, before, 0.0), axis=1, keepdims=True)
    rank_ref[...] = jnp.where(lane_i == 0, r1, jnp.where(lane_i == 1, r2, 0.0))
    total = carry[...] + jnp.sum(onehot, axis=0, keepdims=True)
    carry[...] = total
    cnt_ref[...] = total


def _plan(route):
    t = route.shape[0]
    tp = min(512, t)
    return pl.pallas_call(
        _plan_kernel,
        grid=(t // tp,),
        in_specs=[pl.BlockSpec((tp, LANES), lambda i: (i, 0))],
        out_specs=[pl.BlockSpec((tp, LANES), lambda i: (i, 0)),
                   pl.BlockSpec((1, LANES), lambda i: (0, 0))],
        out_shape=[jax.ShapeDtypeStruct((t, LANES), F32), jax.ShapeDtypeStruct((1, LANES), F32)],
        scratch_shapes=[pltpu.VMEM((1, LANES), F32)],
        compiler_params=_cparams(("arbitrary",)),
        name="route_plan",
    )(route)


def _row_copy(src, src_row, dst, dst_row, sem):
    return pltpu.make_async_copy(src.at[pl.ds(src_row, 1), :], dst.at[pl.ds(dst_row, 1), :], sem)


def _wait_rows(src, dst, sem, n):
    def wait(r, carry):
        _row_copy(src, 0, dst, 0, sem).wait()
        return carry
    lax.fori_loop(0, n, wait, 0, unroll=DMA_UNROLL)


def _dispatch_kernel(zflag_ref, pos_ref, h_ref, xs_ref, zero_buf, sems, zsem):
    i = pl.program_id(0)
    nt = pl.num_programs(0)
    td = pos_ref.shape[2] // 2
    tm = zero_buf.shape[0]

    @pl.when(i == 0)
    def _():
        zero_buf[...] = jnp.zeros_like(zero_buf)

        def zero_tile(tile, carry):
            @pl.when(zflag_ref[tile] != 0)
            def _():
                row0 = pl.multiple_of(tile * tm, tm)
                cp = pltpu.make_async_copy(zero_buf, xs_ref.at[pl.ds(row0, tm), :], zsem)
                cp.start()
                cp.wait()
            return carry

        lax.fori_loop(0, zflag_ref.shape[0], zero_tile, 0)

    def issue(sem):
        def start(r, carry):
            tok = i * td + r
            _row_copy(h_ref, tok, xs_ref, pos_ref[0, 0, 2 * r], sem).start()
            _row_copy(h_ref, tok, xs_ref, pos_ref[0, 0, 2 * r + 1], sem).start()
            return carry
        lax.fori_loop(0, td, start, 0, unroll=DMA_UNROLL)

    for parity in range(2):
        @pl.when(i % 2 == parity)
        def _():
            issue(sems.at[parity])

            @pl.when(i > 0)
            def _():
                _wait_rows(h_ref, xs_ref, sems.at[1 - parity], 2 * td)

            @pl.when(i == nt - 1)
            def _():
                _wait_rows(h_ref, xs_ref, sems.at[parity], 2 * td)


def _dispatch(h3, pos, zflag, rows, tm):
    t, d = h3.shape
    td = min(TD_ROUTE, t)
    nt = t // td
    pos3 = pos.reshape(nt, 1, 2 * td)
    return pl.pallas_call(
        _dispatch_kernel,
        grid_spec=pltpu.PrefetchScalarGridSpec(
            num_scalar_prefetch=1,
            grid=(nt,),
            in_specs=[
                pl.BlockSpec((1, 1, 2 * td), lambda i, zf: (i, 0, 0), memory_space=pltpu.SMEM),
                pl.BlockSpec(memory_space=pl.ANY),
            ],
            out_specs=pl.BlockSpec(memory_space=pl.ANY),
            scratch_shapes=[pltpu.VMEM((tm, d), F32), pltpu.SemaphoreType.DMA((2,)),
                            pltpu.SemaphoreType.DMA],
        ),
        out_shape=jax.ShapeDtypeStruct((rows, d), F32),
        compiler_params=_cparams(("arbitrary",)),
        name="dispatch",
    )(zflag, pos3, h3)


def _combine_kernel(pos_ref, pos_next_ref, x_ref, route_ref, nw_ref, ys_ref, o_ref, buf, sems):
    i = pl.program_id(0)
    nt = pl.num_programs(0)
    td = x_ref.shape[0]

    def issue(p_ref, slot):
        def start(r, carry):
            _row_copy(ys_ref, p_ref[0, 0, 2 * r], buf.at[slot, 0], r, sems.at[slot]).start()
            _row_copy(ys_ref, p_ref[0, 0, 2 * r + 1], buf.at[slot, 1], r, sems.at[slot]).start()
            return carry
        lax.fori_loop(0, td, start, 0, unroll=DMA_UNROLL)

    @pl.when(i == 0)
    def _():
        issue(pos_ref, 0)

    for parity in range(2):
        @pl.when(i % 2 == parity)
        def _():
            @pl.when(i + 1 < nt)
            def _():
                issue(pos_next_ref, 1 - parity)

            _wait_rows(ys_ref, buf.at[parity, 0], sems.at[parity], 2 * td)
            route = route_ref[...]
            x = x_ref[...] + route[:, 2:3] * buf[parity, 0] + route[:, 3:4] * buf[parity, 1]
            o_ref[...] = _rmsnorm(x, nw_ref[...])


def _combine(x2, route, pos, ys, norm_w):
    t, d = x2.shape
    td = min(TD_ROUTE, t)
    nt = t // td
    pos3 = pos.reshape(nt, 1, 2 * td)
    return pl.pallas_call(
        _combine_kernel,
        grid=(nt,),
        in_specs=[
            pl.BlockSpec((1, 1, 2 * td), lambda i: (i, 0, 0), memory_space=pltpu.SMEM),
            pl.BlockSpec((1, 1, 2 * td), lambda i: (jnp.minimum(i + 1, nt - 1), 0, 0),
                         memory_space=pltpu.SMEM),
            pl.BlockSpec((td, d), lambda i: (i, 0)),
            pl.BlockSpec((td, LANES), lambda i: (i, 0)),
            pl.BlockSpec((1, d), lambda i: (0, 0)),
            pl.BlockSpec(memory_space=pl.ANY),
        ],
        out_specs=pl.BlockSpec((td, d), lambda i: (i, 0)),
        out_shape=jax.ShapeDtypeStruct((t, d), F32),
        scratch_shapes=[pltpu.VMEM((2, 2, td, d), F32), pltpu.SemaphoreType.DMA((2,))],
        compiler_params=_cparams(("arbitrary",)),
        name="combine",
    )(pos3, pos3, x2, route, norm_w, ys)


def _split_w_in(w_in):
    mw, cw, fw = MLSTM_WIDTH, CONF_CHANNELS, FOX_WIDTH
    sizes = (2 * mw, mw, mw, MLSTM_HEADS, MLSTM_HEADS, 2 * cw, fw, fw, fw, FOX_HEADS)
    parts, off = [], 0
    for sz in sizes:
        parts.append(w_in[:, off:off + sz])
        off += sz
    m_qk, m_v, m_o, m_i, m_f, c_glu, f_q, f_k, f_v, f_f = parts
    w_main = jnp.concatenate([m_qk, m_v, m_o, c_glu, f_q, f_k, f_v], axis=1).astype(BF16)
    w_gate = jnp.concatenate([m_i, m_f, f_f], axis=1)
    w_gate = jnp.pad(w_gate, ((0, 0), (0, LANES - w_gate.shape[1]))).astype(BF16)
    return w_main, w_gate


def _gate_bias(b_i, b_f, fox_b):
    gb = jnp.concatenate([b_i, b_f, fox_b]).astype(F32)
    return jnp.pad(gb, (0, LANES - gb.shape[0])).reshape(1, LANES)


def _route_tables(route, rank, counts, tm, n_tiles_max):
    cnt = counts[0, :N_EXPERTS].astype(jnp.int32)
    tiles = (cnt + tm - 1) // tm
    tile_end = jnp.cumsum(tiles)
    row_off = (tile_end - tiles) * tm
    idx = route[:, 0:2].astype(jnp.int32)
    pos = row_off[idx] + rank[:, 0:2].astype(jnp.int32)
    tile_ids = jnp.arange(n_tiles_max, dtype=jnp.int32)
    tile_expert = jnp.sum(tile_ids[:, None] >= tile_end[None, :], axis=1).astype(jnp.int32)
    tile_expert = jnp.minimum(tile_expert, N_EXPERTS - 1)
    n_tiles = tile_end[-1:].astype(jnp.int32)
    is_last = jnp.any((tile_ids[:, None] == tile_end[None, :] - 1) & (tiles[None, :] > 0), axis=1)
    zero_flag = (is_last | (tile_ids >= n_tiles[0])).astype(jnp.int32)
    return pos.reshape(-1), tile_expert, n_tiles, zero_flag


def kernel(x, mem, norm_mix_w, w_in, mlstm_conv_w, mlstm_conv_b, mlstm_b_i, mlstm_b_f, mlstm_norm_w,
           conf_conv_w, conf_conv_b, conf_ln_w, conf_ln_b, fox_b_f, w_out, norm_xattn_w, norm_mem_w,
           xattn_w_q, xattn_w_kv, xattn_w_o, norm_ffn_w, ffn_w_gate, ffn_w_up, ffn_w_down, router_w,
           moe_w_gate, moe_w_up, moe_w_down, norm_final_w):
    batch, seq, d = x.shape
    depth = w_in.shape[0]
    t = batch * seq
    xf = x.reshape(t, d)
    row = lambda a: a.reshape(1, -1).astype(F32)
    out = None
    for l in range(depth):
        w_main, w_gate = _split_w_in(w_in[l])
        z, gates = _in_proj(xf, row(norm_mix_w[l]), w_main, w_gate)
        hmc, c2 = _seq_mix(
            z, gates, _gate_bias(mlstm_b_i[l], mlstm_b_f[l], fox_b_f[l]),
            mlstm_conv_w[l].astype(F32), row(mlstm_conv_b[l]), row(mlstm_norm_w[l]),
            conf_conv_w[l].astype(F32), row(conf_conv_b[l]), row(conf_ln_w[l]), row(conf_ln_b[l]),
            batch)
        ka, qta, vta, fox_stats = _fox_prep(z, c2)
        visits, cref = _fox_tile_plan(fox_stats, batch)
        hf = _fox_attn(ka, qta, vta, visits, cref, batch)
        kv = _mem_kv(mem, row(norm_mem_w[l]), xattn_w_kv[l].astype(BF16))
        dense = l % 2 == 0
        j = l // 2
        rw = None
        if not dense:
            rw = jnp.pad(router_w[j].astype(F32).T, ((0, ROUTER_ROWS - N_EXPERTS), (0, 0)))
        res = _post_mix(hmc, hf, xf, w_out[l].astype(BF16), row(norm_xattn_w[l]),
                        xattn_w_q[l].astype(BF16), kv, xattn_w_o[l].astype(BF16),
                        row(norm_ffn_w[l]), rw, batch)
        if dense:
            x2, h3 = res
            nt = t // min(TM_FFN, t)
            xf = _ffn(h3, ffn_w_gate[j][None].astype(BF16), ffn_w_up[j][None].astype(BF16),
                      ffn_w_down[j][None].astype(BF16), jnp.zeros((nt,), jnp.int32),
                      jnp.full((1,), nt, jnp.int32), residual=x2)
            if l == depth - 1:
                raise NotImplementedError("final norm after a dense layer")
        else:
            x2, h3, route = res
            rank, counts = _plan(route)
            tm = min(TM_FFN, t)
            n_tiles_max = (2 * t) // tm + N_EXPERTS
            pos, tile_expert, n_tiles, zero_flag = _route_tables(route, rank, counts, tm, n_tiles_max)
            xs = _dispatch(h3, pos, zero_flag, n_tiles_max * tm, tm)
            ys = _ffn(xs, moe_w_gate[j].astype(BF16), moe_w_up[j].astype(BF16),
                      moe_w_down[j].astype(BF16), tile_expert, n_tiles)
            assert l == depth - 1
            out = _combine(x2, route, pos, ys, row(norm_final_w))
    return out.reshape(batch, seq, d)
```

```python
import functools

import jax
import jax.numpy as jnp
from jax import lax
from jax.experimental import pallas as pl
from jax.experimental.pallas import tpu as pltpu

F32 = jnp.float32
BF16 = jnp.bfloat16
EPS = 1e-6

MLSTM_HEADS = 4
HEAD_DIM = 64
MLSTM_WIDTH = MLSTM_HEADS * HEAD_DIM
MLSTM_CONV = 4
CONF_CHANNELS = 256
CONF_KERNEL = 31
FOX_HEADS = 8
FOX_WIDTH = FOX_HEADS * HEAD_DIM
XATTN_HEADS = 4
XATTN_HEAD_DIM = 128
XATTN_WIDTH = XATTN_HEADS * XATTN_HEAD_DIM
N_EXPERTS = 8
ROUTER_ROWS = 16
LANES = 128
SUBLANES = 8

Z_QK = 0
Z_V = 512
Z_O = 768
Z_GLU = 1024
Z_SEQ_WIDTH = 1536
Z_FQ = 1536
Z_FK = 2048
Z_FV = 2560
Z_WIDTH = 3072
G_I = 0
G_F = 4
G_FOX = 8

TM_PROJ = 512
SEQ_CHUNK = 256
CONV_TAIL = 8
CONF_TAIL = 32
FOX_TQ = 512
FOX_TK = 512
TM_FFN = 256
FF_CHUNKS = (1024, 1024, 768)
TD_ROUTE = 256
DMA_UNROLL = 8
VMEM_LIMIT = 56 * 1024 * 1024


def _cparams(sem, vmem=VMEM_LIMIT):
    return pltpu.CompilerParams(dimension_semantics=sem, vmem_limit_bytes=vmem)


def _sigmoid(x):
    return 1.0 / (1.0 + jnp.exp(-x))


def _log_sigmoid(x):
    return jnp.minimum(x, 0.0) - jnp.log(1.0 + jnp.exp(-jnp.abs(x)))


def _rmsnorm(x, w):
    ms = jnp.mean(x * x, axis=-1, keepdims=True)
    return x * lax.rsqrt(ms + EPS) * w


def _in_proj_kernel(x_ref, nw_ref, w_ref, wg_ref, z_ref, g_ref):
    h = _rmsnorm(x_ref[...], nw_ref[...]).astype(BF16)
    for n in range(0, Z_WIDTH, 512):
        z_ref[:, n:n + 512] = jnp.dot(h, w_ref[:, n:n + 512],
                                      preferred_element_type=F32).astype(BF16)
    g_ref[...] = jnp.dot(h, wg_ref[...], preferred_element_type=F32)


def _in_proj(x, norm_w, w_main, w_gate):
    t, d = x.shape
    tm = min(TM_PROJ, t)
    return pl.pallas_call(
        _in_proj_kernel,
        grid=(t // tm,),
        in_specs=[
            pl.BlockSpec((tm, d), lambda i: (i, 0)),
            pl.BlockSpec((1, d), lambda i: (0, 0)),
            pl.BlockSpec((d, Z_WIDTH), lambda i: (0, 0)),
            pl.BlockSpec((d, LANES), lambda i: (0, 0)),
        ],
        out_specs=[
            pl.BlockSpec((tm, Z_WIDTH), lambda i: (i, 0)),
            pl.BlockSpec((tm, LANES), lambda i: (i, 0)),
        ],
        out_shape=[jax.ShapeDtypeStruct((t, Z_WIDTH), BF16),
                   jax.ShapeDtypeStruct((t, LANES), F32)],
        compiler_params=_cparams(("parallel",)),
        name="in_proj",
    )(x, norm_w, w_main, w_gate)


def _seq_kernel(z_ref, g_ref, gb_ref, cw_ref, cb_ref, mnw_ref, ccw_ref, ccb_ref, lnw_ref, lnb_ref,
                out_ref, c_ref,
                qk_buf, u_buf, ush_buf, cstate, nstate, mstate, carry):
    L = z_ref.shape[0]
    W = MLSTM_WIDTH

    @pl.when(pl.program_id(1) == 0)
    def _():
        qk_buf[0:CONV_TAIL, :] = jnp.zeros((CONV_TAIL, 2 * W), F32)
        u_buf[0:CONF_TAIL, :] = jnp.zeros((CONF_TAIL, CONF_CHANNELS), F32)
        cstate[...] = jnp.zeros_like(cstate)
        nstate[...] = jnp.zeros_like(nstate)
        mstate[...] = jnp.zeros_like(mstate)
        carry[...] = jnp.zeros_like(carry)

    lane_g = lax.broadcasted_iota(jnp.int32, (1, LANES), 1)
    lane_w = lax.broadcasted_iota(jnp.int32, (1, W), 1)

    g = g_ref[...] + gb_ref[...]
    is_forget = (lane_g >= G_F) & (lane_g < G_FOX + FOX_HEADS)
    lsg = jnp.where(is_forget, _log_sigmoid(g), 0.0)
    row = lax.broadcasted_iota(jnp.int32, (L, L), 0)
    col = lax.broadcasted_iota(jnp.int32, (L, L), 1)
    causal = row >= col
    tri = causal.astype(F32)
    cs = jnp.dot(tri, lsg, preferred_element_type=F32, precision=lax.Precision.HIGHEST)
    c_all = cs + carry[...]
    c_ref[...] = c_all * 1.4426950408889634
    carry[...] = c_all[L - 1:L, :]

    qk_buf[CONV_TAIL:CONV_TAIL + L, :] = z_ref[:, Z_QK:Z_QK + 2 * W].astype(F32)
    qk = jnp.zeros((L, 2 * W), F32) + cb_ref[...]
    for j in range(MLSTM_CONV):
        off = CONV_TAIL - (MLSTM_CONV - 1) + j
        qk = qk + qk_buf[off:off + L, :] * cw_ref[j:j + 1, :]
    qk_buf[0:CONV_TAIL, :] = qk_buf[L:L + CONV_TAIL, :]
    qk = qk * _sigmoid(qk)
    q = qk[:, 0:W]
    k = qk[:, W:2 * W] * (HEAD_DIM ** -0.5)
    v = z_ref[:, Z_V:Z_V + W]
    q_b = q.astype(BF16)
    k_b = k.astype(BF16)

    cs_t = cs.T
    g_t = g.T
    m_prev = mstate[...]
    n_prev = nstate[...]
    c_prev = cstate[...]
    qn = q * n_prev
    q_c = jnp.dot(q_b, c_prev.astype(BF16), preferred_element_type=F32)

    num = jnp.zeros((L, W), F32)
    w_inter_l = jnp.zeros((L, W), F32)
    denom_l = jnp.ones((L, W), F32)
    wg_l = jnp.zeros((L, W), F32)
    decay_l = jnp.zeros((1, W), F32)
    m_new_row = m_prev
    for h in range(MLSTM_HEADS):
        hmask = (lane_w // HEAD_DIM) == h
        b_col = cs[:, G_F + h:G_F + h + 1]
        b_row = cs_t[G_F + h:G_F + h + 1, :]
        li_row = g_t[G_I + h:G_I + h + 1, :]
        li_col = g[:, G_I + h:G_I + h + 1]
        m_h = m_prev[:, h:h + 1]
        d_log = jnp.where(causal, b_col - b_row + li_row, -jnp.inf)
        inter = b_col + m_h
        m_t = jnp.maximum(jnp.max(d_log, axis=1, keepdims=True), inter)
        dmat = jnp.exp(d_log - m_t)
        q_h = jnp.where(hmask, q_b, jnp.zeros_like(q_b))
        s = lax.dot_general(q_h, k_b, (((1,), (1,)), ((), ())), preferred_element_type=F32) * dmat
        w_inter = jnp.exp(inter - m_t)
        pv = jnp.dot(s.astype(BF16), v, preferred_element_type=F32)
        num = jnp.where(hmask, pv, num)
        qn_h = jnp.sum(jnp.where(hmask, qn, 0.0), axis=1, keepdims=True)
        den = jnp.sum(s, axis=1, keepdims=True) + w_inter * qn_h
        dn = jnp.maximum(jnp.abs(den), jnp.exp(-m_t))
        w_inter_l = jnp.where(hmask, w_inter, w_inter_l)
        denom_l = jnp.where(hmask, dn, denom_l)
        b_tot = cs[L - 1:L, G_F + h:G_F + h + 1]
        g_h = b_tot - b_col + li_col
        m_new = jnp.maximum(b_tot + m_h, jnp.max(g_h, axis=0, keepdims=True))
        wg_l = jnp.where(hmask, jnp.exp(g_h - m_new), wg_l)
        decay_l = jnp.where(hmask, jnp.exp(b_tot + m_h - m_new), decay_l)
        m_new_row = jnp.where(lane_g == h, m_new, m_new_row)

    hout = (num + w_inter_l * q_c) / denom_l
    hsq = hout * hout
    rs_l = jnp.zeros((L, W), F32)
    for h in range(MLSTM_HEADS):
        hmask = (lane_w // HEAD_DIM) == h
        ms = jnp.sum(jnp.where(hmask, hsq, 0.0), axis=1, keepdims=True) * (1.0 / HEAD_DIM)
        rs_l = jnp.where(hmask, lax.rsqrt(ms + EPS), rs_l)
    o_gate = _sigmoid(z_ref[:, Z_O:Z_O + W].astype(F32))
    out_ref[:, 0:W] = (hout * rs_l * mnw_ref[...] * o_gate).astype(BF16)

    kw = k * wg_l
    upd = lax.dot_general(kw.astype(BF16), v, (((0,), (0,)), ((), ())), preferred_element_type=F32)
    rk = lax.broadcasted_iota(jnp.int32, (W, W), 0) // HEAD_DIM
    rv = lax.broadcasted_iota(jnp.int32, (W, W), 1) // HEAD_DIM
    cstate[...] = decay_l * c_prev + jnp.where(rk == rv, upd, 0.0)
    nstate[...] = decay_l * n_prev + jnp.sum(kw, axis=0, keepdims=True)
    mstate[...] = m_new_row

    a = z_ref[:, Z_GLU:Z_GLU + CONF_CHANNELS].astype(F32)
    gg = z_ref[:, Z_GLU + CONF_CHANNELS:Z_GLU + 2 * CONF_CHANNELS].astype(F32)
    u_buf[CONF_TAIL:CONF_TAIL + L, :] = a * _sigmoid(gg)
    span = L + CONF_TAIL - SUBLANES
    for r in range(1, SUBLANES):
        ush_buf[r - 1] = u_buf[r:r + span, :]
    hc = jnp.zeros((L, CONF_CHANNELS), F32) + ccb_ref[...]
    for j in range(CONF_KERNEL):
        off = CONF_TAIL - (CONF_KERNEL - 1) + j
        base, r = off - off % SUBLANES, off % SUBLANES
        src = u_buf[base:base + L, :] if r == 0 else ush_buf[r - 1, base:base + L, :]
        hc = hc + src * ccw_ref[j:j + 1, :]
    u_buf[0:CONF_TAIL, :] = u_buf[L:L + CONF_TAIL, :]
    mu = jnp.mean(hc, axis=1, keepdims=True)
    xc = hc - mu
    var = jnp.mean(xc * xc, axis=1, keepdims=True)
    y = xc * lax.rsqrt(var + EPS) * lnw_ref[...] + lnb_ref[...]
    out_ref[:, W:W + CONF_CHANNELS] = (y * _sigmoid(y)).astype(BF16)


def _seq_mix(z, gates, gate_bias, conv_w, conv_b, mnorm_w, cconv_w, cconv_b, ln_w, ln_b, batch):
    t = z.shape[0]
    s = t // batch
    L = min(SEQ_CHUNK, s)
    nc = s // L
    W = MLSTM_WIDTH
    full = lambda a: pl.BlockSpec(a.shape, lambda b, c: (0,) * a.ndim)
    return pl.pallas_call(
        _seq_kernel,
        grid=(batch, nc),
        in_specs=[
            pl.BlockSpec((L, Z_SEQ_WIDTH), lambda b, c: (b * nc + c, 0)),
            pl.BlockSpec((L, LANES), lambda b, c: (b * nc + c, 0)),
            full(gate_bias), full(conv_w), full(conv_b), full(mnorm_w),
            full(cconv_w), full(cconv_b), full(ln_w), full(ln_b),
        ],
        out_specs=[
            pl.BlockSpec((L, W + CONF_CHANNELS), lambda b, c: (b * nc + c, 0)),
            pl.BlockSpec((L, LANES), lambda b, c: (b * nc + c, 0)),
        ],
        out_shape=[jax.ShapeDtypeStruct((t, W + CONF_CHANNELS), BF16),
                   jax.ShapeDtypeStruct((t, LANES), F32)],
        scratch_shapes=[
            pltpu.VMEM((CONV_TAIL + L + CONV_TAIL, 2 * W), F32),
            pltpu.VMEM((CONF_TAIL + L + CONF_TAIL, CONF_CHANNELS), F32),
            pltpu.VMEM((SUBLANES - 1, L + CONF_TAIL - SUBLANES, CONF_CHANNELS), F32),
            pltpu.VMEM((W, W), F32),
            pltpu.VMEM((1, W), F32),
            pltpu.VMEM((1, LANES), F32),
            pltpu.VMEM((1, LANES), F32),
        ],
        compiler_params=_cparams(("arbitrary", "arbitrary")),
        name="seq_mix",
    )(z, gates, gate_bias, conv_w, conv_b, mnorm_w, cconv_w, cconv_b, ln_w, ln_b)


AUG_NEG_A = 0
AUG_ONE = 3
ST_ROWS = 8
ST_NORMS = 0
ST_FIRST = 1
ST_LAST = 2
FOX_SKIP_LOG2 = 160.0
NORM_SLACK = 1.01


def _split3(x):
    hi = x.astype(BF16).astype(F32)
    mid = (x - hi).astype(BF16).astype(F32)
    lo = (x - hi - mid).astype(BF16).astype(F32)
    return hi, mid, lo


def _fox_prep_kernel(zq_ref, zk_ref, zv_ref, c2_ref, ka_ref, qta_ref, vta_ref, st_ref):
    tp = zq_ref.shape[0]
    c2 = c2_ref[...]
    a = c2 - c2[0:1, :]
    c2_t = c2.T
    b_t = c2_t - c2_t[:, 0:1]
    lane = lax.broadcasted_iota(jnp.int32, (1, LANES), 1)
    row = lax.broadcasted_iota(jnp.int32, (LANES, 1), 0)
    q_scale = (HEAD_DIM ** -0.5) * 1.4426950408889634
    norms = jnp.zeros((1, LANES), F32)
    for p in range(FOX_HEADS // 2):
        sl = slice(p * LANES, (p + 1) * LANES)
        kp = zk_ref[:, sl].astype(F32)
        q_r = (zq_ref[:, sl].astype(F32) * q_scale).astype(BF16).astype(F32)
        q_t = q_r.T
        v_t = zv_ref[:, sl].astype(F32).T
        sq = jnp.concatenate([q_r * q_r, kp * kp], axis=1).astype(BF16)
        grp = lax.broadcasted_iota(jnp.int32, (2 * LANES, LANES), 0) // HEAD_DIM
        dst = jnp.where(grp < 2, 2 * p + grp, FOX_HEADS + 2 * p + grp - 2)
        sel = (lax.broadcasted_iota(jnp.int32, (2 * LANES, LANES), 1) == dst).astype(BF16)
        n2 = jnp.max(jnp.dot(sq, sel, preferred_element_type=F32), axis=0, keepdims=True)
        norms = jnp.maximum(norms, jnp.sqrt(n2) * NORM_SLACK)
        for hh in range(2):
            h = 2 * p + hh
            own_lo = hh * HEAD_DIM
            o = (1 - hh) * HEAD_DIM
            a_hi, a_mid, a_lo = _split3(a[:, G_FOX + h:G_FOX + h + 1])
            b_hi, b_mid, b_lo = _split3(b_t[G_FOX + h:G_FOX + h + 1, :])
            own_lane = (lane >= own_lo) & (lane < own_lo + HEAD_DIM)
            own_row = (row >= own_lo) & (row < own_lo + HEAD_DIM)
            ones_l = ((lane >= o + AUG_ONE) & (lane < o + AUG_ONE + 3)).astype(F32)
            ka = jnp.where(own_lane, kp, ones_l)
            ka = jnp.where(lane == o + AUG_NEG_A, -a_hi, ka)
            ka = jnp.where(lane == o + AUG_NEG_A + 1, -a_mid, ka)
            ka = jnp.where(lane == o + AUG_NEG_A + 2, -a_lo, ka)
            ones_r = ((row >= o + AUG_NEG_A) & (row < o + AUG_NEG_A + 3)).astype(F32)
            qa = jnp.where(own_row, q_t, ones_r)
            qa = jnp.where(row == o + AUG_ONE, b_hi, qa)
            qa = jnp.where(row == o + AUG_ONE + 1, b_mid, qa)
            qa = jnp.where(row == o + AUG_ONE + 2, b_lo, qa)
            va = jnp.where(own_row, v_t, (row == o).astype(F32))
            hs = slice(h * LANES, (h + 1) * LANES)
            ka_ref[:, hs] = ka.astype(BF16)
            qta_ref[hs, :] = qa.astype(BF16)
            vta_ref[hs, :] = va.astype(BF16)
    st_ref[ST_NORMS:ST_NORMS + 1, :] = norms
    st_ref[ST_FIRST:ST_FIRST + 1, :] = c2[0:1, :]
    st_ref[ST_LAST:ST_LAST + 1, :] = c2[tp - 1:tp, :]
    st_ref[ST_LAST + 1:, :] = jnp.zeros((ST_ROWS - ST_LAST - 1, LANES), F32)


def _fox_prep(z, c2):
    t = z.shape[0]
    tp = min(FOX_TQ, t)
    width = FOX_HEADS * LANES
    return pl.pallas_call(
        _fox_prep_kernel,
        grid=(t // tp,),
        in_specs=[
            pl.BlockSpec((tp, FOX_WIDTH), lambda i: (i, Z_FQ // FOX_WIDTH)),
            pl.BlockSpec((tp, FOX_WIDTH), lambda i: (i, Z_FK // FOX_WIDTH)),
            pl.BlockSpec((tp, FOX_WIDTH), lambda i: (i, Z_FV // FOX_WIDTH)),
            pl.BlockSpec((tp, LANES), lambda i: (i, 0)),
        ],
        out_specs=[
            pl.BlockSpec((tp, width), lambda i: (i, 0)),
            pl.BlockSpec((width, tp), lambda i: (0, i)),
            pl.BlockSpec((width, tp), lambda i: (0, i)),
            pl.BlockSpec((ST_ROWS, LANES), lambda i: (i, 0)),
        ],
        out_shape=[jax.ShapeDtypeStruct((t, width), BF16),
                   jax.ShapeDtypeStruct((width, t), BF16),
                   jax.ShapeDtypeStruct((width, t), BF16),
                   jax.ShapeDtypeStruct((t // tp * ST_ROWS, LANES), F32)],
        compiler_params=_cparams(("parallel",)),
        name="fox_prep",
    )(z, z, z, c2)


def _fox_tile_plan(stats, batch):
    st = stats.reshape(batch, -1, ST_ROWS, LANES)
    nq = st.shape[1]
    qmax = st[:, :, ST_NORMS, 0:FOX_HEADS]
    kmax = st[:, :, ST_NORMS, FOX_HEADS:2 * FOX_HEADS]
    first = st[:, :, ST_FIRST, G_FOX:G_FOX + FOX_HEADS]
    last = st[:, :, ST_LAST, G_FOX:G_FOX + FOX_HEADS]
    ub = (qmax[:, :, None, :] * (kmax[:, None, :, :] + kmax[:, :, None, :])
          + first[:, :, None, :] - last[:, None, :, :])
    ti = jnp.arange(nq)[:, None]
    tj = jnp.arange(nq)[None, :]
    need = (tj == ti) | ((tj < ti) & ~(ub < -FOX_SKIP_LOG2).transpose(0, 3, 1, 2))
    jmin = jnp.min(jnp.where(need, tj, nq), axis=-1)
    count = jnp.arange(nq)[None, None, :] - jmin + 1
    count = jnp.max(count.reshape(batch, FOX_HEADS // 2, 2, nq), axis=2)
    cref = first.transpose(0, 2, 1).reshape(-1)
    return count.reshape(-1).astype(jnp.int32), cref


def _fox_kernel(n_cast, cnt_ref, cref_ref, qta_ref, ka_ref, vta_ref, *rest):
    cast_in = rest[:n_cast]
    o_ref = rest[n_cast]
    cast_out = rest[n_cast + 1:2 * n_cast + 1]
    m_sc, acc_sc, s0_sc, cm0_sc, s1_sc, cm1_sc = rest[2 * n_cast + 1:]
    for src_ref, dst_ref in zip(cast_in, cast_out):
        dst_ref[...] = src_ref[...].astype(BF16)
    tq = qta_ref.shape[1]
    tk = tq
    b = pl.program_id(0)
    p = pl.program_id(1)
    i = pl.program_id(2)
    nq = pl.num_programs(2)
    for hh in range(2):
        m_sc[hh] = jnp.full((1, tq), -jnp.inf, F32)
        acc_sc[hh] = jnp.zeros((LANES, tq), F32)

    slots = ((s0_sc, cm0_sc), (s1_sc, cm1_sc))

    def scores(j, diagonal, slot):
        s_sc, cm_sc = slots[slot]
        start = pl.multiple_of(j * tk, tk)
        for hh in range(2):
            hs = slice(hh * LANES, (hh + 1) * LANES)
            s = jnp.dot(ka_ref[pl.ds(start, tk), hs], qta_ref[hs, :],
                        preferred_element_type=F32)
            if diagonal:
                kr = lax.broadcasted_iota(jnp.int32, (tk, tq), 0)
                qc = lax.broadcasted_iota(jnp.int32, (tk, tq), 1)
                s = jnp.where(kr <= qc, s, -jnp.inf)
            s_sc[hh] = s
            cm_sc[hh] = jnp.max(s, axis=0, keepdims=True)

    def softmax_pv(j, slot):
        s_sc, cm_sc = slots[slot]
        start = pl.multiple_of(j * tk, tk)
        for hh in range(2):
            hs = slice(hh * LANES, (hh + 1) * LANES)
            base = (b * FOX_HEADS + 2 * p + hh) * nq
            delta = cref_ref[base + i] - cref_ref[base + j]
            m_old = m_sc[hh]
            m_new = jnp.maximum(m_old, cm_sc[hh] + delta)
            alpha = jnp.exp2(m_old - m_new)
            pm = jnp.exp2(s_sc[hh] - (m_new - delta)).astype(BF16)
            acc_sc[hh] = alpha * acc_sc[hh] + jnp.dot(vta_ref[hs, pl.ds(start, tk)], pm,
                                                      preferred_element_type=F32)
            m_sc[hh] = m_new

    n_visits = cnt_ref[(b * pl.num_programs(1) + p) * nq + i]
    scores(i, True, 0)

    def body(t, carry):
        for parity in range(2):
            @pl.when(t % 2 == parity)
            def _():
                scores(i - t - 1, False, 1 - parity)
                softmax_pv(i - t, parity)
        return carry

    lax.fori_loop(0, n_visits - 1, body, 0)
    for parity in range(2):
        @pl.when((n_visits - 1) % 2 == parity)
        def _():
            softmax_pv(i - n_visits + 1, parity)
    acc0 = acc_sc[0]
    acc1 = acc_sc[1]
    row = lax.broadcasted_iota(jnp.int32, (LANES, 1), 0)
    o_t = jnp.where(row < HEAD_DIM, acc0 / acc0[HEAD_DIM:HEAD_DIM + 1, :], acc1 / acc1[0:1, :])
    o_ref[...] = o_t.T.astype(BF16)


def _row_block_cast_jobs(weights, n_steps):
    jobs = []
    for w in weights:
        lead = w.shape[:-2]
        n_lead = 1
        for v in lead:
            n_lead *= v
        rows, cols = w.shape[-2:]
        per_lead = max(n_steps // n_lead, 1)
        blk = rows
        for cand in range(2 * SUBLANES, rows + 1, 2 * SUBLANES):
            if rows % cand == 0 and rows // cand <= per_lead:
                blk = cand
                break
        nblk = rows // blk
        total = n_lead * nblk

        def index(step, nblk=nblk, total=total, lead=lead):
            k = jnp.minimum(step, total - 1)
            idx = [k % nblk, 0]
            k = k // nblk
            for v in reversed(lead):
                idx.insert(0, k % v)
                k = k // v
            return tuple(idx)

        jobs.append((w, (1,) * len(lead) + (blk, cols), index))
    return jobs


def _fox_attn(ka, qta, vta, visits, cref, batch, cast_weights=()):
    t = ka.shape[0]
    s = t // batch
    tq = min(FOX_TQ, s)
    nq = s // tq
    npairs = FOX_HEADS // 2
    jobs = _row_block_cast_jobs(cast_weights, batch * npairs * nq)
    step_of = lambda b, p, i: (b * npairs + p) * nq + i
    cast_specs = [pl.BlockSpec(blk, lambda b, p, i, nv, cr, f=f: f(step_of(b, p, i)))
                  for _, blk, f in jobs]
    outs = pl.pallas_call(
        functools.partial(_fox_kernel, len(jobs)),
        grid_spec=pltpu.PrefetchScalarGridSpec(
            num_scalar_prefetch=2,
            grid=(batch, npairs, nq),
            in_specs=[
                pl.BlockSpec((2 * LANES, tq), lambda b, p, i, nv, cr: (p, b * nq + i)),
                pl.BlockSpec((s, 2 * LANES), lambda b, p, i, nv, cr: (b, p)),
                pl.BlockSpec((2 * LANES, s), lambda b, p, i, nv, cr: (p, b)),
            ] + cast_specs,
            out_specs=[pl.BlockSpec((tq, LANES), lambda b, p, i, nv, cr: (b * nq + i, p))]
            + cast_specs,
            scratch_shapes=[
                pltpu.VMEM((2, 1, tq), F32),
                pltpu.VMEM((2, LANES, tq), F32),
                pltpu.VMEM((2, tq, tq), F32),
                pltpu.VMEM((2, 1, tq), F32),
                pltpu.VMEM((2, tq, tq), F32),
                pltpu.VMEM((2, 1, tq), F32),
            ],
        ),
        out_shape=[jax.ShapeDtypeStruct((t, FOX_WIDTH), BF16)]
        + [jax.ShapeDtypeStruct(w.shape, BF16) for w, _, _ in jobs],
        compiler_params=_cparams(("parallel", "parallel", "arbitrary")),
        name="fox_attn",
    )(visits, cref, qta, ka, vta, *[w for w, _, _ in jobs])
    return outs[0], outs[1:]


def _mem_kv_kernel(mem_ref, nw_ref, w_ref, o_ref):
    h = _rmsnorm(mem_ref[0], nw_ref[...]).astype(BF16)
    o_ref[0] = jnp.dot(h, w_ref[...], preferred_element_type=F32).astype(BF16)


def _mem_kv(mem, norm_w, w_kv):
    b, m, d = mem.shape
    return pl.pallas_call(
        _mem_kv_kernel,
        grid=(b,),
        in_specs=[
            pl.BlockSpec((1, m, d), lambda i: (i, 0, 0)),
            pl.BlockSpec((1, d), lambda i: (0, 0)),
            pl.BlockSpec(w_kv.shape, lambda i: (0, 0)),
        ],
        out_specs=pl.BlockSpec((1, m, w_kv.shape[1]), lambda i: (i, 0, 0)),
        out_shape=jax.ShapeDtypeStruct((b, m, w_kv.shape[1]), BF16),
        compiler_params=_cparams(("parallel",)),
        name="mem_kv",
    )(mem, norm_w, w_kv)


def _post_mix_kernel(with_router, hmc_ref, hf_ref, x_ref, wout_ref, nxw_ref, wq_ref, kv_ref, wo_ref,
                     nfw_ref, *rest):
    if with_router:
        rw_ref, x2_ref, h3_ref, route_ref = rest
    else:
        x2_ref, h3_ref = rest
    half = MLSTM_WIDTH + CONF_CHANNELS
    x1 = (x_ref[...]
          + jnp.dot(hmc_ref[...], wout_ref[0:half, :], preferred_element_type=F32)
          + jnp.dot(hf_ref[...], wout_ref[half:, :], preferred_element_type=F32))
    h2 = _rmsnorm(x1, nxw_ref[...]).astype(BF16)
    q = jnp.dot(h2, wq_ref[...], preferred_element_type=F32) * (XATTN_HEAD_DIM ** -0.5)
    q = q.astype(BF16)
    kv = kv_ref[0]
    outs = []
    for h in range(XATTN_HEADS):
        lo = h * XATTN_HEAD_DIM
        kh = kv[:, lo:lo + XATTN_HEAD_DIM]
        vh = kv[:, XATTN_WIDTH + lo:XATTN_WIDTH + lo + XATTN_HEAD_DIM]
        s = lax.dot_general(q[:, lo:lo + XATTN_HEAD_DIM], kh, (((1,), (1,)), ((), ())),
                            preferred_element_type=F32)
        s = s - jnp.max(s, axis=1, keepdims=True)
        e = jnp.exp(s)
        pm = e / jnp.sum(e, axis=1, keepdims=True)
        outs.append(jnp.dot(pm.astype(BF16), vh, preferred_element_type=F32).astype(BF16))
    o = jnp.concatenate(outs, axis=1)
    x2 = x1 + jnp.dot(o, wo_ref[...], preferred_element_type=F32)
    x2_ref[...] = x2
    h3 = _rmsnorm(x2, nfw_ref[...])
    h3_ref[...] = h3.astype(h3_ref.dtype)
    if with_router:
        h_hi = h3.astype(BF16)
        h_lo = (h3 - h_hi.astype(F32)).astype(BF16)
        rw = rw_ref[...]
        w_hi = rw.astype(BF16)
        w_lo = (rw - w_hi.astype(F32)).astype(BF16)
        nt_dims = (((1,), (1,)), ((), ()))
        on_hi = lax.dot_general(jnp.concatenate([w_hi, w_lo], axis=0), h_hi, nt_dims,
                                preferred_element_type=F32)
        on_lo = lax.dot_general(w_hi, h_lo, nt_dims, preferred_element_type=F32)
        logits_t = on_hi[0:ROUTER_ROWS] + on_hi[ROUTER_ROWS:] + on_lo
        le = logits_t[0:N_EXPERTS, :]
        eid = lax.broadcasted_iota(jnp.int32, (N_EXPERTS, 1), 0).astype(F32)
        m1 = jnp.max(le, axis=0, keepdims=True)
        i1 = jnp.min(jnp.where(le == m1, eid, float(N_EXPERTS)), axis=0, keepdims=True)
        rest_l = jnp.where(eid == i1, -jnp.inf, le)
        m2 = jnp.max(rest_l, axis=0, keepdims=True)
        i2 = jnp.min(jnp.where(rest_l == m2, eid, float(N_EXPERTS)), axis=0, keepdims=True)
        e2 = jnp.exp(m2 - m1)
        w1 = 1.0 / (1.0 + e2)
        w2 = e2 * w1
        r8 = jnp.where(eid == 0, i1, jnp.where(eid == 1, i2, jnp.where(eid == 2, w1,
                       jnp.where(eid == 3, w2, 0.0))))
        route_t = jnp.concatenate([r8, jnp.zeros((LANES - N_EXPERTS, r8.shape[1]), F32)], axis=0)
        route_ref[...] = route_t.T


def _post_mix(hmc, hf, x, w_out, nx_w, w_q, kv, w_o, nf_w, router_w, batch):
    t, d = x.shape
    tm = min(TM_PROJ, t // batch)
    per_b = (t // batch) // tm
    with_router = router_w is not None
    const = lambda a: pl.BlockSpec(a.shape, lambda i: (0,) * a.ndim)
    in_specs = [
        pl.BlockSpec((tm, hmc.shape[1]), lambda i: (i, 0)),
        pl.BlockSpec((tm, hf.shape[1]), lambda i: (i, 0)),
        pl.BlockSpec((tm, d), lambda i: (i, 0)),
        const(w_out), const(nx_w), const(w_q),
        pl.BlockSpec((1,) + kv.shape[1:], lambda i: (i // per_b, 0, 0)),
        const(w_o), const(nf_w),
    ]
    args = [hmc, hf, x, w_out, nx_w, w_q, kv, w_o, nf_w]
    out_specs = [pl.BlockSpec((tm, d), lambda i: (i, 0)), pl.BlockSpec((tm, d), lambda i: (i, 0))]
    out_shape = [jax.ShapeDtypeStruct((t, d), F32),
                 jax.ShapeDtypeStruct((t, d), F32 if with_router else BF16)]
    if with_router:
        in_specs.append(const(router_w))
        args.append(router_w)
        out_specs.append(pl.BlockSpec((tm, LANES), lambda i: (i, 0)))
        out_shape.append(jax.ShapeDtypeStruct((t, LANES), F32))
    return pl.pallas_call(
        functools.partial(_post_mix_kernel, with_router),
        grid=(t // tm,),
        in_specs=in_specs, out_specs=out_specs, out_shape=out_shape,
        compiler_params=_cparams(("parallel",)),
        name="post_mix_router" if with_router else "post_mix",
    )(*args)


def _ffn_kernel(with_residual, te_ref, nt_ref, x_ref, wg_ref, wu_ref, wd_ref, *rest):
    if with_residual:
        res_ref, y_ref = rest
    else:
        (y_ref,) = rest

    @pl.when(pl.program_id(0) < nt_ref[0])
    def _():
        xb = x_ref[...].astype(BF16)
        acc = res_ref[...] if with_residual else None
        lo = 0
        for fc in FF_CHUNKS:
            a = jnp.dot(xb, wg_ref[0, :, lo:lo + fc], preferred_element_type=F32)
            u = jnp.dot(xb, wu_ref[0, :, lo:lo + fc], preferred_element_type=F32)
            hcur = (a * _sigmoid(a) * u).astype(BF16)
            part = jnp.dot(hcur, wd_ref[0, lo:lo + fc, :], preferred_element_type=F32)
            acc = part if acc is None else acc + part
            lo += fc
        y_ref[...] = acc

    @pl.when(pl.program_id(0) >= nt_ref[0])
    def _():
        y_ref[...] = jnp.zeros_like(y_ref)


def _ffn(x, w_gate, w_up, w_down, tile_expert, n_tiles, residual=None):
    rows, d = x.shape
    tm = min(TM_FFN, rows)
    nt = rows // tm
    ff = w_gate.shape[2]
    assert sum(FF_CHUNKS) == ff
    with_residual = residual is not None

    def row_map(i, te, ntl):
        return (jnp.minimum(i, ntl[0] - 1), 0)

    def w_map(i, te, ntl):
        return (te[jnp.minimum(i, ntl[0] - 1)], 0, 0)

    in_specs = [
        pl.BlockSpec((tm, d), row_map),
        pl.BlockSpec((1, d, ff), w_map),
        pl.BlockSpec((1, d, ff), w_map),
        pl.BlockSpec((1, ff, d), w_map),
    ]
    args = [x, w_gate, w_up, w_down]
    if with_residual:
        in_specs.append(pl.BlockSpec((tm, d), row_map))
        args.append(residual)
    return pl.pallas_call(
        functools.partial(_ffn_kernel, with_residual),
        grid_spec=pltpu.PrefetchScalarGridSpec(
            num_scalar_prefetch=2,
            grid=(nt,),
            in_specs=in_specs,
            out_specs=pl.BlockSpec((tm, d), lambda i, te, ntl: (i, 0)),
        ),
        out_shape=jax.ShapeDtypeStruct((rows, d), F32),
        compiler_params=_cparams(("arbitrary",)),
        name="ffn_dense" if with_residual else "ffn_experts",
    )(tile_expert, n_tiles, *args)


def _plan_kernel(route_ref, rank_ref, cnt_ref, carry):
    tp = route_ref.shape[0]

    @pl.when(pl.program_id(0) == 0)
    def _():
        carry[...] = jnp.zeros_like(carry)

    lane_i = lax.broadcasted_iota(jnp.int32, (1, LANES), 1)
    lane = lane_i.astype(F32)
    route = route_ref[...]
    i1 = route[:, 0:1]
    i2 = route[:, 1:2]
    onehot = (lane == i1).astype(F32) + (lane == i2).astype(F32)
    row = lax.broadcasted_iota(jnp.int32, (tp, tp), 0)
    col = lax.broadcasted_iota(jnp.int32, (tp, tp), 1)
    strict = (row > col).astype(BF16)
    before = jnp.dot(strict, onehot.astype(BF16), preferred_element_type=F32) + carry[...]
    r1 = jnp.sum(jnp.where(lane == i1, before, 0.0), axis=1, keepdims=True)
    r2 = jnp.sum(jnp.where(lane == i2, before, 0.0), axis=1, keepdims=True)
    rank_ref[...] = jnp.where(lane_i == 0, r1, jnp.where(lane_i == 1, r2, 0.0))
    total = carry[...] + jnp.sum(onehot, axis=0, keepdims=True)
    carry[...] = total
    cnt_ref[...] = total


def _plan(route):
    t = route.shape[0]
    tp = min(512, t)
    return pl.pallas_call(
        _plan_kernel,
        grid=(t // tp,),
        in_specs=[pl.BlockSpec((tp, LANES), lambda i: (i, 0))],
        out_specs=[pl.BlockSpec((tp, LANES), lambda i: (i, 0)),
                   pl.BlockSpec((1, LANES), lambda i: (0, 0))],
        out_shape=[jax.ShapeDtypeStruct((t, LANES), F32), jax.ShapeDtypeStruct((1, LANES), F32)],
        scratch_shapes=[pltpu.VMEM((1, LANES), F32)],
        compiler_params=_cparams(("arbitrary",)),
        name="route_plan",
    )(route)


def _row_copy(src, src_row, dst, dst_row, sem):
    return pltpu.make_async_copy(src.at[pl.ds(src_row, 1), :], dst.at[pl.ds(dst_row, 1), :], sem)


def _wait_rows(src, dst, sem, n):
    def wait(r, carry):
        _row_copy(src, 0, dst, 0, sem).wait()
        return carry
    lax.fori_loop(0, n, wait, 0, unroll=DMA_UNROLL)


def _dispatch_kernel(zflag_ref, pos_ref, h_ref, xs_ref, zero_buf, stage, sems, zsem):
    i = pl.program_id(0)
    nt = pl.num_programs(0)
    td = h_ref.shape[0]
    tm = zero_buf.shape[0]

    @pl.when(i == 0)
    def _():
        zero_buf[...] = jnp.zeros_like(zero_buf)

        def zero_tile(tile, carry):
            @pl.when(zflag_ref[tile] != 0)
            def _():
                row0 = pl.multiple_of(tile * tm, tm)
                cp = pltpu.make_async_copy(zero_buf, xs_ref.at[pl.ds(row0, tm), :], zsem)
                cp.start()
                cp.wait()
            return carry

        lax.fori_loop(0, zflag_ref.shape[0], zero_tile, 0)

    def issue(slot):
        src = stage.at[slot]
        def start(r, carry):
            _row_copy(src, r, xs_ref, pos_ref[0, 0, 2 * r], sems.at[slot]).start()
            _row_copy(src, r, xs_ref, pos_ref[0, 0, 2 * r + 1], sems.at[slot]).start()
            return carry
        lax.fori_loop(0, td, start, 0, unroll=DMA_UNROLL)

    for parity in range(2):
        @pl.when(i % 2 == parity)
        def _():
            stage[parity] = h_ref[...]
            issue(parity)

            @pl.when(i > 0)
            def _():
                _wait_rows(stage.at[1 - parity], xs_ref, sems.at[1 - parity], 2 * td)

            @pl.when(i == nt - 1)
            def _():
                _wait_rows(stage.at[parity], xs_ref, sems.at[parity], 2 * td)


def _dispatch(h3, pos, zflag, rows, tm):
    t, d = h3.shape
    td = min(TD_ROUTE, t)
    nt = t // td
    pos3 = pos.reshape(nt, 1, 2 * td)
    return pl.pallas_call(
        _dispatch_kernel,
        grid_spec=pltpu.PrefetchScalarGridSpec(
            num_scalar_prefetch=1,
            grid=(nt,),
            in_specs=[
                pl.BlockSpec((1, 1, 2 * td), lambda i, zf: (i, 0, 0), memory_space=pltpu.SMEM),
                pl.BlockSpec((td, d), lambda i, zf: (i, 0)),
            ],
            out_specs=pl.BlockSpec(memory_space=pl.ANY),
            scratch_shapes=[pltpu.VMEM((tm, d), F32), pltpu.VMEM((2, td, d), F32),
                            pltpu.SemaphoreType.DMA((2,)), pltpu.SemaphoreType.DMA],
        ),
        out_shape=jax.ShapeDtypeStruct((rows, d), F32),
        compiler_params=_cparams(("arbitrary",)),
        name="dispatch",
    )(zflag, pos3, h3)


def _combine_kernel(pos_ref, pos_next_ref, x_ref, route_ref, nw_ref, ys_ref, o_ref, buf, sems):
    i = pl.program_id(0)
    nt = pl.num_programs(0)
    td = x_ref.shape[0]

    def issue(p_ref, slot):
        def start(r, carry):
            _row_copy(ys_ref, p_ref[0, 0, 2 * r], buf.at[slot, 0], r, sems.at[slot]).start()
            _row_copy(ys_ref, p_ref[0, 0, 2 * r + 1], buf.at[slot, 1], r, sems.at[slot]).start()
            return carry
        lax.fori_loop(0, td, start, 0, unroll=DMA_UNROLL)

    @pl.when(i == 0)
    def _():
        issue(pos_ref, 0)

    for parity in range(2):
        @pl.when(i % 2 == parity)
        def _():
            @pl.when(i + 1 < nt)
            def _():
                issue(pos_next_ref, 1 - parity)

            _wait_rows(ys_ref, buf.at[parity, 0], sems.at[parity], 2 * td)
            route = route_ref[...]
            x = x_ref[...] + route[:, 2:3] * buf[parity, 0] + route[:, 3:4] * buf[parity, 1]
            o_ref[...] = _rmsnorm(x, nw_ref[...])


def _combine(x2, route, pos, ys, norm_w):
    t, d = x2.shape
    td = min(TD_ROUTE, t)
    nt = t // td
    pos3 = pos.reshape(nt, 1, 2 * td)
    return pl.pallas_call(
        _combine_kernel,
        grid=(nt,),
        in_specs=[
            pl.BlockSpec((1, 1, 2 * td), lambda i: (i, 0, 0), memory_space=pltpu.SMEM),
            pl.BlockSpec((1, 1, 2 * td), lambda i: (jnp.minimum(i + 1, nt - 1), 0, 0),
                         memory_space=pltpu.SMEM),
            pl.BlockSpec((td, d), lambda i: (i, 0)),
            pl.BlockSpec((td, LANES), lambda i: (i, 0)),
            pl.BlockSpec((1, d), lambda i: (0, 0)),
            pl.BlockSpec(memory_space=pl.ANY),
        ],
        out_specs=pl.BlockSpec((td, d), lambda i: (i, 0)),
        out_shape=jax.ShapeDtypeStruct((t, d), F32),
        scratch_shapes=[pltpu.VMEM((2, 2, td, d), F32), pltpu.SemaphoreType.DMA((2,))],
        compiler_params=_cparams(("arbitrary",)),
        name="combine",
    )(pos3, pos3, x2, route, norm_w, ys)


def _split_w_in(w_in):
    mw, cw, fw = MLSTM_WIDTH, CONF_CHANNELS, FOX_WIDTH
    sizes = (2 * mw, mw, mw, MLSTM_HEADS, MLSTM_HEADS, 2 * cw, fw, fw, fw, FOX_HEADS)
    parts, off = [], 0
    for sz in sizes:
        parts.append(w_in[:, off:off + sz])
        off += sz
    m_qk, m_v, m_o, m_i, m_f, c_glu, f_q, f_k, f_v, f_f = parts
    w_main = jnp.concatenate([m_qk, m_v, m_o, c_glu, f_q, f_k, f_v], axis=1).astype(BF16)
    w_gate = jnp.concatenate([m_i, m_f, f_f], axis=1)
    w_gate = jnp.pad(w_gate, ((0, 0), (0, LANES - w_gate.shape[1]))).astype(BF16)
    return w_main, w_gate


def _gate_bias(b_i, b_f, fox_b):
    gb = jnp.concatenate([b_i, b_f, fox_b]).astype(F32)
    return jnp.pad(gb, (0, LANES - gb.shape[0])).reshape(1, LANES)


def _route_tables(route, rank, counts, tm, n_tiles_max):
    cnt = counts[0, :N_EXPERTS].astype(jnp.int32)
    tiles = (cnt + tm - 1) // tm
    tile_end = jnp.cumsum(tiles)
    row_off = (tile_end - tiles) * tm
    idx = route[:, 0:2].astype(jnp.int32)
    pos = row_off[idx] + rank[:, 0:2].astype(jnp.int32)
    tile_ids = jnp.arange(n_tiles_max, dtype=jnp.int32)
    tile_expert = jnp.sum(tile_ids[:, None] >= tile_end[None, :], axis=1).astype(jnp.int32)
    tile_expert = jnp.minimum(tile_expert, N_EXPERTS - 1)
    n_tiles = tile_end[-1:].astype(jnp.int32)
    is_last = jnp.any((tile_ids[:, None] == tile_end[None, :] - 1) & (tiles[None, :] > 0), axis=1)
    zero_flag = (is_last | (tile_ids >= n_tiles[0])).astype(jnp.int32)
    return pos.reshape(-1), tile_expert, n_tiles, zero_flag


def kernel(x, mem, norm_mix_w, w_in, mlstm_conv_w, mlstm_conv_b, mlstm_b_i, mlstm_b_f, mlstm_norm_w,
           conf_conv_w, conf_conv_b, conf_ln_w, conf_ln_b, fox_b_f, w_out, norm_xattn_w, norm_mem_w,
           xattn_w_q, xattn_w_kv, xattn_w_o, norm_ffn_w, ffn_w_gate, ffn_w_up, ffn_w_down, router_w,
           moe_w_gate, moe_w_up, moe_w_down, norm_final_w):
    batch, seq, d = x.shape
    depth = w_in.shape[0]
    t = batch * seq
    xf = x.reshape(t, d)
    row = lambda a: a.reshape(1, -1).astype(F32)
    out = None
    for l in range(depth):
        w_main, w_gate = _split_w_in(w_in[l])
        z, gates = _in_proj(xf, row(norm_mix_w[l]), w_main, w_gate)
        hmc, c2 = _seq_mix(
            z, gates, _gate_bias(mlstm_b_i[l], mlstm_b_f[l], fox_b_f[l]),
            mlstm_conv_w[l].astype(F32), row(mlstm_conv_b[l]), row(mlstm_norm_w[l]),
            conf_conv_w[l].astype(F32), row(conf_conv_b[l]), row(conf_ln_w[l]), row(conf_ln_b[l]),
            batch)
        ka, qta, vta, fox_stats = _fox_prep(z, c2)
        visits, cref = _fox_tile_plan(fox_stats, batch)
        dense = l % 2 == 0
        j = l // 2
        if dense:
            ffn_f32 = (ffn_w_gate[j][None], ffn_w_up[j][None], ffn_w_down[j][None])
        else:
            ffn_f32 = (moe_w_gate[j], moe_w_up[j], moe_w_down[j])
        hf, (wg_b, wu_b, wd_b) = _fox_attn(ka, qta, vta, visits, cref, batch, ffn_f32)
        kv = _mem_kv(mem, row(norm_mem_w[l]), xattn_w_kv[l].astype(BF16))
        rw = None
        if not dense:
            rw = jnp.pad(router_w[j].astype(F32).T, ((0, ROUTER_ROWS - N_EXPERTS), (0, 0)))
        res = _post_mix(hmc, hf, xf, w_out[l].astype(BF16), row(norm_xattn_w[l]),
                        xattn_w_q[l].astype(BF16), kv, xattn_w_o[l].astype(BF16),
                        row(norm_ffn_w[l]), rw, batch)
        if dense:
            x2, h3 = res
            nt = t // min(TM_FFN, t)
            xf = _ffn(h3, wg_b, wu_b, wd_b, jnp.zeros((nt,), jnp.int32),
                      jnp.full((1,), nt, jnp.int32), residual=x2)
            if l == depth - 1:
                raise NotImplementedError("final norm after a dense layer")
        else:
            x2, h3, route = res
            rank, counts = _plan(route)
            tm = min(TM_FFN, t)
            n_tiles_max = (2 * t) // tm + N_EXPERTS
            pos, tile_expert, n_tiles, zero_flag = _route_tables(route, rank, counts, tm, n_tiles_max)
            xs = _dispatch(h3, pos, zero_flag, n_tiles_max * tm, tm)
            ys = _ffn(xs, wg_b, wu_b, wd_b, tile_expert, n_tiles)
            assert l == depth - 1
            out = _combine(x2, route, pos, ys, row(norm_final_w))
    return out.reshape(batch, seq, d)
```

```python
import functools

import jax
import jax.numpy as jnp
from jax import lax
from jax.experimental import pallas as pl
from jax.experimental.pallas import tpu as pltpu

F32 = jnp.float32
BF16 = jnp.bfloat16
EPS = 1e-6

MLSTM_HEADS = 4
HEAD_DIM = 64
MLSTM_WIDTH = MLSTM_HEADS * HEAD_DIM
MLSTM_CONV = 4
CONF_CHANNELS = 256
CONF_KERNEL = 31
FOX_HEADS = 8
FOX_WIDTH = FOX_HEADS * HEAD_DIM
XATTN_HEADS = 4
XATTN_HEAD_DIM = 128
XATTN_WIDTH = XATTN_HEADS * XATTN_HEAD_DIM
N_EXPERTS = 8
ROUTER_ROWS = 16
LANES = 128
SUBLANES = 8

Z_QK = 0
Z_V = 512
Z_O = 768
Z_GLU = 1024
Z_SEQ_WIDTH = 1536
Z_FQ = 1536
Z_FK = 2048
Z_FV = 2560
Z_WIDTH = 3072
G_I = 0
G_F = 4
G_FOX = 8

TM_PROJ = 512
SEQ_CHUNK = 256
CONV_TAIL = 8
CONF_TAIL = 32
FOX_TQ = 512
TM_FFN = 512
FF_CHUNKS = (1024, 1024, 768)
TD_ROUTE = 256
DMA_UNROLL = 8
VMEM_LIMIT = 56 * 1024 * 1024


def _cparams(sem, vmem=VMEM_LIMIT):
    return pltpu.CompilerParams(dimension_semantics=sem, vmem_limit_bytes=vmem)


def _sigmoid(x):
    return 1.0 / (1.0 + jnp.exp(-x))


def _log_sigmoid(x):
    return jnp.minimum(x, 0.0) - jnp.log(1.0 + jnp.exp(-jnp.abs(x)))


def _rmsnorm(x, w):
    ms = jnp.mean(x * x, axis=-1, keepdims=True)
    return x * lax.rsqrt(ms + EPS) * w


def _in_proj_kernel(x_ref, nw_ref, w_ref, wg_ref, z_ref, g_ref):
    h = _rmsnorm(x_ref[...], nw_ref[...]).astype(BF16)
    for n in range(0, Z_WIDTH, 512):
        z_ref[:, n:n + 512] = jnp.dot(h, w_ref[:, n:n + 512],
                                      preferred_element_type=F32).astype(BF16)
    g_ref[...] = jnp.dot(h, wg_ref[...], preferred_element_type=F32)


def _in_proj(x, norm_w, w_main, w_gate):
    t, d = x.shape
    tm = min(TM_PROJ, t)
    return pl.pallas_call(
        _in_proj_kernel,
        grid=(t // tm,),
        in_specs=[
            pl.BlockSpec((tm, d), lambda i: (i, 0)),
            pl.BlockSpec((1, d), lambda i: (0, 0)),
            pl.BlockSpec((d, Z_WIDTH), lambda i: (0, 0)),
            pl.BlockSpec((d, LANES), lambda i: (0, 0)),
        ],
        out_specs=[
            pl.BlockSpec((tm, Z_WIDTH), lambda i: (i, 0)),
            pl.BlockSpec((tm, LANES), lambda i: (i, 0)),
        ],
        out_shape=[jax.ShapeDtypeStruct((t, Z_WIDTH), BF16),
                   jax.ShapeDtypeStruct((t, LANES), F32)],
        compiler_params=_cparams(("parallel",)),
        name="in_proj",
    )(x, norm_w, w_main, w_gate)


def _seq_kernel(z_ref, g_ref, gb_ref, cw_ref, cb_ref, mnw_ref, ccw_ref, ccb_ref, lnw_ref, lnb_ref,
                out_ref, c_ref,
                qk_buf, u_buf, ush_buf, cstate, nstate, mstate, carry):
    L = z_ref.shape[0]
    W = MLSTM_WIDTH

    @pl.when(pl.program_id(1) == 0)
    def _():
        qk_buf[0:CONV_TAIL, :] = jnp.zeros((CONV_TAIL, 2 * W), F32)
        u_buf[0:CONF_TAIL, :] = jnp.zeros((CONF_TAIL, CONF_CHANNELS), F32)
        cstate[...] = jnp.zeros_like(cstate)
        nstate[...] = jnp.zeros_like(nstate)
        mstate[...] = jnp.zeros_like(mstate)
        carry[...] = jnp.zeros_like(carry)

    lane_g = lax.broadcasted_iota(jnp.int32, (1, LANES), 1)
    lane_w = lax.broadcasted_iota(jnp.int32, (1, W), 1)

    g = g_ref[...] + gb_ref[...]
    is_forget = (lane_g >= G_F) & (lane_g < G_FOX + FOX_HEADS)
    lsg = jnp.where(is_forget, _log_sigmoid(g), 0.0)
    row = lax.broadcasted_iota(jnp.int32, (L, L), 0)
    col = lax.broadcasted_iota(jnp.int32, (L, L), 1)
    causal = row >= col
    tri = causal.astype(F32)
    cs = jnp.dot(tri, lsg, preferred_element_type=F32, precision=lax.Precision.HIGHEST)
    c_all = cs + carry[...]
    c_ref[...] = c_all * 1.4426950408889634
    carry[...] = c_all[L - 1:L, :]

    qk_buf[CONV_TAIL:CONV_TAIL + L, :] = z_ref[:, Z_QK:Z_QK + 2 * W].astype(F32)
    qk = jnp.zeros((L, 2 * W), F32) + cb_ref[...]
    for j in range(MLSTM_CONV):
        off = CONV_TAIL - (MLSTM_CONV - 1) + j
        qk = qk + qk_buf[off:off + L, :] * cw_ref[j:j + 1, :]
    qk_buf[0:CONV_TAIL, :] = qk_buf[L:L + CONV_TAIL, :]
    qk = qk * _sigmoid(qk)
    q = qk[:, 0:W]
    k = qk[:, W:2 * W] * (HEAD_DIM ** -0.5)
    v = z_ref[:, Z_V:Z_V + W]
    q_b = q.astype(BF16)
    k_b = k.astype(BF16)

    cs_t = cs.T
    g_t = g.T
    m_prev = mstate[...]
    n_prev = nstate[...]
    c_prev = cstate[...]
    qn = q * n_prev
    q_c = jnp.dot(q_b, c_prev.astype(BF16), preferred_element_type=F32)

    num = jnp.zeros((L, W), F32)
    w_inter_l = jnp.zeros((L, W), F32)
    denom_l = jnp.ones((L, W), F32)
    wg_l = jnp.zeros((L, W), F32)
    decay_l = jnp.zeros((1, W), F32)
    m_new_row = m_prev
    for h in range(MLSTM_HEADS):
        hmask = (lane_w // HEAD_DIM) == h
        b_col = cs[:, G_F + h:G_F + h + 1]
        b_row = cs_t[G_F + h:G_F + h + 1, :]
        li_row = g_t[G_I + h:G_I + h + 1, :]
        li_col = g[:, G_I + h:G_I + h + 1]
        m_h = m_prev[:, h:h + 1]
        d_log = jnp.where(causal, b_col - b_row + li_row, -jnp.inf)
        inter = b_col + m_h
        m_t = jnp.maximum(jnp.max(d_log, axis=1, keepdims=True), inter)
        dmat = jnp.exp(d_log - m_t)
        q_h = jnp.where(hmask, q_b, jnp.zeros_like(q_b))
        s = lax.dot_general(q_h, k_b, (((1,), (1,)), ((), ())), preferred_element_type=F32) * dmat
        w_inter = jnp.exp(inter - m_t)
        pv = jnp.dot(s.astype(BF16), v, preferred_element_type=F32)
        num = jnp.where(hmask, pv, num)
        qn_h = jnp.sum(jnp.where(hmask, qn, 0.0), axis=1, keepdims=True)
        den = jnp.sum(s, axis=1, keepdims=True) + w_inter * qn_h
        dn = jnp.maximum(jnp.abs(den), jnp.exp(-m_t))
        w_inter_l = jnp.where(hmask, w_inter, w_inter_l)
        denom_l = jnp.where(hmask, dn, denom_l)
        b_tot = cs[L - 1:L, G_F + h:G_F + h + 1]
        g_h = b_tot - b_col + li_col
        m_new = jnp.maximum(b_tot + m_h, jnp.max(g_h, axis=0, keepdims=True))
        wg_l = jnp.where(hmask, jnp.exp(g_h - m_new), wg_l)
        decay_l = jnp.where(hmask, jnp.exp(b_tot + m_h - m_new), decay_l)
        m_new_row = jnp.where(lane_g == h, m_new, m_new_row)

    hout = (num + w_inter_l * q_c) / denom_l
    hsq = hout * hout
    rs_l = jnp.zeros((L, W), F32)
    for h in range(MLSTM_HEADS):
        hmask = (lane_w // HEAD_DIM) == h
        ms = jnp.sum(jnp.where(hmask, hsq, 0.0), axis=1, keepdims=True) * (1.0 / HEAD_DIM)
        rs_l = jnp.where(hmask, lax.rsqrt(ms + EPS), rs_l)
    o_gate = _sigmoid(z_ref[:, Z_O:Z_O + W].astype(F32))
    out_ref[:, 0:W] = (hout * rs_l * mnw_ref[...] * o_gate).astype(BF16)

    kw = k * wg_l
    upd = lax.dot_general(kw.astype(BF16), v, (((0,), (0,)), ((), ())), preferred_element_type=F32)
    rk = lax.broadcasted_iota(jnp.int32, (W, W), 0) // HEAD_DIM
    rv = lax.broadcasted_iota(jnp.int32, (W, W), 1) // HEAD_DIM
    cstate[...] = decay_l * c_prev + jnp.where(rk == rv, upd, 0.0)
    nstate[...] = decay_l * n_prev + jnp.sum(kw, axis=0, keepdims=True)
    mstate[...] = m_new_row

    a = z_ref[:, Z_GLU:Z_GLU + CONF_CHANNELS].astype(F32)
    gg = z_ref[:, Z_GLU + CONF_CHANNELS:Z_GLU + 2 * CONF_CHANNELS].astype(F32)
    u_buf[CONF_TAIL:CONF_TAIL + L, :] = a * _sigmoid(gg)
    span = L + CONF_TAIL - SUBLANES
    for r in range(1, SUBLANES):
        ush_buf[r - 1] = u_buf[r:r + span, :]
    hc = jnp.zeros((L, CONF_CHANNELS), F32) + ccb_ref[...]
    for j in range(CONF_KERNEL):
        off = CONF_TAIL - (CONF_KERNEL - 1) + j
        base, r = off - off % SUBLANES, off % SUBLANES
        src = u_buf[base:base + L, :] if r == 0 else ush_buf[r - 1, base:base + L, :]
        hc = hc + src * ccw_ref[j:j + 1, :]
    u_buf[0:CONF_TAIL, :] = u_buf[L:L + CONF_TAIL, :]
    mu = jnp.mean(hc, axis=1, keepdims=True)
    xc = hc - mu
    var = jnp.mean(xc * xc, axis=1, keepdims=True)
    y = xc * lax.rsqrt(var + EPS) * lnw_ref[...] + lnb_ref[...]
    out_ref[:, W:W + CONF_CHANNELS] = (y * _sigmoid(y)).astype(BF16)


def _seq_mix(z, gates, gate_bias, conv_w, conv_b, mnorm_w, cconv_w, cconv_b, ln_w, ln_b, batch):
    t = z.shape[0]
    s = t // batch
    L = min(SEQ_CHUNK, s)
    nc = s // L
    W = MLSTM_WIDTH
    full = lambda a: pl.BlockSpec(a.shape, lambda b, c: (0,) * a.ndim)
    return pl.pallas_call(
        _seq_kernel,
        grid=(batch, nc),
        in_specs=[
            pl.BlockSpec((L, Z_SEQ_WIDTH), lambda b, c: (b * nc + c, 0)),
            pl.BlockSpec((L, LANES), lambda b, c: (b * nc + c, 0)),
            full(gate_bias), full(conv_w), full(conv_b), full(mnorm_w),
            full(cconv_w), full(cconv_b), full(ln_w), full(ln_b),
        ],
        out_specs=[
            pl.BlockSpec((L, W + CONF_CHANNELS), lambda b, c: (b * nc + c, 0)),
            pl.BlockSpec((L, LANES), lambda b, c: (b * nc + c, 0)),
        ],
        out_shape=[jax.ShapeDtypeStruct((t, W + CONF_CHANNELS), BF16),
                   jax.ShapeDtypeStruct((t, LANES), F32)],
        scratch_shapes=[
            pltpu.VMEM((CONV_TAIL + L + CONV_TAIL, 2 * W), F32),
            pltpu.VMEM((CONF_TAIL + L + CONF_TAIL, CONF_CHANNELS), F32),
            pltpu.VMEM((SUBLANES - 1, L + CONF_TAIL - SUBLANES, CONF_CHANNELS), F32),
            pltpu.VMEM((W, W), F32),
            pltpu.VMEM((1, W), F32),
            pltpu.VMEM((1, LANES), F32),
            pltpu.VMEM((1, LANES), F32),
        ],
        compiler_params=_cparams(("arbitrary", "arbitrary")),
        name="seq_mix",
    )(z, gates, gate_bias, conv_w, conv_b, mnorm_w, cconv_w, cconv_b, ln_w, ln_b)


AUG_NEG_A = 0
AUG_ONE = 3
ST_ROWS = 8
FOX_KSUB = 1
ST_NORMS = 0
ST_FIRST = 1
ST_LAST = 2
FOX_SKIP_LOG2 = 160.0
NORM_SLACK = 1.01


def _split3(x):
    hi = x.astype(BF16).astype(F32)
    mid = (x - hi).astype(BF16).astype(F32)
    lo = (x - hi - mid).astype(BF16).astype(F32)
    return hi, mid, lo


def _fox_prep_kernel(zq_ref, zk_ref, zv_ref, c2_ref, ka_ref, qta_ref, vta_ref, st_ref):
    tp = zq_ref.shape[0]
    c2 = c2_ref[...]
    a = c2 - c2[0:1, :]
    c2_t = c2.T
    b_t = c2_t - c2_t[:, 0:1]
    lane = lax.broadcasted_iota(jnp.int32, (1, LANES), 1)
    row = lax.broadcasted_iota(jnp.int32, (LANES, 1), 0)
    q_scale = (HEAD_DIM ** -0.5) * 1.4426950408889634
    norms = jnp.zeros((1, LANES), F32)
    for p in range(FOX_HEADS // 2):
        sl = slice(p * LANES, (p + 1) * LANES)
        kp = zk_ref[:, sl].astype(F32)
        q_r = (zq_ref[:, sl].astype(F32) * q_scale).astype(BF16).astype(F32)
        q_t = q_r.T
        v_t = zv_ref[:, sl].astype(F32).T
        sq = jnp.concatenate([q_r * q_r, kp * kp], axis=1).astype(BF16)
        grp = lax.broadcasted_iota(jnp.int32, (2 * LANES, LANES), 0) // HEAD_DIM
        tks = tp // FOX_KSUB
        for r in range(FOX_KSUB):
            dst = jnp.where(grp < 2, 2 * p + grp, (1 + r) * FOX_HEADS + 2 * p + grp - 2)
            sel = (lax.broadcasted_iota(jnp.int32, (2 * LANES, LANES), 1) == dst).astype(BF16)
            n2 = jnp.max(jnp.dot(sq[r * tks:(r + 1) * tks], sel, preferred_element_type=F32),
                         axis=0, keepdims=True)
            norms = jnp.maximum(norms, jnp.sqrt(n2) * NORM_SLACK)
        for hh in range(2):
            h = 2 * p + hh
            own_lo = hh * HEAD_DIM
            o = (1 - hh) * HEAD_DIM
            a_hi, a_mid, a_lo = _split3(a[:, G_FOX + h:G_FOX + h + 1])
            b_hi, b_mid, b_lo = _split3(b_t[G_FOX + h:G_FOX + h + 1, :])
            own_lane = (lane >= own_lo) & (lane < own_lo + HEAD_DIM)
            own_row = (row >= own_lo) & (row < own_lo + HEAD_DIM)
            ones_l = ((lane >= o + AUG_ONE) & (lane < o + AUG_ONE + 3)).astype(F32)
            ka = jnp.where(own_lane, kp, ones_l)
            ka = jnp.where(lane == o + AUG_NEG_A, -a_hi, ka)
            ka = jnp.where(lane == o + AUG_NEG_A + 1, -a_mid, ka)
            ka = jnp.where(lane == o + AUG_NEG_A + 2, -a_lo, ka)
            ones_r = ((row >= o + AUG_NEG_A) & (row < o + AUG_NEG_A + 3)).astype(F32)
            qa = jnp.where(own_row, q_t, ones_r)
            qa = jnp.where(row == o + AUG_ONE, b_hi, qa)
            qa = jnp.where(row == o + AUG_ONE + 1, b_mid, qa)
            qa = jnp.where(row == o + AUG_ONE + 2, b_lo, qa)
            va = jnp.where(own_row, v_t, (row == o).astype(F32))
            hs = slice(h * LANES, (h + 1) * LANES)
            ka_ref[:, hs] = ka.astype(BF16)
            qta_ref[hs, :] = qa.astype(BF16)
            vta_ref[hs, :] = va.astype(BF16)
    st_ref[ST_NORMS:ST_NORMS + 1, :] = norms
    st_ref[ST_FIRST:ST_FIRST + 1, :] = c2[0:1, :]
    tks = tp // FOX_KSUB
    for r in range(FOX_KSUB):
        st_ref[ST_LAST + r:ST_LAST + r + 1, :] = c2[(r + 1) * tks - 1:(r + 1) * tks, :]
    st_ref[ST_LAST + FOX_KSUB:, :] = jnp.zeros((ST_ROWS - ST_LAST - FOX_KSUB, LANES), F32)


def _fox_prep(z, c2):
    t = z.shape[0]
    tp = min(FOX_TQ, t)
    width = FOX_HEADS * LANES
    return pl.pallas_call(
        _fox_prep_kernel,
        grid=(t // tp,),
        in_specs=[
            pl.BlockSpec((tp, FOX_WIDTH), lambda i: (i, Z_FQ // FOX_WIDTH)),
            pl.BlockSpec((tp, FOX_WIDTH), lambda i: (i, Z_FK // FOX_WIDTH)),
            pl.BlockSpec((tp, FOX_WIDTH), lambda i: (i, Z_FV // FOX_WIDTH)),
            pl.BlockSpec((tp, LANES), lambda i: (i, 0)),
        ],
        out_specs=[
            pl.BlockSpec((tp, width), lambda i: (i, 0)),
            pl.BlockSpec((width, tp), lambda i: (0, i)),
            pl.BlockSpec((width, tp), lambda i: (0, i)),
            pl.BlockSpec((ST_ROWS, LANES), lambda i: (i, 0)),
        ],
        out_shape=[jax.ShapeDtypeStruct((t, width), BF16),
                   jax.ShapeDtypeStruct((width, t), BF16),
                   jax.ShapeDtypeStruct((width, t), BF16),
                   jax.ShapeDtypeStruct((t // tp * ST_ROWS, LANES), F32)],
        compiler_params=_cparams(("parallel",)),
        name="fox_prep",
    )(z, z, z, c2)


def _fox_tile_plan(stats, batch):
    st = stats.reshape(batch, -1, ST_ROWS, LANES)
    nq = st.shape[1]
    nk = nq * FOX_KSUB
    qmax = st[:, :, ST_NORMS, 0:FOX_HEADS]
    kmax = st[:, :, ST_NORMS, FOX_HEADS:(1 + FOX_KSUB) * FOX_HEADS]
    kmax = kmax.reshape(batch, nk, FOX_HEADS)
    kdiag = jnp.max(kmax.reshape(batch, nq, FOX_KSUB, FOX_HEADS), axis=2)
    first = st[:, :, ST_FIRST, G_FOX:G_FOX + FOX_HEADS]
    last = st[:, :, ST_LAST:ST_LAST + FOX_KSUB, G_FOX:G_FOX + FOX_HEADS].reshape(batch, nk, FOX_HEADS)
    ub = (qmax[:, :, None, :] * (kmax[:, None, :, :] + kdiag[:, :, None, :])
          + first[:, :, None, :] - last[:, None, :, :])
    ti = jnp.arange(nq)[:, None]
    tj = jnp.arange(nk)[None, :]
    diag_lo = FOX_KSUB * ti
    need = (tj < diag_lo) & ~(ub < -FOX_SKIP_LOG2).transpose(0, 3, 1, 2)
    jmin = jnp.min(jnp.where(need, tj, diag_lo), axis=-1)
    count = FOX_KSUB * (jnp.arange(nq)[None, None, :] + 1) - jmin
    count = jnp.max(count.reshape(batch, FOX_HEADS // 2, 2, nq), axis=2)
    cref = first.transpose(0, 2, 1).reshape(-1)
    return count.reshape(-1).astype(jnp.int32), cref


def _fox_kernel(n_cast, cnt_ref, cref_ref, qta_ref, ka_ref, vta_ref, *rest):
    cast_in = rest[:n_cast]
    o_ref = rest[n_cast]
    cast_out = rest[n_cast + 1:2 * n_cast + 1]
    m_sc, acc_sc, s0_sc, cm0_sc, s1_sc, cm1_sc = rest[2 * n_cast + 1:]
    for src_ref, dst_ref in zip(cast_in, cast_out):
        dst_ref[...] = src_ref[...].astype(BF16)
    tq = qta_ref.shape[1]
    tk = tq // FOX_KSUB
    b = pl.program_id(0)
    p = pl.program_id(1)
    i = pl.program_id(2)
    nq = pl.num_programs(2)
    for hh in range(2):
        m_sc[hh] = jnp.full((1, tq), -jnp.inf, F32)
        acc_sc[hh] = jnp.zeros((LANES, tq), F32)

    slots = ((s0_sc, cm0_sc), (s1_sc, cm1_sc))
    diag0 = FOX_KSUB * i

    def sub_tile(v):
        return jnp.where(v < FOX_KSUB, diag0 + v, diag0 + FOX_KSUB - 1 - v)

    def scores(v, diag_sub, slot):
        s_sc, cm_sc = slots[slot]
        start = pl.multiple_of(sub_tile(v) * tk, tk)
        for hh in range(2):
            hs = slice(hh * LANES, (hh + 1) * LANES)
            s = jnp.dot(ka_ref[pl.ds(start, tk), hs], qta_ref[hs, :],
                        preferred_element_type=F32)
            if diag_sub is not None:
                kr = lax.broadcasted_iota(jnp.int32, (tk, tq), 0) + diag_sub * tk
                qc = lax.broadcasted_iota(jnp.int32, (tk, tq), 1)
                s = jnp.where(kr <= qc, s, -jnp.inf)
            s_sc[hh] = s
            cm_sc[hh] = jnp.max(s, axis=0, keepdims=True)

    def softmax_pv(v, slot):
        s_sc, cm_sc = slots[slot]
        j = sub_tile(v)
        start = pl.multiple_of(j * tk, tk)
        for hh in range(2):
            hs = slice(hh * LANES, (hh + 1) * LANES)
            base = (b * FOX_HEADS + 2 * p + hh) * nq
            delta = cref_ref[base + i] - cref_ref[base + j // FOX_KSUB]
            m_old = m_sc[hh]
            m_new = jnp.maximum(m_old, cm_sc[hh] + delta)
            alpha = jnp.exp2(m_old - m_new)
            pm = jnp.exp2(s_sc[hh] - (m_new - delta)).astype(BF16)
            acc_sc[hh] = alpha * acc_sc[hh] + jnp.dot(vta_ref[hs, pl.ds(start, tk)], pm,
                                                      preferred_element_type=F32)
            m_sc[hh] = m_new

    n_visits = cnt_ref[(b * pl.num_programs(1) + p) * nq + i]
    scores(0, 0, 0)
    for v in range(FOX_KSUB - 1):
        scores(v + 1, v + 1, (v + 1) % 2)
        softmax_pv(v, v % 2)

    def body(v, carry):
        for parity in range(2):
            @pl.when(v % 2 == parity)
            def _():
                scores(v + 1, None, 1 - parity)
                softmax_pv(v, parity)
        return carry

    lax.fori_loop(FOX_KSUB - 1, n_visits - 1, body, 0)
    for parity in range(2):
        @pl.when((n_visits - 1) % 2 == parity)
        def _():
            softmax_pv(n_visits - 1, parity)
    acc0 = acc_sc[0]
    acc1 = acc_sc[1]
    row = lax.broadcasted_iota(jnp.int32, (LANES, 1), 0)
    o_t = jnp.where(row < HEAD_DIM, acc0 / acc0[HEAD_DIM:HEAD_DIM + 1, :], acc1 / acc1[0:1, :])
    o_ref[...] = o_t.T.astype(BF16)


def _row_block_cast_jobs(weights, n_steps):
    jobs = []
    for w in weights:
        lead = w.shape[:-2]
        n_lead = 1
        for v in lead:
            n_lead *= v
        rows, cols = w.shape[-2:]
        per_lead = max(n_steps // n_lead, 1)
        blk = rows
        for cand in range(2 * SUBLANES, rows + 1, 2 * SUBLANES):
            if rows % cand == 0 and rows // cand <= per_lead:
                blk = cand
                break
        nblk = rows // blk
        total = n_lead * nblk

        def index(step, nblk=nblk, total=total, lead=lead):
            k = jnp.minimum(step, total - 1)
            idx = [k % nblk, 0]
            k = k // nblk
            for v in reversed(lead):
                idx.insert(0, k % v)
                k = k // v
            return tuple(idx)

        jobs.append((w, (1,) * len(lead) + (blk, cols), index))
    return jobs


def _fox_attn(ka, qta, vta, visits, cref, batch, cast_weights=()):
    t = ka.shape[0]
    s = t // batch
    tq = min(FOX_TQ, s)
    nq = s // tq
    npairs = FOX_HEADS // 2
    jobs = _row_block_cast_jobs(cast_weights, batch * npairs * nq)
    step_of = lambda b, p, i: (b * npairs + p) * nq + i
    cast_specs = [pl.BlockSpec(blk, lambda b, p, i, nv, cr, f=f: f(step_of(b, p, i)))
                  for _, blk, f in jobs]
    outs = pl.pallas_call(
        functools.partial(_fox_kernel, len(jobs)),
        grid_spec=pltpu.PrefetchScalarGridSpec(
            num_scalar_prefetch=2,
            grid=(batch, npairs, nq),
            in_specs=[
                pl.BlockSpec((2 * LANES, tq), lambda b, p, i, nv, cr: (p, b * nq + i)),
                pl.BlockSpec((s, 2 * LANES), lambda b, p, i, nv, cr: (b, p)),
                pl.BlockSpec((2 * LANES, s), lambda b, p, i, nv, cr: (p, b)),
            ] + cast_specs,
            out_specs=[pl.BlockSpec((tq, LANES), lambda b, p, i, nv, cr: (b * nq + i, p))]
            + cast_specs,
            scratch_shapes=[
                pltpu.VMEM((2, 1, tq), F32),
                pltpu.VMEM((2, LANES, tq), F32),
                pltpu.VMEM((2, tq // FOX_KSUB, tq), F32),
                pltpu.VMEM((2, 1, tq), F32),
                pltpu.VMEM((2, tq // FOX_KSUB, tq), F32),
                pltpu.VMEM((2, 1, tq), F32),
            ],
        ),
        out_shape=[jax.ShapeDtypeStruct((t, FOX_WIDTH), BF16)]
        + [jax.ShapeDtypeStruct(w.shape, BF16) for w, _, _ in jobs],
        compiler_params=_cparams(("arbitrary", "arbitrary", "arbitrary")),
        name="fox_attn",
    )(visits, cref, qta, ka, vta, *[w for w, _, _ in jobs])
    return outs[0], outs[1:]


def _mem_kv_kernel(mem_ref, nw_ref, w_ref, o_ref):
    h = _rmsnorm(mem_ref[0], nw_ref[...]).astype(BF16)
    o_ref[0] = jnp.dot(h, w_ref[...], preferred_element_type=F32).astype(BF16)


def _mem_kv(mem, norm_w, w_kv):
    b, m, d = mem.shape
    return pl.pallas_call(
        _mem_kv_kernel,
        grid=(b,),
        in_specs=[
            pl.BlockSpec((1, m, d), lambda i: (i, 0, 0)),
            pl.BlockSpec((1, d), lambda i: (0, 0)),
            pl.BlockSpec(w_kv.shape, lambda i: (0, 0)),
        ],
        out_specs=pl.BlockSpec((1, m, w_kv.shape[1]), lambda i: (i, 0, 0)),
        out_shape=jax.ShapeDtypeStruct((b, m, w_kv.shape[1]), BF16),
        compiler_params=_cparams(("parallel",)),
        name="mem_kv",
    )(mem, norm_w, w_kv)


def _post_mix_kernel(with_router, hmc_ref, hf_ref, x_ref, wout_ref, nxw_ref, wq_ref, kv_ref, wo_ref,
                     nfw_ref, *rest):
    if with_router:
        rw_ref, x2_ref, h3_ref, route_ref = rest
    else:
        x2_ref, h3_ref = rest
    half = MLSTM_WIDTH + CONF_CHANNELS
    x1 = (x_ref[...]
          + jnp.dot(hmc_ref[...], wout_ref[0:half, :], preferred_element_type=F32)
          + jnp.dot(hf_ref[...], wout_ref[half:, :], preferred_element_type=F32))
    h2 = _rmsnorm(x1, nxw_ref[...]).astype(BF16)
    q = jnp.dot(h2, wq_ref[...], preferred_element_type=F32) * (XATTN_HEAD_DIM ** -0.5)
    q = q.astype(BF16)
    kv = kv_ref[0]
    outs = []
    for h in range(XATTN_HEADS):
        lo = h * XATTN_HEAD_DIM
        kh = kv[:, lo:lo + XATTN_HEAD_DIM]
        vh = kv[:, XATTN_WIDTH + lo:XATTN_WIDTH + lo + XATTN_HEAD_DIM]
        s = lax.dot_general(q[:, lo:lo + XATTN_HEAD_DIM], kh, (((1,), (1,)), ((), ())),
                            preferred_element_type=F32)
        s = s - jnp.max(s, axis=1, keepdims=True)
        e = jnp.exp(s)
        pm = e / jnp.sum(e, axis=1, keepdims=True)
        outs.append(jnp.dot(pm.astype(BF16), vh, preferred_element_type=F32).astype(BF16))
    o = jnp.concatenate(outs, axis=1)
    x2 = x1 + jnp.dot(o, wo_ref[...], preferred_element_type=F32)
    x2_ref[...] = x2
    h3 = _rmsnorm(x2, nfw_ref[...])
    h3_ref[...] = h3.astype(h3_ref.dtype)
    if with_router:
        h_hi = h3.astype(BF16)
        h_lo = (h3 - h_hi.astype(F32)).astype(BF16)
        rw = rw_ref[...]
        w_hi = rw.astype(BF16)
        w_lo = (rw - w_hi.astype(F32)).astype(BF16)
        nt_dims = (((1,), (1,)), ((), ()))
        on_hi = lax.dot_general(jnp.concatenate([w_hi, w_lo], axis=0), h_hi, nt_dims,
                                preferred_element_type=F32)
        on_lo = lax.dot_general(w_hi, h_lo, nt_dims, preferred_element_type=F32)
        logits_t = on_hi[0:ROUTER_ROWS] + on_hi[ROUTER_ROWS:] + on_lo
        le = logits_t[0:N_EXPERTS, :]
        eid = lax.broadcasted_iota(jnp.int32, (N_EXPERTS, 1), 0).astype(F32)
        m1 = jnp.max(le, axis=0, keepdims=True)
        i1 = jnp.min(jnp.where(le == m1, eid, float(N_EXPERTS)), axis=0, keepdims=True)
        rest_l = jnp.where(eid == i1, -jnp.inf, le)
        m2 = jnp.max(rest_l, axis=0, keepdims=True)
        i2 = jnp.min(jnp.where(rest_l == m2, eid, float(N_EXPERTS)), axis=0, keepdims=True)
        e2 = jnp.exp(m2 - m1)
        w1 = 1.0 / (1.0 + e2)
        w2 = e2 * w1
        r8 = jnp.where(eid == 0, i1, jnp.where(eid == 1, i2, jnp.where(eid == 2, w1,
                       jnp.where(eid == 3, w2, 0.0))))
        route_t = jnp.concatenate([r8, jnp.zeros((LANES - N_EXPERTS, r8.shape[1]), F32)], axis=0)
        route_ref[...] = route_t.T


def _post_mix(hmc, hf, x, w_out, nx_w, w_q, kv, w_o, nf_w, router_w, batch):
    t, d = x.shape
    tm = min(TM_PROJ, t // batch)
    per_b = (t // batch) // tm
    with_router = router_w is not None
    const = lambda a: pl.BlockSpec(a.shape, lambda i: (0,) * a.ndim)
    in_specs = [
        pl.BlockSpec((tm, hmc.shape[1]), lambda i: (i, 0)),
        pl.BlockSpec((tm, hf.shape[1]), lambda i: (i, 0)),
        pl.BlockSpec((tm, d), lambda i: (i, 0)),
        const(w_out), const(nx_w), const(w_q),
        pl.BlockSpec((1,) + kv.shape[1:], lambda i: (i // per_b, 0, 0)),
        const(w_o), const(nf_w),
    ]
    args = [hmc, hf, x, w_out, nx_w, w_q, kv, w_o, nf_w]
    out_specs = [pl.BlockSpec((tm, d), lambda i: (i, 0)), pl.BlockSpec((tm, d), lambda i: (i, 0))]
    out_shape = [jax.ShapeDtypeStruct((t, d), F32),
                 jax.ShapeDtypeStruct((t, d), F32 if with_router else BF16)]
    if with_router:
        in_specs.append(const(router_w))
        args.append(router_w)
        out_specs.append(pl.BlockSpec((tm, LANES), lambda i: (i, 0)))
        out_shape.append(jax.ShapeDtypeStruct((t, LANES), F32))
    return pl.pallas_call(
        functools.partial(_post_mix_kernel, with_router),
        grid=(t // tm,),
        in_specs=in_specs, out_specs=out_specs, out_shape=out_shape,
        compiler_params=_cparams(("parallel",)),
        name="post_mix_router" if with_router else "post_mix",
    )(*args)


def _ffn_kernel(with_residual, te_ref, nt_ref, x_ref, wg_ref, wu_ref, wd_ref, *rest):
    if with_residual:
        res_ref, y_ref = rest
    else:
        (y_ref,) = rest

    @pl.when(pl.program_id(0) < nt_ref[0])
    def _():
        xb = x_ref[...].astype(BF16)
        acc = res_ref[...] if with_residual else None
        lo = 0
        for fc in FF_CHUNKS:
            a = jnp.dot(xb, wg_ref[0, :, lo:lo + fc], preferred_element_type=F32)
            u = jnp.dot(xb, wu_ref[0, :, lo:lo + fc], preferred_element_type=F32)
            hcur = (a * _sigmoid(a) * u).astype(BF16)
            part = jnp.dot(hcur, wd_ref[0, lo:lo + fc, :], preferred_element_type=F32)
            acc = part if acc is None else acc + part
            lo += fc
        y_ref[...] = acc

    @pl.when(pl.program_id(0) >= nt_ref[0])
    def _():
        y_ref[...] = jnp.zeros_like(y_ref)


def _ffn(x, w_gate, w_up, w_down, tile_expert, n_tiles, residual=None):
    rows, d = x.shape
    tm = min(TM_FFN, rows)
    nt = rows // tm
    ff = w_gate.shape[2]
    assert sum(FF_CHUNKS) == ff
    with_residual = residual is not None

    def row_map(i, te, ntl):
        return (jnp.minimum(i, ntl[0] - 1), 0)

    def w_map(i, te, ntl):
        return (te[jnp.minimum(i, ntl[0] - 1)], 0, 0)

    in_specs = [
        pl.BlockSpec((tm, d), row_map),
        pl.BlockSpec((1, d, ff), w_map),
        pl.BlockSpec((1, d, ff), w_map),
        pl.BlockSpec((1, ff, d), w_map),
    ]
    args = [x, w_gate, w_up, w_down]
    if with_residual:
        in_specs.append(pl.BlockSpec((tm, d), row_map))
        args.append(residual)
    return pl.pallas_call(
        functools.partial(_ffn_kernel, with_residual),
        grid_spec=pltpu.PrefetchScalarGridSpec(
            num_scalar_prefetch=2,
            grid=(nt,),
            in_specs=in_specs,
            out_specs=pl.BlockSpec((tm, d), lambda i, te, ntl: (i, 0)),
        ),
        out_shape=jax.ShapeDtypeStruct((rows, d), F32),
        compiler_params=_cparams(("arbitrary",)),
        name="ffn_dense" if with_residual else "ffn_experts",
    )(tile_expert, n_tiles, *args)


def _plan_kernel(route_ref, rank_ref, cnt_ref, carry):
    tp = route_ref.shape[0]

    @pl.when(pl.program_id(0) == 0)
    def _():
        carry[...] = jnp.zeros_like(carry)

    lane_i = lax.broadcasted_iota(jnp.int32, (1, LANES), 1)
    lane = lane_i.astype(F32)
    route = route_ref[...]
    i1 = route[:, 0:1]
    i2 = route[:, 1:2]
    onehot = (lane == i1).astype(F32) + (lane == i2).astype(F32)
    row = lax.broadcasted_iota(jnp.int32, (tp, tp), 0)
    col = lax.broadcasted_iota(jnp.int32, (tp, tp), 1)
    strict = (row > col).astype(BF16)
    before = jnp.dot(strict, onehot.astype(BF16), preferred_element_type=F32) + carry[...]
    r1 = jnp.sum(jnp.where(lane == i1, before, 0.0), axis=1, keepdims=True)
    r2 = jnp.sum(jnp.where(lane == i2, before, 0.0), axis=1, keepdims=True)
    rank_ref[...] = jnp.where(lane_i == 0, r1, jnp.where(lane_i == 1, r2, 0.0))
    total = carry[...] + jnp.sum(onehot, axis=0, keepdims=True)
    carry[...] = total
    cnt_ref[...] = total


def _plan(route):
    t = route.shape[0]
    tp = min(512, t)
    return pl.pallas_call(
        _plan_kernel,
        grid=(t // tp,),
        in_specs=[pl.BlockSpec((tp, LANES), lambda i: (i, 0))],
        out_specs=[pl.BlockSpec((tp, LANES), lambda i: (i, 0)),
                   pl.BlockSpec((1, LANES), lambda i: (0, 0))],
        out_shape=[jax.ShapeDtypeStruct((t, LANES), F32), jax.ShapeDtypeStruct((1, LANES), F32)],
        scratch_shapes=[pltpu.VMEM((1, LANES), F32)],
        compiler_params=_cparams(("arbitrary",)),
        name="route_plan",
    )(route)


def _row_copy(src, src_row, dst, dst_row, sem):
    return pltpu.make_async_copy(src.at[pl.ds(src_row, 1), :], dst.at[pl.ds(dst_row, 1), :], sem)


def _wait_rows(src, dst, sem, n):
    def wait(r, carry):
        _row_copy(src, 0, dst, 0, sem).wait()
        return carry
    lax.fori_loop(0, n, wait, 0, unroll=DMA_UNROLL)


def _dispatch_kernel(zflag_ref, pos_ref, h_ref, xs_ref, zero_buf, stage, sems, zsem):
    i = pl.program_id(0)
    nt = pl.num_programs(0)
    td = h_ref.shape[0]
    tm = zero_buf.shape[0]

    @pl.when(i == 0)
    def _():
        zero_buf[...] = jnp.zeros_like(zero_buf)

        def zero_tile(tile, carry):
            @pl.when(zflag_ref[tile] != 0)
            def _():
                row0 = pl.multiple_of(tile * tm, tm)
                cp = pltpu.make_async_copy(zero_buf, xs_ref.at[pl.ds(row0, tm), :], zsem)
                cp.start()
                cp.wait()
            return carry

        lax.fori_loop(0, zflag_ref.shape[0], zero_tile, 0)

    def issue(slot):
        src = stage.at[slot]
        def start(r, carry):
            _row_copy(src, r, xs_ref, pos_ref[0, 0, 2 * r], sems.at[slot]).start(priority=0)
            _row_copy(src, r, xs_ref, pos_ref[0, 0, 2 * r + 1], sems.at[slot]).start(priority=1)
            return carry
        lax.fori_loop(0, td, start, 0, unroll=DMA_UNROLL)

    for parity in range(2):
        @pl.when(i % 2 == parity)
        def _():
            stage[parity] = h_ref[...]
            issue(parity)

            @pl.when(i > 0)
            def _():
                _wait_rows(stage.at[1 - parity], xs_ref, sems.at[1 - parity], 2 * td)

            @pl.when(i == nt - 1)
            def _():
                _wait_rows(stage.at[parity], xs_ref, sems.at[parity], 2 * td)


def _dispatch(h3, pos, zflag, rows, tm):
    t, d = h3.shape
    td = min(TD_ROUTE, t)
    nt = t // td
    pos3 = pos.reshape(nt, 1, 2 * td)
    return pl.pallas_call(
        _dispatch_kernel,
        grid_spec=pltpu.PrefetchScalarGridSpec(
            num_scalar_prefetch=1,
            grid=(nt,),
            in_specs=[
                pl.BlockSpec((1, 1, 2 * td), lambda i, zf: (i, 0, 0), memory_space=pltpu.SMEM),
                pl.BlockSpec((td, d), lambda i, zf: (i, 0)),
            ],
            out_specs=pl.BlockSpec(memory_space=pl.ANY),
            scratch_shapes=[pltpu.VMEM((tm, d), F32), pltpu.VMEM((2, td, d), F32),
                            pltpu.SemaphoreType.DMA((2,)), pltpu.SemaphoreType.DMA],
        ),
        out_shape=jax.ShapeDtypeStruct((rows, d), F32),
        compiler_params=_cparams(("arbitrary",)),
        name="dispatch",
    )(zflag, pos3, h3)


def _combine_kernel(pos_ref, pos_next_ref, x_ref, route_ref, nw_ref, ys_ref, o_ref, buf, sems):
    i = pl.program_id(0)
    nt = pl.num_programs(0)
    td = x_ref.shape[0]

    def issue(p_ref, slot):
        def start(r, carry):
            _row_copy(ys_ref, p_ref[0, 0, 2 * r], buf.at[slot, 0], r, sems.at[slot]).start(
                priority=0)
            _row_copy(ys_ref, p_ref[0, 0, 2 * r + 1], buf.at[slot, 1], r, sems.at[slot]).start(
                priority=1)
            return carry
        lax.fori_loop(0, td, start, 0, unroll=DMA_UNROLL)

    @pl.when(i == 0)
    def _():
        issue(pos_ref, 0)

    for parity in range(2):
        @pl.when(i % 2 == parity)
        def _():
            @pl.when(i + 1 < nt)
            def _():
                issue(pos_next_ref, 1 - parity)

            _wait_rows(ys_ref, buf.at[parity, 0], sems.at[parity], 2 * td)
            route = route_ref[...]
            x = x_ref[...] + route[:, 2:3] * buf[parity, 0] + route[:, 3:4] * buf[parity, 1]
            o_ref[...] = _rmsnorm(x, nw_ref[...])


def _combine(x2, route, pos, ys, norm_w):
    t, d = x2.shape
    td = min(TD_ROUTE, t)
    nt = t // td
    pos3 = pos.reshape(nt, 1, 2 * td)
    return pl.pallas_call(
        _combine_kernel,
        grid=(nt,),
        in_specs=[
            pl.BlockSpec((1, 1, 2 * td), lambda i: (i, 0, 0), memory_space=pltpu.SMEM),
            pl.BlockSpec((1, 1, 2 * td), lambda i: (jnp.minimum(i + 1, nt - 1), 0, 0),
                         memory_space=pltpu.SMEM),
            pl.BlockSpec((td, d), lambda i: (i, 0)),
            pl.BlockSpec((td, LANES), lambda i: (i, 0)),
            pl.BlockSpec((1, d), lambda i: (0, 0)),
            pl.BlockSpec(memory_space=pl.ANY),
        ],
        out_specs=pl.BlockSpec((td, d), lambda i: (i, 0)),
        out_shape=jax.ShapeDtypeStruct((t, d), F32),
        scratch_shapes=[pltpu.VMEM((2, 2, td, d), F32), pltpu.SemaphoreType.DMA((2,))],
        compiler_params=_cparams(("arbitrary",)),
        name="combine",
    )(pos3, pos3, x2, route, norm_w, ys)


def _split_w_in(w_in):
    mw, cw, fw = MLSTM_WIDTH, CONF_CHANNELS, FOX_WIDTH
    sizes = (2 * mw, mw, mw, MLSTM_HEADS, MLSTM_HEADS, 2 * cw, fw, fw, fw, FOX_HEADS)
    parts, off = [], 0
    for sz in sizes:
        parts.append(w_in[:, off:off + sz])
        off += sz
    m_qk, m_v, m_o, m_i, m_f, c_glu, f_q, f_k, f_v, f_f = parts
    w_main = jnp.concatenate([m_qk, m_v, m_o, c_glu, f_q, f_k, f_v], axis=1).astype(BF16)
    w_gate = jnp.concatenate([m_i, m_f, f_f], axis=1)
    w_gate = jnp.pad(w_gate, ((0, 0), (0, LANES - w_gate.shape[1]))).astype(BF16)
    return w_main, w_gate


def _gate_bias(b_i, b_f, fox_b):
    gb = jnp.concatenate([b_i, b_f, fox_b]).astype(F32)
    return jnp.pad(gb, (0, LANES - gb.shape[0])).reshape(1, LANES)


def _route_tables(route, rank, counts, tm, n_tiles_max):
    cnt = counts[0, :N_EXPERTS].astype(jnp.int32)
    tiles = (cnt + tm - 1) // tm
    tile_end = jnp.cumsum(tiles)
    row_off = (tile_end - tiles) * tm
    idx = route[:, 0:2].astype(jnp.int32)
    pos = row_off[idx] + rank[:, 0:2].astype(jnp.int32)
    tile_ids = jnp.arange(n_tiles_max, dtype=jnp.int32)
    tile_expert = jnp.sum(tile_ids[:, None] >= tile_end[None, :], axis=1).astype(jnp.int32)
    tile_expert = jnp.minimum(tile_expert, N_EXPERTS - 1)
    n_tiles = tile_end[-1:].astype(jnp.int32)
    is_last = jnp.any((tile_ids[:, None] == tile_end[None, :] - 1) & (tiles[None, :] > 0), axis=1)
    zero_flag = (is_last | (tile_ids >= n_tiles[0])).astype(jnp.int32)
    return pos.reshape(-1), tile_expert, n_tiles, zero_flag


def kernel(x, mem, norm_mix_w, w_in, mlstm_conv_w, mlstm_conv_b, mlstm_b_i, mlstm_b_f, mlstm_norm_w,
           conf_conv_w, conf_conv_b, conf_ln_w, conf_ln_b, fox_b_f, w_out, norm_xattn_w, norm_mem_w,
           xattn_w_q, xattn_w_kv, xattn_w_o, norm_ffn_w, ffn_w_gate, ffn_w_up, ffn_w_down, router_w,
           moe_w_gate, moe_w_up, moe_w_down, norm_final_w):
    batch, seq, d = x.shape
    depth = w_in.shape[0]
    t = batch * seq
    xf = x.reshape(t, d)
    row = lambda a: a.reshape(1, -1).astype(F32)
    out = None
    for l in range(depth):
        w_main, w_gate = _split_w_in(w_in[l])
        z, gates = _in_proj(xf, row(norm_mix_w[l]), w_main, w_gate)
        hmc, c2 = _seq_mix(
            z, gates, _gate_bias(mlstm_b_i[l], mlstm_b_f[l], fox_b_f[l]),
            mlstm_conv_w[l].astype(F32), row(mlstm_conv_b[l]), row(mlstm_norm_w[l]),
            conf_conv_w[l].astype(F32), row(conf_conv_b[l]), row(conf_ln_w[l]), row(conf_ln_b[l]),
            batch)
        ka, qta, vta, fox_stats = _fox_prep(z, c2)
        visits, cref = _fox_tile_plan(fox_stats, batch)
        dense = l % 2 == 0
        j = l // 2
        if dense:
            ffn_f32 = (ffn_w_gate[j][None], ffn_w_up[j][None], ffn_w_down[j][None])
        else:
            ffn_f32 = (moe_w_gate[j], moe_w_up[j], moe_w_down[j])
        hf, (wg_b, wu_b, wd_b) = _fox_attn(ka, qta, vta, visits, cref, batch, ffn_f32)
        kv = _mem_kv(mem, row(norm_mem_w[l]), xattn_w_kv[l].astype(BF16))
        rw = None
        if not dense:
            rw = jnp.pad(router_w[j].astype(F32).T, ((0, ROUTER_ROWS - N_EXPERTS), (0, 0)))
        res = _post_mix(hmc, hf, xf, w_out[l].astype(BF16), row(norm_xattn_w[l]),
                        xattn_w_q[l].astype(BF16), kv, xattn_w_o[l].astype(BF16),
                        row(norm_ffn_w[l]), rw, batch)
        if dense:
            x2, h3 = res
            nt = t // min(TM_FFN, t)
            xf = _ffn(h3, wg_b, wu_b, wd_b, jnp.zeros((nt,), jnp.int32),
                      jnp.full((1,), nt, jnp.int32), residual=x2)
            if l == depth - 1:
                raise NotImplementedError("final norm after a dense layer")
        else:
            x2, h3, route = res
            rank, counts = _plan(route)
            tm = min(TM_FFN, t)
            n_tiles_max = (2 * t) // tm + N_EXPERTS
            pos, tile_expert, n_tiles, zero_flag = _route_tables(route, rank, counts, tm, n_tiles_max)
            xs = _dispatch(h3, pos, zero_flag, n_tiles_max * tm, tm)
            ys = _ffn(xs, wg_b, wu_b, wd_b, tile_expert, n_tiles)
            assert l == depth - 1
            out = _combine(x2, route, pos, ys, row(norm_final_w))
    return out.reshape(batch, seq, d)
```

```python
import functools

import jax
import jax.numpy as jnp
from jax import lax
from jax.experimental import pallas as pl
from jax.experimental.pallas import tpu as pltpu

F32 = jnp.float32
BF16 = jnp.bfloat16
EPS = 1e-6

MLSTM_HEADS = 4
HEAD_DIM = 64
MLSTM_WIDTH = MLSTM_HEADS * HEAD_DIM
MLSTM_CONV = 4
CONF_CHANNELS = 256
CONF_KERNEL = 31
FOX_HEADS = 8
FOX_WIDTH = FOX_HEADS * HEAD_DIM
XATTN_HEADS = 4
XATTN_HEAD_DIM = 128
XATTN_WIDTH = XATTN_HEADS * XATTN_HEAD_DIM
N_EXPERTS = 8
ROUTER_ROWS = 16
LANES = 128
SUBLANES = 8

Z_QK = 0
Z_V = 512
Z_O = 768
Z_GLU = 1024
Z_SEQ_WIDTH = 1536
Z_FQ = 1536
Z_FK = 2048
Z_FV = 2560
Z_WIDTH = 3072
G_I = 0
G_F = 4
G_FOX = 8

TM_PROJ = 512
SEQ_CHUNK = 256
CONV_TAIL = 8
CONF_TAIL = 32
FOX_TQ = 512
TM_FFN = 512
FF_CHUNKS = (1024, 1024, 768)
TD_ROUTE = 256
DMA_UNROLL = 64
VMEM_LIMIT = 56 * 1024 * 1024


def _cparams(sem, vmem=VMEM_LIMIT):
    return pltpu.CompilerParams(dimension_semantics=sem, vmem_limit_bytes=vmem)


def _sigmoid(x):
    return 1.0 / (1.0 + jnp.exp(-x))


def _log_sigmoid(x):
    return jnp.minimum(x, 0.0) - jnp.log(1.0 + jnp.exp(-jnp.abs(x)))


def _rmsnorm(x, w):
    ms = jnp.mean(x * x, axis=-1, keepdims=True)
    return x * lax.rsqrt(ms + EPS) * w


def _in_proj_kernel(x_ref, nw_ref, w_ref, wg_ref, z_ref, g_ref):
    h = _rmsnorm(x_ref[...], nw_ref[...]).astype(BF16)
    for n in range(0, Z_WIDTH, 512):
        z_ref[:, n:n + 512] = jnp.dot(h, w_ref[:, n:n + 512],
                                      preferred_element_type=F32).astype(BF16)
    g_ref[...] = jnp.dot(h, wg_ref[...], preferred_element_type=F32)


def _in_proj(x, norm_w, w_main, w_gate):
    t, d = x.shape
    tm = min(TM_PROJ, t)
    return pl.pallas_call(
        _in_proj_kernel,
        grid=(t // tm,),
        in_specs=[
            pl.BlockSpec((tm, d), lambda i: (i, 0)),
            pl.BlockSpec((1, d), lambda i: (0, 0)),
            pl.BlockSpec((d, Z_WIDTH), lambda i: (0, 0)),
            pl.BlockSpec((d, LANES), lambda i: (0, 0)),
        ],
        out_specs=[
            pl.BlockSpec((tm, Z_WIDTH), lambda i: (i, 0)),
            pl.BlockSpec((tm, LANES), lambda i: (i, 0)),
        ],
        out_shape=[jax.ShapeDtypeStruct((t, Z_WIDTH), BF16),
                   jax.ShapeDtypeStruct((t, LANES), F32)],
        compiler_params=_cparams(("parallel",)),
        name="in_proj",
    )(x, norm_w, w_main, w_gate)


def _seq_kernel(z_ref, g_ref, gb_ref, cw_ref, cb_ref, mnw_ref, ccw_ref, ccb_ref, lnw_ref, lnb_ref,
                out_ref, c_ref,
                qk_buf, u_buf, ush_buf, cstate, nstate, mstate, carry):
    L = z_ref.shape[0]
    W = MLSTM_WIDTH

    @pl.when(pl.program_id(1) == 0)
    def _():
        qk_buf[0:CONV_TAIL, :] = jnp.zeros((CONV_TAIL, 2 * W), F32)
        u_buf[0:CONF_TAIL, :] = jnp.zeros((CONF_TAIL, CONF_CHANNELS), F32)
        cstate[...] = jnp.zeros_like(cstate)
        nstate[...] = jnp.zeros_like(nstate)
        mstate[...] = jnp.zeros_like(mstate)
        carry[...] = jnp.zeros_like(carry)

    lane_g = lax.broadcasted_iota(jnp.int32, (1, LANES), 1)
    lane_w = lax.broadcasted_iota(jnp.int32, (1, W), 1)

    g = g_ref[...] + gb_ref[...]
    is_forget = (lane_g >= G_F) & (lane_g < G_FOX + FOX_HEADS)
    lsg = jnp.where(is_forget, _log_sigmoid(g), 0.0)
    row = lax.broadcasted_iota(jnp.int32, (L, L), 0)
    col = lax.broadcasted_iota(jnp.int32, (L, L), 1)
    causal = row >= col
    tri = causal.astype(F32)
    cs = jnp.dot(tri, lsg, preferred_element_type=F32, precision=lax.Precision.HIGHEST)
    c_all = cs + carry[...]
    c_ref[...] = c_all * 1.4426950408889634
    carry[...] = c_all[L - 1:L, :]

    qk_buf[CONV_TAIL:CONV_TAIL + L, :] = z_ref[:, Z_QK:Z_QK + 2 * W].astype(F32)
    qk = jnp.zeros((L, 2 * W), F32) + cb_ref[...]
    for j in range(MLSTM_CONV):
        off = CONV_TAIL - (MLSTM_CONV - 1) + j
        qk = qk + qk_buf[off:off + L, :] * cw_ref[j:j + 1, :]
    qk_buf[0:CONV_TAIL, :] = qk_buf[L:L + CONV_TAIL, :]
    qk = qk * _sigmoid(qk)
    q = qk[:, 0:W]
    k = qk[:, W:2 * W] * (HEAD_DIM ** -0.5)
    v = z_ref[:, Z_V:Z_V + W]
    q_b = q.astype(BF16)
    k_b = k.astype(BF16)

    cs_t = cs.T
    g_t = g.T
    m_prev = mstate[...]
    n_prev = nstate[...]
    c_prev = cstate[...]
    qn = q * n_prev
    q_c = jnp.dot(q_b, c_prev.astype(BF16), preferred_element_type=F32)

    num = jnp.zeros((L, W), F32)
    w_inter_l = jnp.zeros((L, W), F32)
    denom_l = jnp.ones((L, W), F32)
    wg_l = jnp.zeros((L, W), F32)
    decay_l = jnp.zeros((1, W), F32)
    m_new_row = m_prev
    for h in range(MLSTM_HEADS):
        hmask = (lane_w // HEAD_DIM) == h
        b_col = cs[:, G_F + h:G_F + h + 1]
        b_row = cs_t[G_F + h:G_F + h + 1, :]
        li_row = g_t[G_I + h:G_I + h + 1, :]
        li_col = g[:, G_I + h:G_I + h + 1]
        m_h = m_prev[:, h:h + 1]
        d_log = jnp.where(causal, b_col - b_row + li_row, -jnp.inf)
        inter = b_col + m_h
        m_t = jnp.maximum(jnp.max(d_log, axis=1, keepdims=True), inter)
        dmat = jnp.exp(d_log - m_t)
        q_h = jnp.where(hmask, q_b, jnp.zeros_like(q_b))
        s = lax.dot_general(q_h, k_b, (((1,), (1,)), ((), ())), preferred_element_type=F32) * dmat
        w_inter = jnp.exp(inter - m_t)
        pv = jnp.dot(s.astype(BF16), v, preferred_element_type=F32)
        num = jnp.where(hmask, pv, num)
        qn_h = jnp.sum(jnp.where(hmask, qn, 0.0), axis=1, keepdims=True)
        den = jnp.sum(s, axis=1, keepdims=True) + w_inter * qn_h
        dn = jnp.maximum(jnp.abs(den), jnp.exp(-m_t))
        w_inter_l = jnp.where(hmask, w_inter, w_inter_l)
        denom_l = jnp.where(hmask, dn, denom_l)
        b_tot = cs[L - 1:L, G_F + h:G_F + h + 1]
        g_h = b_tot - b_col + li_col
        m_new = jnp.maximum(b_tot + m_h, jnp.max(g_h, axis=0, keepdims=True))
        wg_l = jnp.where(hmask, jnp.exp(g_h - m_new), wg_l)
        decay_l = jnp.where(hmask, jnp.exp(b_tot + m_h - m_new), decay_l)
        m_new_row = jnp.where(lane_g == h, m_new, m_new_row)

    hout = (num + w_inter_l * q_c) / denom_l
    hsq = hout * hout
    rs_l = jnp.zeros((L, W), F32)
    for h in range(MLSTM_HEADS):
        hmask = (lane_w // HEAD_DIM) == h
        ms = jnp.sum(jnp.where(hmask, hsq, 0.0), axis=1, keepdims=True) * (1.0 / HEAD_DIM)
        rs_l = jnp.where(hmask, lax.rsqrt(ms + EPS), rs_l)
    o_gate = _sigmoid(z_ref[:, Z_O:Z_O + W].astype(F32))
    out_ref[:, 0:W] = (hout * rs_l * mnw_ref[...] * o_gate).astype(BF16)

    kw = k * wg_l
    upd = lax.dot_general(kw.astype(BF16), v, (((0,), (0,)), ((), ())), preferred_element_type=F32)
    rk = lax.broadcasted_iota(jnp.int32, (W, W), 0) // HEAD_DIM
    rv = lax.broadcasted_iota(jnp.int32, (W, W), 1) // HEAD_DIM
    cstate[...] = decay_l * c_prev + jnp.where(rk == rv, upd, 0.0)
    nstate[...] = decay_l * n_prev + jnp.sum(kw, axis=0, keepdims=True)
    mstate[...] = m_new_row

    a = z_ref[:, Z_GLU:Z_GLU + CONF_CHANNELS].astype(F32)
    gg = z_ref[:, Z_GLU + CONF_CHANNELS:Z_GLU + 2 * CONF_CHANNELS].astype(F32)
    u_buf[CONF_TAIL:CONF_TAIL + L, :] = a * _sigmoid(gg)
    span = L + CONF_TAIL - SUBLANES
    for r in range(1, SUBLANES):
        ush_buf[r - 1] = u_buf[r:r + span, :]
    hc = jnp.zeros((L, CONF_CHANNELS), F32) + ccb_ref[...]
    for j in range(CONF_KERNEL):
        off = CONF_TAIL - (CONF_KERNEL - 1) + j
        base, r = off - off % SUBLANES, off % SUBLANES
        src = u_buf[base:base + L, :] if r == 0 else ush_buf[r - 1, base:base + L, :]
        hc = hc + src * ccw_ref[j:j + 1, :]
    u_buf[0:CONF_TAIL, :] = u_buf[L:L + CONF_TAIL, :]
    mu = jnp.mean(hc, axis=1, keepdims=True)
    xc = hc - mu
    var = jnp.mean(xc * xc, axis=1, keepdims=True)
    y = xc * lax.rsqrt(var + EPS) * lnw_ref[...] + lnb_ref[...]
    out_ref[:, W:W + CONF_CHANNELS] = (y * _sigmoid(y)).astype(BF16)


def _seq_mix(z, gates, gate_bias, conv_w, conv_b, mnorm_w, cconv_w, cconv_b, ln_w, ln_b, batch):
    t = z.shape[0]
    s = t // batch
    L = min(SEQ_CHUNK, s)
    nc = s // L
    W = MLSTM_WIDTH
    full = lambda a: pl.BlockSpec(a.shape, lambda b, c: (0,) * a.ndim)
    return pl.pallas_call(
        _seq_kernel,
        grid=(batch, nc),
        in_specs=[
            pl.BlockSpec((L, Z_SEQ_WIDTH), lambda b, c: (b * nc + c, 0)),
            pl.BlockSpec((L, LANES), lambda b, c: (b * nc + c, 0)),
            full(gate_bias), full(conv_w), full(conv_b), full(mnorm_w),
            full(cconv_w), full(cconv_b), full(ln_w), full(ln_b),
        ],
        out_specs=[
            pl.BlockSpec((L, W + CONF_CHANNELS), lambda b, c: (b * nc + c, 0)),
            pl.BlockSpec((L, LANES), lambda b, c: (b * nc + c, 0)),
        ],
        out_shape=[jax.ShapeDtypeStruct((t, W + CONF_CHANNELS), BF16),
                   jax.ShapeDtypeStruct((t, LANES), F32)],
        scratch_shapes=[
            pltpu.VMEM((CONV_TAIL + L + CONV_TAIL, 2 * W), F32),
            pltpu.VMEM((CONF_TAIL + L + CONF_TAIL, CONF_CHANNELS), F32),
            pltpu.VMEM((SUBLANES - 1, L + CONF_TAIL - SUBLANES, CONF_CHANNELS), F32),
            pltpu.VMEM((W, W), F32),
            pltpu.VMEM((1, W), F32),
            pltpu.VMEM((1, LANES), F32),
            pltpu.VMEM((1, LANES), F32),
        ],
        compiler_params=_cparams(("arbitrary", "arbitrary")),
        name="seq_mix",
    )(z, gates, gate_bias, conv_w, conv_b, mnorm_w, cconv_w, cconv_b, ln_w, ln_b)


AUG_NEG_A = 0
AUG_ONE = 3
ST_ROWS = 8
FOX_KSUB = 1
ST_NORMS = 0
ST_FIRST = 1
ST_LAST = 2
FOX_SKIP_LOG2 = 160.0
NORM_SLACK = 1.01


def _split3(x):
    hi = x.astype(BF16).astype(F32)
    mid = (x - hi).astype(BF16).astype(F32)
    lo = (x - hi - mid).astype(BF16).astype(F32)
    return hi, mid, lo


def _fox_prep_kernel(zq_ref, zk_ref, zv_ref, c2_ref, ka_ref, qta_ref, vta_ref, st_ref):
    tp = zq_ref.shape[0]
    c2 = c2_ref[...]
    a = c2 - c2[0:1, :]
    c2_t = c2.T
    b_t = c2_t - c2_t[:, 0:1]
    lane = lax.broadcasted_iota(jnp.int32, (1, LANES), 1)
    row = lax.broadcasted_iota(jnp.int32, (LANES, 1), 0)
    q_scale = (HEAD_DIM ** -0.5) * 1.4426950408889634
    norms = jnp.zeros((1, LANES), F32)
    for p in range(FOX_HEADS // 2):
        sl = slice(p * LANES, (p + 1) * LANES)
        kp = zk_ref[:, sl].astype(F32)
        q_r = (zq_ref[:, sl].astype(F32) * q_scale).astype(BF16).astype(F32)
        q_t = q_r.T
        v_t = zv_ref[:, sl].astype(F32).T
        sq = jnp.concatenate([q_r * q_r, kp * kp], axis=1).astype(BF16)
        grp = lax.broadcasted_iota(jnp.int32, (2 * LANES, LANES), 0) // HEAD_DIM
        tks = tp // FOX_KSUB
        for r in range(FOX_KSUB):
            dst = jnp.where(grp < 2, 2 * p + grp, (1 + r) * FOX_HEADS + 2 * p + grp - 2)
            sel = (lax.broadcasted_iota(jnp.int32, (2 * LANES, LANES), 1) == dst).astype(BF16)
            n2 = jnp.max(jnp.dot(sq[r * tks:(r + 1) * tks], sel, preferred_element_type=F32),
                         axis=0, keepdims=True)
            norms = jnp.maximum(norms, jnp.sqrt(n2) * NORM_SLACK)
        for hh in range(2):
            h = 2 * p + hh
            own_lo = hh * HEAD_DIM
            o = (1 - hh) * HEAD_DIM
            a_hi, a_mid, a_lo = _split3(a[:, G_FOX + h:G_FOX + h + 1])
            b_hi, b_mid, b_lo = _split3(b_t[G_FOX + h:G_FOX + h + 1, :])
            own_lane = (lane >= own_lo) & (lane < own_lo + HEAD_DIM)
            own_row = (row >= own_lo) & (row < own_lo + HEAD_DIM)
            ones_l = ((lane >= o + AUG_ONE) & (lane < o + AUG_ONE + 3)).astype(F32)
            ka = jnp.where(own_lane, kp, ones_l)
            ka = jnp.where(lane == o + AUG_NEG_A, -a_hi, ka)
            ka = jnp.where(lane == o + AUG_NEG_A + 1, -a_mid, ka)
            ka = jnp.where(lane == o + AUG_NEG_A + 2, -a_lo, ka)
            ones_r = ((row >= o + AUG_NEG_A) & (row < o + AUG_NEG_A + 3)).astype(F32)
            qa = jnp.where(own_row, q_t, ones_r)
            qa = jnp.where(row == o + AUG_ONE, b_hi, qa)
            qa = jnp.where(row == o + AUG_ONE + 1, b_mid, qa)
            qa = jnp.where(row == o + AUG_ONE + 2, b_lo, qa)
            va = jnp.where(own_row, v_t, (row == o).astype(F32))
            hs = slice(h * LANES, (h + 1) * LANES)
            ka_ref[:, hs] = ka.astype(BF16)
            qta_ref[hs, :] = qa.astype(BF16)
            vta_ref[hs, :] = va.astype(BF16)
    st_ref[ST_NORMS:ST_NORMS + 1, :] = norms
    st_ref[ST_FIRST:ST_FIRST + 1, :] = c2[0:1, :]
    tks = tp // FOX_KSUB
    for r in range(FOX_KSUB):
        st_ref[ST_LAST + r:ST_LAST + r + 1, :] = c2[(r + 1) * tks - 1:(r + 1) * tks, :]
    st_ref[ST_LAST + FOX_KSUB:, :] = jnp.zeros((ST_ROWS - ST_LAST - FOX_KSUB, LANES), F32)


def _fox_prep(z, c2):
    t = z.shape[0]
    tp = min(FOX_TQ, t)
    width = FOX_HEADS * LANES
    return pl.pallas_call(
        _fox_prep_kernel,
        grid=(t // tp,),
        in_specs=[
            pl.BlockSpec((tp, FOX_WIDTH), lambda i: (i, Z_FQ // FOX_WIDTH)),
            pl.BlockSpec((tp, FOX_WIDTH), lambda i: (i, Z_FK // FOX_WIDTH)),
            pl.BlockSpec((tp, FOX_WIDTH), lambda i: (i, Z_FV // FOX_WIDTH)),
            pl.BlockSpec((tp, LANES), lambda i: (i, 0)),
        ],
        out_specs=[
            pl.BlockSpec((tp, width), lambda i: (i, 0)),
            pl.BlockSpec((width, tp), lambda i: (0, i)),
            pl.BlockSpec((width, tp), lambda i: (0, i)),
            pl.BlockSpec((ST_ROWS, LANES), lambda i: (i, 0)),
        ],
        out_shape=[jax.ShapeDtypeStruct((t, width), BF16),
                   jax.ShapeDtypeStruct((width, t), BF16),
                   jax.ShapeDtypeStruct((width, t), BF16),
                   jax.ShapeDtypeStruct((t // tp * ST_ROWS, LANES), F32)],
        compiler_params=_cparams(("parallel",)),
        name="fox_prep",
    )(z, z, z, c2)


def _fox_tile_plan(stats, batch):
    st = stats.reshape(batch, -1, ST_ROWS, LANES)
    nq = st.shape[1]
    nk = nq * FOX_KSUB
    qmax = st[:, :, ST_NORMS, 0:FOX_HEADS]
    kmax = st[:, :, ST_NORMS, FOX_HEADS:(1 + FOX_KSUB) * FOX_HEADS]
    kmax = kmax.reshape(batch, nk, FOX_HEADS)
    kdiag = jnp.max(kmax.reshape(batch, nq, FOX_KSUB, FOX_HEADS), axis=2)
    first = st[:, :, ST_FIRST, G_FOX:G_FOX + FOX_HEADS]
    last = st[:, :, ST_LAST:ST_LAST + FOX_KSUB, G_FOX:G_FOX + FOX_HEADS].reshape(batch, nk, FOX_HEADS)
    ub = (qmax[:, :, None, :] * (kmax[:, None, :, :] + kdiag[:, :, None, :])
          + first[:, :, None, :] - last[:, None, :, :])
    ti = jnp.arange(nq)[:, None]
    tj = jnp.arange(nk)[None, :]
    diag_lo = FOX_KSUB * ti
    need = (tj < diag_lo) & ~(ub < -FOX_SKIP_LOG2).transpose(0, 3, 1, 2)
    jmin = jnp.min(jnp.where(need, tj, diag_lo), axis=-1)
    count = FOX_KSUB * (jnp.arange(nq)[None, None, :] + 1) - jmin
    count = jnp.max(count.reshape(batch, FOX_HEADS // 2, 2, nq), axis=2)
    cref = first.transpose(0, 2, 1).reshape(-1)
    return count.reshape(-1).astype(jnp.int32), cref


def _fox_kernel(n_cast, cnt_ref, cref_ref, qta_ref, ka_ref, vta_ref, *rest):
    cast_in = rest[:n_cast]
    o_ref = rest[n_cast]
    cast_out = rest[n_cast + 1:2 * n_cast + 1]
    m_sc, acc_sc, s0_sc, cm0_sc, s1_sc, cm1_sc = rest[2 * n_cast + 1:]
    for src_ref, dst_ref in zip(cast_in, cast_out):
        dst_ref[...] = src_ref[...].astype(BF16)
    tq = qta_ref.shape[1]
    tk = tq // FOX_KSUB
    b = pl.program_id(0)
    p = pl.program_id(1)
    i = pl.program_id(2)
    nq = pl.num_programs(2)
    for hh in range(2):
        m_sc[hh] = jnp.full((1, tq), -jnp.inf, F32)
        acc_sc[hh] = jnp.zeros((LANES, tq), F32)

    slots = ((s0_sc, cm0_sc), (s1_sc, cm1_sc))
    diag0 = FOX_KSUB * i

    def sub_tile(v):
        return jnp.where(v < FOX_KSUB, diag0 + v, diag0 + FOX_KSUB - 1 - v)

    def scores(v, diag_sub, slot):
        s_sc, cm_sc = slots[slot]
        start = pl.multiple_of(sub_tile(v) * tk, tk)
        for hh in range(2):
            hs = slice(hh * LANES, (hh + 1) * LANES)
            s = jnp.dot(ka_ref[pl.ds(start, tk), hs], qta_ref[hs, :],
                        preferred_element_type=F32)
            if diag_sub is not None:
                kr = lax.broadcasted_iota(jnp.int32, (tk, tq), 0) + diag_sub * tk
                qc = lax.broadcasted_iota(jnp.int32, (tk, tq), 1)
                s = jnp.where(kr <= qc, s, -jnp.inf)
            s_sc[hh] = s
            cm_sc[hh] = jnp.max(s, axis=0, keepdims=True)

    def softmax_pv(v, slot):
        s_sc, cm_sc = slots[slot]
        j = sub_tile(v)
        start = pl.multiple_of(j * tk, tk)
        for hh in range(2):
            hs = slice(hh * LANES, (hh + 1) * LANES)
            base = (b * FOX_HEADS + 2 * p + hh) * nq
            delta = cref_ref[base + i] - cref_ref[base + j // FOX_KSUB]
            m_old = m_sc[hh]
            m_new = jnp.maximum(m_old, cm_sc[hh] + delta)
            alpha = jnp.exp2(m_old - m_new)
            pm = jnp.exp2(s_sc[hh] - (m_new - delta)).astype(BF16)
            acc_sc[hh] = alpha * acc_sc[hh] + jnp.dot(vta_ref[hs, pl.ds(start, tk)], pm,
                                                      preferred_element_type=F32)
            m_sc[hh] = m_new

    n_visits = cnt_ref[(b * pl.num_programs(1) + p) * nq + i]
    scores(0, 0, 0)
    for v in range(FOX_KSUB - 1):
        scores(v + 1, v + 1, (v + 1) % 2)
        softmax_pv(v, v % 2)

    def body(v, carry):
        for parity in range(2):
            @pl.when(v % 2 == parity)
            def _():
                scores(v + 1, None, 1 - parity)
                softmax_pv(v, parity)
        return carry

    lax.fori_loop(FOX_KSUB - 1, n_visits - 1, body, 0)
    for parity in range(2):
        @pl.when((n_visits - 1) % 2 == parity)
        def _():
            softmax_pv(n_visits - 1, parity)
    acc0 = acc_sc[0]
    acc1 = acc_sc[1]
    row = lax.broadcasted_iota(jnp.int32, (LANES, 1), 0)
    o_t = jnp.where(row < HEAD_DIM, acc0 / acc0[HEAD_DIM:HEAD_DIM + 1, :], acc1 / acc1[0:1, :])
    o_ref[...] = o_t.T.astype(BF16)


def _row_block_cast_jobs(weights, n_steps):
    jobs = []
    for w in weights:
        lead = w.shape[:-2]
        n_lead = 1
        for v in lead:
            n_lead *= v
        rows, cols = w.shape[-2:]
        per_lead = max(n_steps // n_lead, 1)
        blk = rows
        for cand in range(2 * SUBLANES, rows + 1, 2 * SUBLANES):
            if rows % cand == 0 and rows // cand <= per_lead:
                blk = cand
                break
        nblk = rows // blk
        total = n_lead * nblk

        def index(step, nblk=nblk, total=total, lead=lead):
            k = jnp.minimum(step, total - 1)
            idx = [k % nblk, 0]
            k = k // nblk
            for v in reversed(lead):
                idx.insert(0, k % v)
                k = k // v
            return tuple(idx)

        jobs.append((w, (1,) * len(lead) + (blk, cols), index))
    return jobs


def _fox_attn(ka, qta, vta, visits, cref, batch, cast_weights=()):
    t = ka.shape[0]
    s = t // batch
    tq = min(FOX_TQ, s)
    nq = s // tq
    npairs = FOX_HEADS // 2
    jobs = _row_block_cast_jobs(cast_weights, batch * npairs * nq)
    step_of = lambda b, p, i: (b * npairs + p) * nq + i
    cast_specs = [pl.BlockSpec(blk, lambda b, p, i, nv, cr, f=f: f(step_of(b, p, i)))
                  for _, blk, f in jobs]
    outs = pl.pallas_call(
        functools.partial(_fox_kernel, len(jobs)),
        grid_spec=pltpu.PrefetchScalarGridSpec(
            num_scalar_prefetch=2,
            grid=(batch, npairs, nq),
            in_specs=[
                pl.BlockSpec((2 * LANES, tq), lambda b, p, i, nv, cr: (p, b * nq + i)),
                pl.BlockSpec((s, 2 * LANES), lambda b, p, i, nv, cr: (b, p)),
                pl.BlockSpec((2 * LANES, s), lambda b, p, i, nv, cr: (p, b)),
            ] + cast_specs,
            out_specs=[pl.BlockSpec((tq, LANES), lambda b, p, i, nv, cr: (b * nq + i, p))]
            + cast_specs,
            scratch_shapes=[
                pltpu.VMEM((2, 1, tq), F32),
                pltpu.VMEM((2, LANES, tq), F32),
                pltpu.VMEM((2, tq // FOX_KSUB, tq), F32),
                pltpu.VMEM((2, 1, tq), F32),
                pltpu.VMEM((2, tq // FOX_KSUB, tq), F32),
                pltpu.VMEM((2, 1, tq), F32),
            ],
        ),
        out_shape=[jax.ShapeDtypeStruct((t, FOX_WIDTH), BF16)]
        + [jax.ShapeDtypeStruct(w.shape, BF16) for w, _, _ in jobs],
        compiler_params=_cparams(("arbitrary", "arbitrary", "arbitrary")),
        name="fox_attn",
    )(visits, cref, qta, ka, vta, *[w for w, _, _ in jobs])
    return outs[0], outs[1:]


def _mem_kv_kernel(mem_ref, nw_ref, w_ref, o_ref):
    h = _rmsnorm(mem_ref[0], nw_ref[...]).astype(BF16)
    o_ref[0] = jnp.dot(h, w_ref[...], preferred_element_type=F32).astype(BF16)


def _mem_kv(mem, norm_w, w_kv):
    b, m, d = mem.shape
    return pl.pallas_call(
        _mem_kv_kernel,
        grid=(b,),
        in_specs=[
            pl.BlockSpec((1, m, d), lambda i: (i, 0, 0)),
            pl.BlockSpec((1, d), lambda i: (0, 0)),
            pl.BlockSpec(w_kv.shape, lambda i: (0, 0)),
        ],
        out_specs=pl.BlockSpec((1, m, w_kv.shape[1]), lambda i: (i, 0, 0)),
        out_shape=jax.ShapeDtypeStruct((b, m, w_kv.shape[1]), BF16),
        compiler_params=_cparams(("parallel",)),
        name="mem_kv",
    )(mem, norm_w, w_kv)


def _post_mix_kernel(with_router, hmc_ref, hf_ref, x_ref, wout_ref, nxw_ref, wq_ref, kv_ref, wo_ref,
                     nfw_ref, *rest):
    if with_router:
        rw_ref, x2_ref, h3_ref, route_ref = rest
    else:
        x2_ref, h3_ref = rest
    half = MLSTM_WIDTH + CONF_CHANNELS
    x1 = (x_ref[...]
          + jnp.dot(hmc_ref[...], wout_ref[0:half, :], preferred_element_type=F32)
          + jnp.dot(hf_ref[...], wout_ref[half:, :], preferred_element_type=F32))
    h2 = _rmsnorm(x1, nxw_ref[...]).astype(BF16)
    q = jnp.dot(h2, wq_ref[...], preferred_element_type=F32) * (XATTN_HEAD_DIM ** -0.5)
    q = q.astype(BF16)
    kv = kv_ref[0]
    outs = []
    for h in range(XATTN_HEADS):
        lo = h * XATTN_HEAD_DIM
        kh = kv[:, lo:lo + XATTN_HEAD_DIM]
        vh = kv[:, XATTN_WIDTH + lo:XATTN_WIDTH + lo + XATTN_HEAD_DIM]
        s = lax.dot_general(q[:, lo:lo + XATTN_HEAD_DIM], kh, (((1,), (1,)), ((), ())),
                            preferred_element_type=F32)
        s = s - jnp.max(s, axis=1, keepdims=True)
        e = jnp.exp(s)
        pm = e / jnp.sum(e, axis=1, keepdims=True)
        outs.append(jnp.dot(pm.astype(BF16), vh, preferred_element_type=F32).astype(BF16))
    o = jnp.concatenate(outs, axis=1)
    x2 = x1 + jnp.dot(o, wo_ref[...], preferred_element_type=F32)
    x2_ref[...] = x2
    h3 = _rmsnorm(x2, nfw_ref[...])
    h3_ref[...] = h3.astype(h3_ref.dtype)
    if with_router:
        h_hi = h3.astype(BF16)
        h_lo = (h3 - h_hi.astype(F32)).astype(BF16)
        rw = rw_ref[...]
        w_hi = rw.astype(BF16)
        w_lo = (rw - w_hi.astype(F32)).astype(BF16)
        nt_dims = (((1,), (1,)), ((), ()))
        on_hi = lax.dot_general(jnp.concatenate([w_hi, w_lo], axis=0), h_hi, nt_dims,
                                preferred_element_type=F32)
        on_lo = lax.dot_general(w_hi, h_lo, nt_dims, preferred_element_type=F32)
        logits_t = on_hi[0:ROUTER_ROWS] + on_hi[ROUTER_ROWS:] + on_lo
        le = logits_t[0:N_EXPERTS, :]
        eid = lax.broadcasted_iota(jnp.int32, (N_EXPERTS, 1), 0).astype(F32)
        m1 = jnp.max(le, axis=0, keepdims=True)
        i1 = jnp.min(jnp.where(le == m1, eid, float(N_EXPERTS)), axis=0, keepdims=True)
        rest_l = jnp.where(eid == i1, -jnp.inf, le)
        m2 = jnp.max(rest_l, axis=0, keepdims=True)
        i2 = jnp.min(jnp.where(rest_l == m2, eid, float(N_EXPERTS)), axis=0, keepdims=True)
        e2 = jnp.exp(m2 - m1)
        w1 = 1.0 / (1.0 + e2)
        w2 = e2 * w1
        r8 = jnp.where(eid == 0, i1, jnp.where(eid == 1, i2, jnp.where(eid == 2, w1,
                       jnp.where(eid == 3, w2, 0.0))))
        route_t = jnp.concatenate([r8, jnp.zeros((LANES - N_EXPERTS, r8.shape[1]), F32)], axis=0)
        route_ref[...] = route_t.T


def _post_mix(hmc, hf, x, w_out, nx_w, w_q, kv, w_o, nf_w, router_w, batch):
    t, d = x.shape
    tm = min(TM_PROJ, t // batch)
    per_b = (t // batch) // tm
    with_router = router_w is not None
    const = lambda a: pl.BlockSpec(a.shape, lambda i: (0,) * a.ndim)
    in_specs = [
        pl.BlockSpec((tm, hmc.shape[1]), lambda i: (i, 0)),
        pl.BlockSpec((tm, hf.shape[1]), lambda i: (i, 0)),
        pl.BlockSpec((tm, d), lambda i: (i, 0)),
        const(w_out), const(nx_w), const(w_q),
        pl.BlockSpec((1,) + kv.shape[1:], lambda i: (i // per_b, 0, 0)),
        const(w_o), const(nf_w),
    ]
    args = [hmc, hf, x, w_out, nx_w, w_q, kv, w_o, nf_w]
    out_specs = [pl.BlockSpec((tm, d), lambda i: (i, 0)), pl.BlockSpec((tm, d), lambda i: (i, 0))]
    out_shape = [jax.ShapeDtypeStruct((t, d), F32),
                 jax.ShapeDtypeStruct((t, d), F32 if with_router else BF16)]
    if with_router:
        in_specs.append(const(router_w))
        args.append(router_w)
        out_specs.append(pl.BlockSpec((tm, LANES), lambda i: (i, 0)))
        out_shape.append(jax.ShapeDtypeStruct((t, LANES), F32))
    return pl.pallas_call(
        functools.partial(_post_mix_kernel, with_router),
        grid=(t // tm,),
        in_specs=in_specs, out_specs=out_specs, out_shape=out_shape,
        compiler_params=_cparams(("parallel",)),
        name="post_mix_router" if with_router else "post_mix",
    )(*args)


def _ffn_kernel(with_residual, te_ref, nt_ref, x_ref, wg_ref, wu_ref, wd_ref, *rest):
    if with_residual:
        res_ref, y_ref = rest
    else:
        (y_ref,) = rest

    @pl.when(pl.program_id(0) < nt_ref[0])
    def _():
        xb = x_ref[...].astype(BF16)
        acc = res_ref[...] if with_residual else None
        lo = 0
        for fc in FF_CHUNKS:
            a = jnp.dot(xb, wg_ref[0, :, lo:lo + fc], preferred_element_type=F32)
            u = jnp.dot(xb, wu_ref[0, :, lo:lo + fc], preferred_element_type=F32)
            hcur = (a * _sigmoid(a) * u).astype(BF16)
            part = jnp.dot(hcur, wd_ref[0, lo:lo + fc, :], preferred_element_type=F32)
            acc = part if acc is None else acc + part
            lo += fc
        y_ref[...] = acc

    @pl.when(pl.program_id(0) >= nt_ref[0])
    def _():
        y_ref[...] = jnp.zeros_like(y_ref)


def _ffn(x, w_gate, w_up, w_down, tile_expert, n_tiles, residual=None):
    rows, d = x.shape
    tm = min(TM_FFN, rows)
    nt = rows // tm
    ff = w_gate.shape[2]
    assert sum(FF_CHUNKS) == ff
    with_residual = residual is not None

    def row_map(i, te, ntl):
        return (jnp.minimum(i, ntl[0] - 1), 0)

    def w_map(i, te, ntl):
        return (te[jnp.minimum(i, ntl[0] - 1)], 0, 0)

    in_specs = [
        pl.BlockSpec((tm, d), row_map),
        pl.BlockSpec((1, d, ff), w_map),
        pl.BlockSpec((1, d, ff), w_map),
        pl.BlockSpec((1, ff, d), w_map),
    ]
    args = [x, w_gate, w_up, w_down]
    if with_residual:
        in_specs.append(pl.BlockSpec((tm, d), row_map))
        args.append(residual)
    return pl.pallas_call(
        functools.partial(_ffn_kernel, with_residual),
        grid_spec=pltpu.PrefetchScalarGridSpec(
            num_scalar_prefetch=2,
            grid=(nt,),
            in_specs=in_specs,
            out_specs=pl.BlockSpec((tm, d), lambda i, te, ntl: (i, 0)),
        ),
        out_shape=jax.ShapeDtypeStruct((rows, d), F32),
        compiler_params=_cparams(("arbitrary",)),
        name="ffn_dense" if with_residual else "ffn_experts",
    )(tile_expert, n_tiles, *args)


def _plan_kernel(route_ref, rank_ref, cnt_ref, carry):
    tp = route_ref.shape[0]

    @pl.when(pl.program_id(0) == 0)
    def _():
        carry[...] = jnp.zeros_like(carry)

    lane_i = lax.broadcasted_iota(jnp.int32, (1, LANES), 1)
    lane = lane_i.astype(F32)
    route = route_ref[...]
    i1 = route[:, 0:1]
    i2 = route[:, 1:2]
    onehot = (lane == i1).astype(F32) + (lane == i2).astype(F32)
    row = lax.broadcasted_iota(jnp.int32, (tp, tp), 0)
    col = lax.broadcasted_iota(jnp.int32, (tp, tp), 1)
    strict = (row > col).astype(BF16)
    before = jnp.dot(strict, onehot.astype(BF16), preferred_element_type=F32) + carry[...]
    r1 = jnp.sum(jnp.where(lane == i1, before, 0.0), axis=1, keepdims=True)
    r2 = jnp.sum(jnp.where(lane == i2, before, 0.0), axis=1, keepdims=True)
    rank_ref[...] = jnp.where(lane_i == 0, r1, jnp.where(lane_i == 1, r2, 0.0))
    total = carry[...] + jnp.sum(onehot, axis=0, keepdims=True)
    carry[...] = total
    cnt_ref[...] = total


def _plan(route):
    t = route.shape[0]
    tp = min(512, t)
    return pl.pallas_call(
        _plan_kernel,
        grid=(t // tp,),
        in_specs=[pl.BlockSpec((tp, LANES), lambda i: (i, 0))],
        out_specs=[pl.BlockSpec((tp, LANES), lambda i: (i, 0)),
                   pl.BlockSpec((1, LANES), lambda i: (0, 0))],
        out_shape=[jax.ShapeDtypeStruct((t, LANES), F32), jax.ShapeDtypeStruct((1, LANES), F32)],
        scratch_shapes=[pltpu.VMEM((1, LANES), F32)],
        compiler_params=_cparams(("arbitrary",)),
        name="route_plan",
    )(route)


def _row_copy(src, src_row, dst, dst_row, sem):
    return pltpu.make_async_copy(src.at[pl.ds(src_row, 1), :], dst.at[pl.ds(dst_row, 1), :], sem)


def _wait_rows(src, dst, sem, n):
    def wait(r, carry):
        _row_copy(src, 0, dst, 0, sem).wait()
        return carry
    lax.fori_loop(0, n, wait, 0, unroll=DMA_UNROLL)


def _dispatch_kernel(zflag_ref, pos_ref, h_ref, xs_ref, zero_buf, stage, sems, zsem):
    i = pl.program_id(0)
    nt = pl.num_programs(0)
    td = h_ref.shape[0]
    tm = zero_buf.shape[0]

    @pl.when(i == 0)
    def _():
        zero_buf[...] = jnp.zeros_like(zero_buf)

        def zero_tile(tile, carry):
            @pl.when(zflag_ref[tile] != 0)
            def _():
                row0 = pl.multiple_of(tile * tm, tm)
                cp = pltpu.make_async_copy(zero_buf, xs_ref.at[pl.ds(row0, tm), :], zsem)
                cp.start()
                cp.wait()
            return carry

        lax.fori_loop(0, zflag_ref.shape[0], zero_tile, 0)

    def issue(slot):
        src = stage.at[slot]
        for r in range(td):
            _row_copy(src, r, xs_ref, pos_ref[0, 0, 2 * r], sems.at[slot]).start(priority=0)
            _row_copy(src, r, xs_ref, pos_ref[0, 0, 2 * r + 1], sems.at[slot]).start(priority=1)

    for parity in range(2):
        @pl.when(i % 2 == parity)
        def _():
            stage[parity] = h_ref[...]
            issue(parity)

            @pl.when(i > 0)
            def _():
                _wait_rows(stage.at[1 - parity], xs_ref, sems.at[1 - parity], 2 * td)

            @pl.when(i == nt - 1)
            def _():
                _wait_rows(stage.at[parity], xs_ref, sems.at[parity], 2 * td)


def _dispatch(h3, pos, zflag, rows, tm):
    t, d = h3.shape
    td = min(TD_ROUTE, t)
    nt = t // td
    pos3 = pos.reshape(nt, 1, 2 * td)
    return pl.pallas_call(
        _dispatch_kernel,
        grid_spec=pltpu.PrefetchScalarGridSpec(
            num_scalar_prefetch=1,
            grid=(nt,),
            in_specs=[
                pl.BlockSpec((1, 1, 2 * td), lambda i, zf: (i, 0, 0), memory_space=pltpu.SMEM),
                pl.BlockSpec((td, d), lambda i, zf: (i, 0)),
            ],
            out_specs=pl.BlockSpec(memory_space=pl.ANY),
            scratch_shapes=[pltpu.VMEM((tm, d), F32), pltpu.VMEM((2, td, d), F32),
                            pltpu.SemaphoreType.DMA((2,)), pltpu.SemaphoreType.DMA],
        ),
        out_shape=jax.ShapeDtypeStruct((rows, d), F32),
        compiler_params=_cparams(("arbitrary",)),
        name="dispatch",
    )(zflag, pos3, h3)


def _combine_kernel(pos_ref, pos_next_ref, x_ref, route_ref, nw_ref, ys_ref, o_ref, buf, sems):
    i = pl.program_id(0)
    nt = pl.num_programs(0)
    td = x_ref.shape[0]

    def issue(p_ref, slot):
        for r in range(td):
            _row_copy(ys_ref, p_ref[0, 0, 2 * r], buf.at[slot, 0], r, sems.at[slot]).start(
                priority=0)
            _row_copy(ys_ref, p_ref[0, 0, 2 * r + 1], buf.at[slot, 1], r, sems.at[slot]).start(
                priority=1)

    @pl.when(i == 0)
    def _():
        issue(pos_ref, 0)

    for parity in range(2):
        @pl.when(i % 2 == parity)
        def _():
            @pl.when(i + 1 < nt)
            def _():
                issue(pos_next_ref, 1 - parity)

            _wait_rows(ys_ref, buf.at[parity, 0], sems.at[parity], 2 * td)
            route = route_ref[...]
            x = x_ref[...] + route[:, 2:3] * buf[parity, 0] + route[:, 3:4] * buf[parity, 1]
            o_ref[...] = _rmsnorm(x, nw_ref[...])


def _combine(x2, route, pos, ys, norm_w):
    t, d = x2.shape
    td = min(TD_ROUTE, t)
    nt = t // td
    pos3 = pos.reshape(nt, 1, 2 * td)
    return pl.pallas_call(
        _combine_kernel,
        grid=(nt,),
        in_specs=[
            pl.BlockSpec((1, 1, 2 * td), lambda i: (i, 0, 0), memory_space=pltpu.SMEM),
            pl.BlockSpec((1, 1, 2 * td), lambda i: (jnp.minimum(i + 1, nt - 1), 0, 0),
                         memory_space=pltpu.SMEM),
            pl.BlockSpec((td, d), lambda i: (i, 0)),
            pl.BlockSpec((td, LANES), lambda i: (i, 0)),
            pl.BlockSpec((1, d), lambda i: (0, 0)),
            pl.BlockSpec(memory_space=pl.ANY),
        ],
        out_specs=pl.BlockSpec((td, d), lambda i: (i, 0)),
        out_shape=jax.ShapeDtypeStruct((t, d), F32),
        scratch_shapes=[pltpu.VMEM((2, 2, td, d), F32), pltpu.SemaphoreType.DMA((2,))],
        compiler_params=_cparams(("arbitrary",)),
        name="combine",
    )(pos3, pos3, x2, route, norm_w, ys)


def _split_w_in(w_in):
    mw, cw, fw = MLSTM_WIDTH, CONF_CHANNELS, FOX_WIDTH
    sizes = (2 * mw, mw, mw, MLSTM_HEADS, MLSTM_HEADS, 2 * cw, fw, fw, fw, FOX_HEADS)
    parts, off = [], 0
    for sz in sizes:
        parts.append(w_in[:, off:off + sz])
        off += sz
    m_qk, m_v, m_o, m_i, m_f, c_glu, f_q, f_k, f_v, f_f = parts
    w_main = jnp.concatenate([m_qk, m_v, m_o, c_glu, f_q, f_k, f_v], axis=1).astype(BF16)
    w_gate = jnp.concatenate([m_i, m_f, f_f], axis=1)
    w_gate = jnp.pad(w_gate, ((0, 0), (0, LANES - w_gate.shape[1]))).astype(BF16)
    return w_main, w_gate


def _gate_bias(b_i, b_f, fox_b):
    gb = jnp.concatenate([b_i, b_f, fox_b]).astype(F32)
    return jnp.pad(gb, (0, LANES - gb.shape[0])).reshape(1, LANES)


def _route_tables(route, rank, counts, tm, n_tiles_max):
    cnt = counts[0, :N_EXPERTS].astype(jnp.int32)
    tiles = (cnt + tm - 1) // tm
    tile_end = jnp.cumsum(tiles)
    row_off = (tile_end - tiles) * tm
    idx = route[:, 0:2].astype(jnp.int32)
    pos = row_off[idx] + rank[:, 0:2].astype(jnp.int32)
    tile_ids = jnp.arange(n_tiles_max, dtype=jnp.int32)
    tile_expert = jnp.sum(tile_ids[:, None] >= tile_end[None, :], axis=1).astype(jnp.int32)
    tile_expert = jnp.minimum(tile_expert, N_EXPERTS - 1)
    n_tiles = tile_end[-1:].astype(jnp.int32)
    is_last = jnp.any((tile_ids[:, None] == tile_end[None, :] - 1) & (tiles[None, :] > 0), axis=1)
    zero_flag = (is_last | (tile_ids >= n_tiles[0])).astype(jnp.int32)
    return pos.reshape(-1), tile_expert, n_tiles, zero_flag


def kernel(x, mem, norm_mix_w, w_in, mlstm_conv_w, mlstm_conv_b, mlstm_b_i, mlstm_b_f, mlstm_norm_w,
           conf_conv_w, conf_conv_b, conf_ln_w, conf_ln_b, fox_b_f, w_out, norm_xattn_w, norm_mem_w,
           xattn_w_q, xattn_w_kv, xattn_w_o, norm_ffn_w, ffn_w_gate, ffn_w_up, ffn_w_down, router_w,
           moe_w_gate, moe_w_up, moe_w_down, norm_final_w):
    batch, seq, d = x.shape
    depth = w_in.shape[0]
    t = batch * seq
    xf = x.reshape(t, d)
    row = lambda a: a.reshape(1, -1).astype(F32)
    out = None
    for l in range(depth):
        w_main, w_gate = _split_w_in(w_in[l])
        z, gates = _in_proj(xf, row(norm_mix_w[l]), w_main, w_gate)
        hmc, c2 = _seq_mix(
            z, gates, _gate_bias(mlstm_b_i[l], mlstm_b_f[l], fox_b_f[l]),
            mlstm_conv_w[l].astype(F32), row(mlstm_conv_b[l]), row(mlstm_norm_w[l]),
            conf_conv_w[l].astype(F32), row(conf_conv_b[l]), row(conf_ln_w[l]), row(conf_ln_b[l]),
            batch)
        ka, qta, vta, fox_stats = _fox_prep(z, c2)
        visits, cref = _fox_tile_plan(fox_stats, batch)
        dense = l % 2 == 0
        j = l // 2
        if dense:
            ffn_f32 = (ffn_w_gate[j][None], ffn_w_up[j][None], ffn_w_down[j][None])
        else:
            ffn_f32 = (moe_w_gate[j], moe_w_up[j], moe_w_down[j])
        hf, (wg_b, wu_b, wd_b) = _fox_attn(ka, qta, vta, visits, cref, batch, ffn_f32)
        kv = _mem_kv(mem, row(norm_mem_w[l]), xattn_w_kv[l].astype(BF16))
        rw = None
        if not dense:
            rw = jnp.pad(router_w[j].astype(F32).T, ((0, ROUTER_ROWS - N_EXPERTS), (0, 0)))
        res = _post_mix(hmc, hf, xf, w_out[l].astype(BF16), row(norm_xattn_w[l]),
                        xattn_w_q[l].astype(BF16), kv, xattn_w_o[l].astype(BF16),
                        row(norm_ffn_w[l]), rw, batch)
        if dense:
            x2, h3 = res
            nt = t // min(TM_FFN, t)
            xf = _ffn(h3, wg_b, wu_b, wd_b, jnp.zeros((nt,), jnp.int32),
                      jnp.full((1,), nt, jnp.int32), residual=x2)
            if l == depth - 1:
                raise NotImplementedError("final norm after a dense layer")
        else:
            x2, h3, route = res
            rank, counts = _plan(route)
            tm = min(TM_FFN, t)
            n_tiles_max = (2 * t) // tm + N_EXPERTS
            pos, tile_expert, n_tiles, zero_flag = _route_tables(route, rank, counts, tm, n_tiles_max)
            xs = _dispatch(h3, pos, zero_flag, n_tiles_max * tm, tm)
            ys = _ffn(xs, wg_b, wu_b, wd_b, tile_expert, n_tiles)
            assert l == depth - 1
            out = _combine(x2, route, pos, ys, row(norm_final_w))
    return out.reshape(batch, seq, d)
```

```python
import functools

import jax
import jax.numpy as jnp
from jax import lax
from jax.experimental import pallas as pl
from jax.experimental.pallas import tpu as pltpu

F32 = jnp.float32
BF16 = jnp.bfloat16
EPS = 1e-6

MLSTM_HEADS = 4
HEAD_DIM = 64
MLSTM_WIDTH = MLSTM_HEADS * HEAD_DIM
MLSTM_CONV = 4
CONF_CHANNELS = 256
CONF_KERNEL = 31
FOX_HEADS = 8
FOX_WIDTH = FOX_HEADS * HEAD_DIM
XATTN_HEADS = 4
XATTN_HEAD_DIM = 128
XATTN_WIDTH = XATTN_HEADS * XATTN_HEAD_DIM
N_EXPERTS = 8
ROUTER_ROWS = 16
LANES = 128
SUBLANES = 8

Z_QK = 0
Z_V = 512
Z_O = 768
Z_GLU = 1024
Z_SEQ_WIDTH = 1536
Z_FQ = 1536
Z_FK = 2048
Z_FV = 2560
Z_WIDTH = 3072
G_I = 0
G_F = 4
G_FOX = 8

TM_PROJ = 512
SEQ_CHUNK = 256
CONV_TAIL = 8
CONF_TAIL = 32
FOX_TQ = 512
TM_FFN = 512
FF_CHUNKS = (1024, 1024, 768)
TD_ROUTE = 512
DMA_UNROLL = 64
VMEM_LIMIT = 56 * 1024 * 1024


def _cparams(sem, vmem=VMEM_LIMIT):
    return pltpu.CompilerParams(dimension_semantics=sem, vmem_limit_bytes=vmem)


def _sigmoid(x):
    return 1.0 / (1.0 + jnp.exp(-x))


def _log_sigmoid(x):
    return jnp.minimum(x, 0.0) - jnp.log(1.0 + jnp.exp(-jnp.abs(x)))


def _rmsnorm(x, w):
    ms = jnp.mean(x * x, axis=-1, keepdims=True)
    return x * lax.rsqrt(ms + EPS) * w


def _in_proj_kernel(x_ref, nw_ref, w_ref, wg_ref, z_ref, g_ref):
    h = _rmsnorm(x_ref[...], nw_ref[...]).astype(BF16)
    for n in range(0, Z_WIDTH, 512):
        z_ref[:, n:n + 512] = jnp.dot(h, w_ref[:, n:n + 512],
                                      preferred_element_type=F32).astype(BF16)
    g_ref[...] = jnp.dot(h, wg_ref[...], preferred_element_type=F32)


def _in_proj(x, norm_w, w_main, w_gate):
    t, d = x.shape
    tm = min(TM_PROJ, t)
    return pl.pallas_call(
        _in_proj_kernel,
        grid=(t // tm,),
        in_specs=[
            pl.BlockSpec((tm, d), lambda i: (i, 0)),
            pl.BlockSpec((1, d), lambda i: (0, 0)),
            pl.BlockSpec((d, Z_WIDTH), lambda i: (0, 0)),
            pl.BlockSpec((d, LANES), lambda i: (0, 0)),
        ],
        out_specs=[
            pl.BlockSpec((tm, Z_WIDTH), lambda i: (i, 0)),
            pl.BlockSpec((tm, LANES), lambda i: (i, 0)),
        ],
        out_shape=[jax.ShapeDtypeStruct((t, Z_WIDTH), BF16),
                   jax.ShapeDtypeStruct((t, LANES), F32)],
        compiler_params=_cparams(("parallel",)),
        name="in_proj",
    )(x, norm_w, w_main, w_gate)


def _seq_kernel(z_ref, g_ref, gb_ref, cw_ref, cb_ref, mnw_ref, ccw_ref, ccb_ref, lnw_ref, lnb_ref,
                out_ref, c_ref,
                qk_buf, u_buf, ush_buf, cstate, nstate, mstate, carry):
    L = z_ref.shape[0]
    W = MLSTM_WIDTH

    @pl.when(pl.program_id(1) == 0)
    def _():
        qk_buf[0:CONV_TAIL, :] = jnp.zeros((CONV_TAIL, 2 * W), F32)
        u_buf[0:CONF_TAIL, :] = jnp.zeros((CONF_TAIL, CONF_CHANNELS), F32)
        cstate[...] = jnp.zeros_like(cstate)
        nstate[...] = jnp.zeros_like(nstate)
        mstate[...] = jnp.zeros_like(mstate)
        carry[...] = jnp.zeros_like(carry)

    lane_g = lax.broadcasted_iota(jnp.int32, (1, LANES), 1)
    lane_w = lax.broadcasted_iota(jnp.int32, (1, W), 1)

    g = g_ref[...] + gb_ref[...]
    is_forget = (lane_g >= G_F) & (lane_g < G_FOX + FOX_HEADS)
    lsg = jnp.where(is_forget, _log_sigmoid(g), 0.0)
    row = lax.broadcasted_iota(jnp.int32, (L, L), 0)
    col = lax.broadcasted_iota(jnp.int32, (L, L), 1)
    causal = row >= col
    tri = causal.astype(F32)
    cs = jnp.dot(tri, lsg, preferred_element_type=F32, precision=lax.Precision.HIGHEST)
    c_all = cs + carry[...]
    c_ref[...] = c_all * 1.4426950408889634
    carry[...] = c_all[L - 1:L, :]

    qk_buf[CONV_TAIL:CONV_TAIL + L, :] = z_ref[:, Z_QK:Z_QK + 2 * W].astype(F32)
    qk = jnp.zeros((L, 2 * W), F32) + cb_ref[...]
    for j in range(MLSTM_CONV):
        off = CONV_TAIL - (MLSTM_CONV - 1) + j
        qk = qk + qk_buf[off:off + L, :] * cw_ref[j:j + 1, :]
    qk_buf[0:CONV_TAIL, :] = qk_buf[L:L + CONV_TAIL, :]
    qk = qk * _sigmoid(qk)
    q = qk[:, 0:W]
    k = qk[:, W:2 * W] * (HEAD_DIM ** -0.5)
    v = z_ref[:, Z_V:Z_V + W]
    q_b = q.astype(BF16)
    k_b = k.astype(BF16)

    cs_t = cs.T
    g_t = g.T
    m_prev = mstate[...]
    n_prev = nstate[...]
    c_prev = cstate[...]
    qn = q * n_prev
    q_c = jnp.dot(q_b, c_prev.astype(BF16), preferred_element_type=F32)

    num = jnp.zeros((L, W), F32)
    w_inter_l = jnp.zeros((L, W), F32)
    denom_l = jnp.ones((L, W), F32)
    wg_l = jnp.zeros((L, W), F32)
    decay_l = jnp.zeros((1, W), F32)
    m_new_row = m_prev
    for h in range(MLSTM_HEADS):
        hmask = (lane_w // HEAD_DIM) == h
        b_col = cs[:, G_F + h:G_F + h + 1]
        b_row = cs_t[G_F + h:G_F + h + 1, :]
        li_row = g_t[G_I + h:G_I + h + 1, :]
        li_col = g[:, G_I + h:G_I + h + 1]
        m_h = m_prev[:, h:h + 1]
        d_log = jnp.where(causal, b_col - b_row + li_row, -jnp.inf)
        inter = b_col + m_h
        m_t = jnp.maximum(jnp.max(d_log, axis=1, keepdims=True), inter)
        dmat = jnp.exp(d_log - m_t)
        q_h = jnp.where(hmask, q_b, jnp.zeros_like(q_b))
        s = lax.dot_general(q_h, k_b, (((1,), (1,)), ((), ())), preferred_element_type=F32) * dmat
        w_inter = jnp.exp(inter - m_t)
        pv = jnp.dot(s.astype(BF16), v, preferred_element_type=F32)
        num = jnp.where(hmask, pv, num)
        qn_h = jnp.sum(jnp.where(hmask, qn, 0.0), axis=1, keepdims=True)
        den = jnp.sum(s, axis=1, keepdims=True) + w_inter * qn_h
        dn = jnp.maximum(jnp.abs(den), jnp.exp(-m_t))
        w_inter_l = jnp.where(hmask, w_inter, w_inter_l)
        denom_l = jnp.where(hmask, dn, denom_l)
        b_tot = cs[L - 1:L, G_F + h:G_F + h + 1]
        g_h = b_tot - b_col + li_col
        m_new = jnp.maximum(b_tot + m_h, jnp.max(g_h, axis=0, keepdims=True))
        wg_l = jnp.where(hmask, jnp.exp(g_h - m_new), wg_l)
        decay_l = jnp.where(hmask, jnp.exp(b_tot + m_h - m_new), decay_l)
        m_new_row = jnp.where(lane_g == h, m_new, m_new_row)

    hout = (num + w_inter_l * q_c) / denom_l
    hsq = hout * hout
    rs_l = jnp.zeros((L, W), F32)
    for h in range(MLSTM_HEADS):
        hmask = (lane_w // HEAD_DIM) == h
        ms = jnp.sum(jnp.where(hmask, hsq, 0.0), axis=1, keepdims=True) * (1.0 / HEAD_DIM)
        rs_l = jnp.where(hmask, lax.rsqrt(ms + EPS), rs_l)
    o_gate = _sigmoid(z_ref[:, Z_O:Z_O + W].astype(F32))
    out_ref[:, 0:W] = (hout * rs_l * mnw_ref[...] * o_gate).astype(BF16)

    kw = k * wg_l
    upd = lax.dot_general(kw.astype(BF16), v, (((0,), (0,)), ((), ())), preferred_element_type=F32)
    rk = lax.broadcasted_iota(jnp.int32, (W, W), 0) // HEAD_DIM
    rv = lax.broadcasted_iota(jnp.int32, (W, W), 1) // HEAD_DIM
    cstate[...] = decay_l * c_prev + jnp.where(rk == rv, upd, 0.0)
    nstate[...] = decay_l * n_prev + jnp.sum(kw, axis=0, keepdims=True)
    mstate[...] = m_new_row

    a = z_ref[:, Z_GLU:Z_GLU + CONF_CHANNELS].astype(F32)
    gg = z_ref[:, Z_GLU + CONF_CHANNELS:Z_GLU + 2 * CONF_CHANNELS].astype(F32)
    u_buf[CONF_TAIL:CONF_TAIL + L, :] = a * _sigmoid(gg)
    span = L + CONF_TAIL - SUBLANES
    for r in range(1, SUBLANES):
        ush_buf[r - 1] = u_buf[r:r + span, :]
    hc = jnp.zeros((L, CONF_CHANNELS), F32) + ccb_ref[...]
    for j in range(CONF_KERNEL):
        off = CONF_TAIL - (CONF_KERNEL - 1) + j
        base, r = off - off % SUBLANES, off % SUBLANES
        src = u_buf[base:base + L, :] if r == 0 else ush_buf[r - 1, base:base + L, :]
        hc = hc + src * ccw_ref[j:j + 1, :]
    u_buf[0:CONF_TAIL, :] = u_buf[L:L + CONF_TAIL, :]
    mu = jnp.mean(hc, axis=1, keepdims=True)
    xc = hc - mu
    var = jnp.mean(xc * xc, axis=1, keepdims=True)
    y = xc * lax.rsqrt(var + EPS) * lnw_ref[...] + lnb_ref[...]
    out_ref[:, W:W + CONF_CHANNELS] = (y * _sigmoid(y)).astype(BF16)


def _seq_mix(z, gates, gate_bias, conv_w, conv_b, mnorm_w, cconv_w, cconv_b, ln_w, ln_b, batch):
    t = z.shape[0]
    s = t // batch
    L = min(SEQ_CHUNK, s)
    nc = s // L
    W = MLSTM_WIDTH
    full = lambda a: pl.BlockSpec(a.shape, lambda b, c: (0,) * a.ndim)
    return pl.pallas_call(
        _seq_kernel,
        grid=(batch, nc),
        in_specs=[
            pl.BlockSpec((L, Z_SEQ_WIDTH), lambda b, c: (b * nc + c, 0)),
            pl.BlockSpec((L, LANES), lambda b, c: (b * nc + c, 0)),
            full(gate_bias), full(conv_w), full(conv_b), full(mnorm_w),
            full(cconv_w), full(cconv_b), full(ln_w), full(ln_b),
        ],
        out_specs=[
            pl.BlockSpec((L, W + CONF_CHANNELS), lambda b, c: (b * nc + c, 0)),
            pl.BlockSpec((L, LANES), lambda b, c: (b * nc + c, 0)),
        ],
        out_shape=[jax.ShapeDtypeStruct((t, W + CONF_CHANNELS), BF16),
                   jax.ShapeDtypeStruct((t, LANES), F32)],
        scratch_shapes=[
            pltpu.VMEM((CONV_TAIL + L + CONV_TAIL, 2 * W), F32),
            pltpu.VMEM((CONF_TAIL + L + CONF_TAIL, CONF_CHANNELS), F32),
            pltpu.VMEM((SUBLANES - 1, L + CONF_TAIL - SUBLANES, CONF_CHANNELS), F32),
            pltpu.VMEM((W, W), F32),
            pltpu.VMEM((1, W), F32),
            pltpu.VMEM((1, LANES), F32),
            pltpu.VMEM((1, LANES), F32),
        ],
        compiler_params=_cparams(("arbitrary", "arbitrary")),
        name="seq_mix",
    )(z, gates, gate_bias, conv_w, conv_b, mnorm_w, cconv_w, cconv_b, ln_w, ln_b)


AUG_NEG_A = 0
AUG_ONE = 3
ST_ROWS = 8
FOX_KSUB = 1
ST_NORMS = 0
ST_FIRST = 1
ST_LAST = 2
FOX_SKIP_LOG2 = 160.0
NORM_SLACK = 1.01


def _split3(x):
    hi = x.astype(BF16).astype(F32)
    mid = (x - hi).astype(BF16).astype(F32)
    lo = (x - hi - mid).astype(BF16).astype(F32)
    return hi, mid, lo


def _fox_prep_kernel(zq_ref, zk_ref, zv_ref, c2_ref, ka_ref, qta_ref, vta_ref, st_ref):
    tp = zq_ref.shape[0]
    c2 = c2_ref[...]
    a = c2 - c2[0:1, :]
    c2_t = c2.T
    b_t = c2_t - c2_t[:, 0:1]
    lane = lax.broadcasted_iota(jnp.int32, (1, LANES), 1)
    row = lax.broadcasted_iota(jnp.int32, (LANES, 1), 0)
    q_scale = (HEAD_DIM ** -0.5) * 1.4426950408889634
    norms = jnp.zeros((1, LANES), F32)
    for p in range(FOX_HEADS // 2):
        sl = slice(p * LANES, (p + 1) * LANES)
        kp = zk_ref[:, sl].astype(F32)
        q_r = (zq_ref[:, sl].astype(F32) * q_scale).astype(BF16).astype(F32)
        q_t = q_r.T
        v_t = zv_ref[:, sl].astype(F32).T
        sq = jnp.concatenate([q_r * q_r, kp * kp], axis=1).astype(BF16)
        grp = lax.broadcasted_iota(jnp.int32, (2 * LANES, LANES), 0) // HEAD_DIM
        tks = tp // FOX_KSUB
        for r in range(FOX_KSUB):
            dst = jnp.where(grp < 2, 2 * p + grp, (1 + r) * FOX_HEADS + 2 * p + grp - 2)
            sel = (lax.broadcasted_iota(jnp.int32, (2 * LANES, LANES), 1) == dst).astype(BF16)
            n2 = jnp.max(jnp.dot(sq[r * tks:(r + 1) * tks], sel, preferred_element_type=F32),
                         axis=0, keepdims=True)
            norms = jnp.maximum(norms, jnp.sqrt(n2) * NORM_SLACK)
        for hh in range(2):
            h = 2 * p + hh
            own_lo = hh * HEAD_DIM
            o = (1 - hh) * HEAD_DIM
            a_hi, a_mid, a_lo = _split3(a[:, G_FOX + h:G_FOX + h + 1])
            b_hi, b_mid, b_lo = _split3(b_t[G_FOX + h:G_FOX + h + 1, :])
            own_lane = (lane >= own_lo) & (lane < own_lo + HEAD_DIM)
            own_row = (row >= own_lo) & (row < own_lo + HEAD_DIM)
            ones_l = ((lane >= o + AUG_ONE) & (lane < o + AUG_ONE + 3)).astype(F32)
            ka = jnp.where(own_lane, kp, ones_l)
            ka = jnp.where(lane == o + AUG_NEG_A, -a_hi, ka)
            ka = jnp.where(lane == o + AUG_NEG_A + 1, -a_mid, ka)
            ka = jnp.where(lane == o + AUG_NEG_A + 2, -a_lo, ka)
            ones_r = ((row >= o + AUG_NEG_A) & (row < o + AUG_NEG_A + 3)).astype(F32)
            qa = jnp.where(own_row, q_t, ones_r)
            qa = jnp.where(row == o + AUG_ONE, b_hi, qa)
            qa = jnp.where(row == o + AUG_ONE + 1, b_mid, qa)
            qa = jnp.where(row == o + AUG_ONE + 2, b_lo, qa)
            va = jnp.where(own_row, v_t, (row == o).astype(F32))
            hs = slice(h * LANES, (h + 1) * LANES)
            ka_ref[:, hs] = ka.astype(BF16)
            qta_ref[hs, :] = qa.astype(BF16)
            vta_ref[hs, :] = va.astype(BF16)
    st_ref[ST_NORMS:ST_NORMS + 1, :] = norms
    st_ref[ST_FIRST:ST_FIRST + 1, :] = c2[0:1, :]
    tks = tp // FOX_KSUB
    for r in range(FOX_KSUB):
        st_ref[ST_LAST + r:ST_LAST + r + 1, :] = c2[(r + 1) * tks - 1:(r + 1) * tks, :]
    st_ref[ST_LAST + FOX_KSUB:, :] = jnp.zeros((ST_ROWS - ST_LAST - FOX_KSUB, LANES), F32)


def _fox_prep(z, c2):
    t = z.shape[0]
    tp = min(FOX_TQ, t)
    width = FOX_HEADS * LANES
    return pl.pallas_call(
        _fox_prep_kernel,
        grid=(t // tp,),
        in_specs=[
            pl.BlockSpec((tp, FOX_WIDTH), lambda i: (i, Z_FQ // FOX_WIDTH)),
            pl.BlockSpec((tp, FOX_WIDTH), lambda i: (i, Z_FK // FOX_WIDTH)),
            pl.BlockSpec((tp, FOX_WIDTH), lambda i: (i, Z_FV // FOX_WIDTH)),
            pl.BlockSpec((tp, LANES), lambda i: (i, 0)),
        ],
        out_specs=[
            pl.BlockSpec((tp, width), lambda i: (i, 0)),
            pl.BlockSpec((width, tp), lambda i: (0, i)),
            pl.BlockSpec((width, tp), lambda i: (0, i)),
            pl.BlockSpec((ST_ROWS, LANES), lambda i: (i, 0)),
        ],
        out_shape=[jax.ShapeDtypeStruct((t, width), BF16),
                   jax.ShapeDtypeStruct((width, t), BF16),
                   jax.ShapeDtypeStruct((width, t), BF16),
                   jax.ShapeDtypeStruct((t // tp * ST_ROWS, LANES), F32)],
        compiler_params=_cparams(("parallel",)),
        name="fox_prep",
    )(z, z, z, c2)


def _fox_tile_plan(stats, batch):
    st = stats.reshape(batch, -1, ST_ROWS, LANES)
    nq = st.shape[1]
    nk = nq * FOX_KSUB
    qmax = st[:, :, ST_NORMS, 0:FOX_HEADS]
    kmax = st[:, :, ST_NORMS, FOX_HEADS:(1 + FOX_KSUB) * FOX_HEADS]
    kmax = kmax.reshape(batch, nk, FOX_HEADS)
    kdiag = jnp.max(kmax.reshape(batch, nq, FOX_KSUB, FOX_HEADS), axis=2)
    first = st[:, :, ST_FIRST, G_FOX:G_FOX + FOX_HEADS]
    last = st[:, :, ST_LAST:ST_LAST + FOX_KSUB, G_FOX:G_FOX + FOX_HEADS].reshape(batch, nk, FOX_HEADS)
    ub = (qmax[:, :, None, :] * (kmax[:, None, :, :] + kdiag[:, :, None, :])
          + first[:, :, None, :] - last[:, None, :, :])
    ti = jnp.arange(nq)[:, None]
    tj = jnp.arange(nk)[None, :]
    diag_lo = FOX_KSUB * ti
    need = (tj < diag_lo) & ~(ub < -FOX_SKIP_LOG2).transpose(0, 3, 1, 2)
    jmin = jnp.min(jnp.where(need, tj, diag_lo), axis=-1)
    count = FOX_KSUB * (jnp.arange(nq)[None, None, :] + 1) - jmin
    count = jnp.max(count.reshape(batch, FOX_HEADS // 2, 2, nq), axis=2)
    cref = first.transpose(0, 2, 1).reshape(-1)
    return count.reshape(-1).astype(jnp.int32), cref


def _fox_kernel(n_cast, cnt_ref, cref_ref, qta_ref, ka_ref, vta_ref, *rest):
    cast_in = rest[:n_cast]
    o_ref = rest[n_cast]
    cast_out = rest[n_cast + 1:2 * n_cast + 1]
    m_sc, acc_sc, s0_sc, cm0_sc, s1_sc, cm1_sc = rest[2 * n_cast + 1:]
    for src_ref, dst_ref in zip(cast_in, cast_out):
        dst_ref[...] = src_ref[...].astype(BF16)
    tq = qta_ref.shape[1]
    tk = tq // FOX_KSUB
    b = pl.program_id(0)
    p = pl.program_id(1)
    i = pl.program_id(2)
    nq = pl.num_programs(2)
    for hh in range(2):
        m_sc[hh] = jnp.full((1, tq), -jnp.inf, F32)
        acc_sc[hh] = jnp.zeros((LANES, tq), F32)

    slots = ((s0_sc, cm0_sc), (s1_sc, cm1_sc))
    diag0 = FOX_KSUB * i

    def sub_tile(v):
        return jnp.where(v < FOX_KSUB, diag0 + v, diag0 + FOX_KSUB - 1 - v)

    def scores(v, diag_sub, slot):
        s_sc, cm_sc = slots[slot]
        start = pl.multiple_of(sub_tile(v) * tk, tk)
        for hh in range(2):
            hs = slice(hh * LANES, (hh + 1) * LANES)
            s = jnp.dot(ka_ref[pl.ds(start, tk), hs], qta_ref[hs, :],
                        preferred_element_type=F32)
            if diag_sub is not None:
                kr = lax.broadcasted_iota(jnp.int32, (tk, tq), 0) + diag_sub * tk
                qc = lax.broadcasted_iota(jnp.int32, (tk, tq), 1)
                s = jnp.where(kr <= qc, s, -jnp.inf)
            s_sc[hh] = s
            cm_sc[hh] = jnp.max(s, axis=0, keepdims=True)

    def softmax_pv(v, slot):
        s_sc, cm_sc = slots[slot]
        j = sub_tile(v)
        start = pl.multiple_of(j * tk, tk)
        for hh in range(2):
            hs = slice(hh * LANES, (hh + 1) * LANES)
            base = (b * FOX_HEADS + 2 * p + hh) * nq
            delta = cref_ref[base + i] - cref_ref[base + j // FOX_KSUB]
            m_old = m_sc[hh]
            m_new = jnp.maximum(m_old, cm_sc[hh] + delta)
            alpha = jnp.exp2(m_old - m_new)
            pm = jnp.exp2(s_sc[hh] - (m_new - delta)).astype(BF16)
            acc_sc[hh] = alpha * acc_sc[hh] + jnp.dot(vta_ref[hs, pl.ds(start, tk)], pm,
                                                      preferred_element_type=F32)
            m_sc[hh] = m_new

    n_visits = cnt_ref[(b * pl.num_programs(1) + p) * nq + i]
    scores(0, 0, 0)
    for v in range(FOX_KSUB - 1):
        scores(v + 1, v + 1, (v + 1) % 2)
        softmax_pv(v, v % 2)

    def body(v, carry):
        for parity in range(2):
            @pl.when(v % 2 == parity)
            def _():
                scores(v + 1, None, 1 - parity)
                softmax_pv(v, parity)
        return carry

    lax.fori_loop(FOX_KSUB - 1, n_visits - 1, body, 0)
    for parity in range(2):
        @pl.when((n_visits - 1) % 2 == parity)
        def _():
            softmax_pv(n_visits - 1, parity)
    acc0 = acc_sc[0]
    acc1 = acc_sc[1]
    row = lax.broadcasted_iota(jnp.int32, (LANES, 1), 0)
    o_t = jnp.where(row < HEAD_DIM, acc0 / acc0[HEAD_DIM:HEAD_DIM + 1, :], acc1 / acc1[0:1, :])
    o_ref[...] = o_t.T.astype(BF16)


def _row_block_cast_jobs(weights, n_steps):
    jobs = []
    for w in weights:
        lead = w.shape[:-2]
        n_lead = 1
        for v in lead:
            n_lead *= v
        rows, cols = w.shape[-2:]
        per_lead = max(n_steps // n_lead, 1)
        blk = rows
        for cand in range(2 * SUBLANES, rows + 1, 2 * SUBLANES):
            if rows % cand == 0 and rows // cand <= per_lead:
                blk = cand
                break
        nblk = rows // blk
        total = n_lead * nblk

        def index(step, nblk=nblk, total=total, lead=lead):
            k = jnp.minimum(step, total - 1)
            idx = [k % nblk, 0]
            k = k // nblk
            for v in reversed(lead):
                idx.insert(0, k % v)
                k = k // v
            return tuple(idx)

        jobs.append((w, (1,) * len(lead) + (blk, cols), index))
    return jobs


def _fox_attn(ka, qta, vta, visits, cref, batch, cast_weights=()):
    t = ka.shape[0]
    s = t // batch
    tq = min(FOX_TQ, s)
    nq = s // tq
    npairs = FOX_HEADS // 2
    jobs = _row_block_cast_jobs(cast_weights, batch * npairs * nq)
    step_of = lambda b, p, i: (b * npairs + p) * nq + i
    cast_specs = [pl.BlockSpec(blk, lambda b, p, i, nv, cr, f=f: f(step_of(b, p, i)))
                  for _, blk, f in jobs]
    outs = pl.pallas_call(
        functools.partial(_fox_kernel, len(jobs)),
        grid_spec=pltpu.PrefetchScalarGridSpec(
            num_scalar_prefetch=2,
            grid=(batch, npairs, nq),
            in_specs=[
                pl.BlockSpec((2 * LANES, tq), lambda b, p, i, nv, cr: (p, b * nq + i)),
                pl.BlockSpec((s, 2 * LANES), lambda b, p, i, nv, cr: (b, p)),
                pl.BlockSpec((2 * LANES, s), lambda b, p, i, nv, cr: (p, b)),
            ] + cast_specs,
            out_specs=[pl.BlockSpec((tq, LANES), lambda b, p, i, nv, cr: (b * nq + i, p))]
            + cast_specs,
            scratch_shapes=[
                pltpu.VMEM((2, 1, tq), F32),
                pltpu.VMEM((2, LANES, tq), F32),
                pltpu.VMEM((2, tq // FOX_KSUB, tq), F32),
                pltpu.VMEM((2, 1, tq), F32),
                pltpu.VMEM((2, tq // FOX_KSUB, tq), F32),
                pltpu.VMEM((2, 1, tq), F32),
            ],
        ),
        out_shape=[jax.ShapeDtypeStruct((t, FOX_WIDTH), BF16)]
        + [jax.ShapeDtypeStruct(w.shape, BF16) for w, _, _ in jobs],
        compiler_params=_cparams(("arbitrary", "arbitrary", "arbitrary")),
        name="fox_attn",
    )(visits, cref, qta, ka, vta, *[w for w, _, _ in jobs])
    return outs[0], outs[1:]


def _mem_kv_kernel(mem_ref, nw_ref, w_ref, o_ref):
    h = _rmsnorm(mem_ref[0], nw_ref[...]).astype(BF16)
    o_ref[0] = jnp.dot(h, w_ref[...], preferred_element_type=F32).astype(BF16)


def _mem_kv(mem, norm_w, w_kv):
    b, m, d = mem.shape
    return pl.pallas_call(
        _mem_kv_kernel,
        grid=(b,),
        in_specs=[
            pl.BlockSpec((1, m, d), lambda i: (i, 0, 0)),
            pl.BlockSpec((1, d), lambda i: (0, 0)),
            pl.BlockSpec(w_kv.shape, lambda i: (0, 0)),
        ],
        out_specs=pl.BlockSpec((1, m, w_kv.shape[1]), lambda i: (i, 0, 0)),
        out_shape=jax.ShapeDtypeStruct((b, m, w_kv.shape[1]), BF16),
        compiler_params=_cparams(("parallel",)),
        name="mem_kv",
    )(mem, norm_w, w_kv)


def _post_mix_kernel(with_router, hmc_ref, hf_ref, x_ref, wout_ref, nxw_ref, wq_ref, kv_ref, wo_ref,
                     nfw_ref, *rest):
    if with_router:
        rw_ref, x2_ref, h3_ref, route_ref = rest
    else:
        x2_ref, h3_ref = rest
    half = MLSTM_WIDTH + CONF_CHANNELS
    x1 = (x_ref[...]
          + jnp.dot(hmc_ref[...], wout_ref[0:half, :], preferred_element_type=F32)
          + jnp.dot(hf_ref[...], wout_ref[half:, :], preferred_element_type=F32))
    h2 = _rmsnorm(x1, nxw_ref[...]).astype(BF16)
    q = jnp.dot(h2, wq_ref[...], preferred_element_type=F32) * (XATTN_HEAD_DIM ** -0.5)
    q = q.astype(BF16)
    kv = kv_ref[0]
    outs = []
    for h in range(XATTN_HEADS):
        lo = h * XATTN_HEAD_DIM
        kh = kv[:, lo:lo + XATTN_HEAD_DIM]
        vh = kv[:, XATTN_WIDTH + lo:XATTN_WIDTH + lo + XATTN_HEAD_DIM]
        s = lax.dot_general(q[:, lo:lo + XATTN_HEAD_DIM], kh, (((1,), (1,)), ((), ())),
                            preferred_element_type=F32)
        s = s - jnp.max(s, axis=1, keepdims=True)
        e = jnp.exp(s)
        pm = e / jnp.sum(e, axis=1, keepdims=True)
        outs.append(jnp.dot(pm.astype(BF16), vh, preferred_element_type=F32).astype(BF16))
    o = jnp.concatenate(outs, axis=1)
    x2 = x1 + jnp.dot(o, wo_ref[...], preferred_element_type=F32)
    x2_ref[...] = x2
    h3 = _rmsnorm(x2, nfw_ref[...])
    h3_ref[...] = h3.astype(h3_ref.dtype)
    if with_router:
        h_hi = h3.astype(BF16)
        h_lo = (h3 - h_hi.astype(F32)).astype(BF16)
        rw = rw_ref[...]
        w_hi = rw.astype(BF16)
        w_lo = (rw - w_hi.astype(F32)).astype(BF16)
        nt_dims = (((1,), (1,)), ((), ()))
        on_hi = lax.dot_general(jnp.concatenate([w_hi, w_lo], axis=0), h_hi, nt_dims,
                                preferred_element_type=F32)
        on_lo = lax.dot_general(w_hi, h_lo, nt_dims, preferred_element_type=F32)
        logits_t = on_hi[0:ROUTER_ROWS] + on_hi[ROUTER_ROWS:] + on_lo
        le = logits_t[0:N_EXPERTS, :]
        eid = lax.broadcasted_iota(jnp.int32, (N_EXPERTS, 1), 0).astype(F32)
        m1 = jnp.max(le, axis=0, keepdims=True)
        i1 = jnp.min(jnp.where(le == m1, eid, float(N_EXPERTS)), axis=0, keepdims=True)
        rest_l = jnp.where(eid == i1, -jnp.inf, le)
        m2 = jnp.max(rest_l, axis=0, keepdims=True)
        i2 = jnp.min(jnp.where(rest_l == m2, eid, float(N_EXPERTS)), axis=0, keepdims=True)
        e2 = jnp.exp(m2 - m1)
        w1 = 1.0 / (1.0 + e2)
        w2 = e2 * w1
        r8 = jnp.where(eid == 0, i1, jnp.where(eid == 1, i2, jnp.where(eid == 2, w1,
                       jnp.where(eid == 3, w2, 0.0))))
        route_t = jnp.concatenate([r8, jnp.zeros((LANES - N_EXPERTS, r8.shape[1]), F32)], axis=0)
        route_ref[...] = route_t.T


def _post_mix(hmc, hf, x, w_out, nx_w, w_q, kv, w_o, nf_w, router_w, batch):
    t, d = x.shape
    tm = min(TM_PROJ, t // batch)
    per_b = (t // batch) // tm
    with_router = router_w is not None
    const = lambda a: pl.BlockSpec(a.shape, lambda i: (0,) * a.ndim)
    in_specs = [
        pl.BlockSpec((tm, hmc.shape[1]), lambda i: (i, 0)),
        pl.BlockSpec((tm, hf.shape[1]), lambda i: (i, 0)),
        pl.BlockSpec((tm, d), lambda i: (i, 0)),
        const(w_out), const(nx_w), const(w_q),
        pl.BlockSpec((1,) + kv.shape[1:], lambda i: (i // per_b, 0, 0)),
        const(w_o), const(nf_w),
    ]
    args = [hmc, hf, x, w_out, nx_w, w_q, kv, w_o, nf_w]
    out_specs = [pl.BlockSpec((tm, d), lambda i: (i, 0)), pl.BlockSpec((tm, d), lambda i: (i, 0))]
    out_shape = [jax.ShapeDtypeStruct((t, d), F32),
                 jax.ShapeDtypeStruct((t, d), F32 if with_router else BF16)]
    if with_router:
        in_specs.append(const(router_w))
        args.append(router_w)
        out_specs.append(pl.BlockSpec((tm, LANES), lambda i: (i, 0)))
        out_shape.append(jax.ShapeDtypeStruct((t, LANES), F32))
    return pl.pallas_call(
        functools.partial(_post_mix_kernel, with_router),
        grid=(t // tm,),
        in_specs=in_specs, out_specs=out_specs, out_shape=out_shape,
        compiler_params=_cparams(("parallel",)),
        name="post_mix_router" if with_router else "post_mix",
    )(*args)


def _ffn_kernel(with_residual, te_ref, nt_ref, x_ref, wg_ref, wu_ref, wd_ref, *rest):
    if with_residual:
        res_ref, y_ref = rest
    else:
        (y_ref,) = rest

    @pl.when(pl.program_id(0) < nt_ref[0])
    def _():
        xb = x_ref[...].astype(BF16)
        acc = res_ref[...] if with_residual else None
        lo = 0
        for fc in FF_CHUNKS:
            a = jnp.dot(xb, wg_ref[0, :, lo:lo + fc], preferred_element_type=F32)
            u = jnp.dot(xb, wu_ref[0, :, lo:lo + fc], preferred_element_type=F32)
            hcur = (a * _sigmoid(a) * u).astype(BF16)
            part = jnp.dot(hcur, wd_ref[0, lo:lo + fc, :], preferred_element_type=F32)
            acc = part if acc is None else acc + part
            lo += fc
        y_ref[...] = acc

    @pl.when(pl.program_id(0) >= nt_ref[0])
    def _():
        y_ref[...] = jnp.zeros_like(y_ref)


def _ffn(x, w_gate, w_up, w_down, tile_expert, n_tiles, residual=None):
    rows, d = x.shape
    tm = min(TM_FFN, rows)
    nt = rows // tm
    ff = w_gate.shape[2]
    assert sum(FF_CHUNKS) == ff
    with_residual = residual is not None

    def row_map(i, te, ntl):
        return (jnp.minimum(i, ntl[0] - 1), 0)

    def w_map(i, te, ntl):
        return (te[jnp.minimum(i, ntl[0] - 1)], 0, 0)

    in_specs = [
        pl.BlockSpec((tm, d), row_map),
        pl.BlockSpec((1, d, ff), w_map),
        pl.BlockSpec((1, d, ff), w_map),
        pl.BlockSpec((1, ff, d), w_map),
    ]
    args = [x, w_gate, w_up, w_down]
    if with_residual:
        in_specs.append(pl.BlockSpec((tm, d), row_map))
        args.append(residual)
    return pl.pallas_call(
        functools.partial(_ffn_kernel, with_residual),
        grid_spec=pltpu.PrefetchScalarGridSpec(
            num_scalar_prefetch=2,
            grid=(nt,),
            in_specs=in_specs,
            out_specs=pl.BlockSpec((tm, d), lambda i, te, ntl: (i, 0)),
        ),
        out_shape=jax.ShapeDtypeStruct((rows, d), F32),
        compiler_params=_cparams(("arbitrary",)),
        name="ffn_dense" if with_residual else "ffn_experts",
    )(tile_expert, n_tiles, *args)


def _plan_kernel(route_ref, rank_ref, cnt_ref, carry):
    tp = route_ref.shape[0]

    @pl.when(pl.program_id(0) == 0)
    def _():
        carry[...] = jnp.zeros_like(carry)

    lane_i = lax.broadcasted_iota(jnp.int32, (1, LANES), 1)
    lane = lane_i.astype(F32)
    route = route_ref[...]
    i1 = route[:, 0:1]
    i2 = route[:, 1:2]
    onehot = (lane == i1).astype(F32) + (lane == i2).astype(F32)
    row = lax.broadcasted_iota(jnp.int32, (tp, tp), 0)
    col = lax.broadcasted_iota(jnp.int32, (tp, tp), 1)
    strict = (row > col).astype(BF16)
    before = jnp.dot(strict, onehot.astype(BF16), preferred_element_type=F32) + carry[...]
    r1 = jnp.sum(jnp.where(lane == i1, before, 0.0), axis=1, keepdims=True)
    r2 = jnp.sum(jnp.where(lane == i2, before, 0.0), axis=1, keepdims=True)
    rank_ref[...] = jnp.where(lane_i == 0, r1, jnp.where(lane_i == 1, r2, 0.0))
    total = carry[...] + jnp.sum(onehot, axis=0, keepdims=True)
    carry[...] = total
    cnt_ref[...] = total


def _plan(route):
    t = route.shape[0]
    tp = min(512, t)
    return pl.pallas_call(
        _plan_kernel,
        grid=(t // tp,),
        in_specs=[pl.BlockSpec((tp, LANES), lambda i: (i, 0))],
        out_specs=[pl.BlockSpec((tp, LANES), lambda i: (i, 0)),
                   pl.BlockSpec((1, LANES), lambda i: (0, 0))],
        out_shape=[jax.ShapeDtypeStruct((t, LANES), F32), jax.ShapeDtypeStruct((1, LANES), F32)],
        scratch_shapes=[pltpu.VMEM((1, LANES), F32)],
        compiler_params=_cparams(("arbitrary",)),
        name="route_plan",
    )(route)


def _row_copy(src, src_row, dst, dst_row, sem):
    return pltpu.make_async_copy(src.at[pl.ds(src_row, 1), :], dst.at[pl.ds(dst_row, 1), :], sem)


def _wait_rows(src, dst, sem, n):
    def wait(r, carry):
        _row_copy(src, 0, dst, 0, sem).wait()
        return carry
    lax.fori_loop(0, n, wait, 0, unroll=DMA_UNROLL)


def _dispatch_kernel(zflag_ref, pos_ref, h_ref, xs_ref, zero_buf, stage, sems, zsem):
    i = pl.program_id(0)
    nt = pl.num_programs(0)
    td = h_ref.shape[0]
    tm = zero_buf.shape[0]

    @pl.when(i == 0)
    def _():
        zero_buf[...] = jnp.zeros_like(zero_buf)

        def zero_copy(tile):
            row0 = pl.multiple_of(tile * tm, tm)
            return pltpu.make_async_copy(zero_buf, xs_ref.at[pl.ds(row0, tm), :], zsem)

        def zero_start(tile, carry):
            @pl.when(zflag_ref[tile] != 0)
            def _():
                zero_copy(tile).start()
            return carry

        def zero_wait(tile, carry):
            @pl.when(zflag_ref[tile] != 0)
            def _():
                zero_copy(tile).wait()
            return carry

        lax.fori_loop(0, zflag_ref.shape[0], zero_start, 0)
        lax.fori_loop(0, zflag_ref.shape[0], zero_wait, 0)

    def issue(slot):
        src = stage.at[slot]
        for r in range(td):
            _row_copy(src, r, xs_ref, pos_ref[0, 0, 2 * r], sems.at[slot]).start(priority=0)
            _row_copy(src, r, xs_ref, pos_ref[0, 0, 2 * r + 1], sems.at[slot]).start(priority=1)

    for parity in range(2):
        @pl.when(i % 2 == parity)
        def _():
            stage[parity] = h_ref[...]
            issue(parity)

            @pl.when(i > 0)
            def _():
                _wait_rows(stage.at[1 - parity], xs_ref, sems.at[1 - parity], 2 * td)

            @pl.when(i == nt - 1)
            def _():
                _wait_rows(stage.at[parity], xs_ref, sems.at[parity], 2 * td)


def _dispatch(h3, pos, zflag, rows, tm):
    t, d = h3.shape
    td = min(TD_ROUTE, t)
    nt = t // td
    pos3 = pos.reshape(nt, 1, 2 * td)
    return pl.pallas_call(
        _dispatch_kernel,
        grid_spec=pltpu.PrefetchScalarGridSpec(
            num_scalar_prefetch=1,
            grid=(nt,),
            in_specs=[
                pl.BlockSpec((1, 1, 2 * td), lambda i, zf: (i, 0, 0), memory_space=pltpu.SMEM),
                pl.BlockSpec((td, d), lambda i, zf: (i, 0)),
            ],
            out_specs=pl.BlockSpec(memory_space=pl.ANY),
            scratch_shapes=[pltpu.VMEM((tm, d), F32), pltpu.VMEM((2, td, d), F32),
                            pltpu.SemaphoreType.DMA((2,)), pltpu.SemaphoreType.DMA],
        ),
        out_shape=jax.ShapeDtypeStruct((rows, d), F32),
        compiler_params=_cparams(("arbitrary",)),
        name="dispatch",
    )(zflag, pos3, h3)


def _combine_kernel(pos_ref, pos_next_ref, x_ref, route_ref, nw_ref, ys_ref, o_ref, buf, sems):
    i = pl.program_id(0)
    nt = pl.num_programs(0)
    td = x_ref.shape[0]

    def issue(p_ref, slot):
        for r in range(td):
            _row_copy(ys_ref, p_ref[0, 0, 2 * r], buf.at[slot, 0], r, sems.at[slot]).start(
                priority=0)
            _row_copy(ys_ref, p_ref[0, 0, 2 * r + 1], buf.at[slot, 1], r, sems.at[slot]).start(
                priority=1)

    @pl.when(i == 0)
    def _():
        issue(pos_ref, 0)

    for parity in range(2):
        @pl.when(i % 2 == parity)
        def _():
            @pl.when(i + 1 < nt)
            def _():
                issue(pos_next_ref, 1 - parity)

            _wait_rows(ys_ref, buf.at[parity, 0], sems.at[parity], 2 * td)
            route = route_ref[...]
            x = x_ref[...] + route[:, 2:3] * buf[parity, 0] + route[:, 3:4] * buf[parity, 1]
            o_ref[...] = _rmsnorm(x, nw_ref[...])


def _combine(x2, route, pos, ys, norm_w):
    t, d = x2.shape
    td = min(TD_ROUTE, t)
    nt = t // td
    pos3 = pos.reshape(nt, 1, 2 * td)
    return pl.pallas_call(
        _combine_kernel,
        grid=(nt,),
        in_specs=[
            pl.BlockSpec((1, 1, 2 * td), lambda i: (i, 0, 0), memory_space=pltpu.SMEM),
            pl.BlockSpec((1, 1, 2 * td), lambda i: (jnp.minimum(i + 1, nt - 1), 0, 0),
                         memory_space=pltpu.SMEM),
            pl.BlockSpec((td, d), lambda i: (i, 0)),
            pl.BlockSpec((td, LANES), lambda i: (i, 0)),
            pl.BlockSpec((1, d), lambda i: (0, 0)),
            pl.BlockSpec(memory_space=pl.ANY),
        ],
        out_specs=pl.BlockSpec((td, d), lambda i: (i, 0)),
        out_shape=jax.ShapeDtypeStruct((t, d), F32),
        scratch_shapes=[pltpu.VMEM((2, 2, td, d), F32), pltpu.SemaphoreType.DMA((2,))],
        compiler_params=_cparams(("arbitrary",)),
        name="combine",
    )(pos3, pos3, x2, route, norm_w, ys)


def _split_w_in(w_in):
    mw, cw, fw = MLSTM_WIDTH, CONF_CHANNELS, FOX_WIDTH
    sizes = (2 * mw, mw, mw, MLSTM_HEADS, MLSTM_HEADS, 2 * cw, fw, fw, fw, FOX_HEADS)
    parts, off = [], 0
    for sz in sizes:
        parts.append(w_in[:, off:off + sz])
        off += sz
    m_qk, m_v, m_o, m_i, m_f, c_glu, f_q, f_k, f_v, f_f = parts
    w_main = jnp.concatenate([m_qk, m_v, m_o, c_glu, f_q, f_k, f_v], axis=1).astype(BF16)
    w_gate = jnp.concatenate([m_i, m_f, f_f], axis=1)
    w_gate = jnp.pad(w_gate, ((0, 0), (0, LANES - w_gate.shape[1]))).astype(BF16)
    return w_main, w_gate


def _gate_bias(b_i, b_f, fox_b):
    gb = jnp.concatenate([b_i, b_f, fox_b]).astype(F32)
    return jnp.pad(gb, (0, LANES - gb.shape[0])).reshape(1, LANES)


def _route_tables(route, rank, counts, tm, n_tiles_max):
    cnt = counts[0, :N_EXPERTS].astype(jnp.int32)
    tiles = (cnt + tm - 1) // tm
    tile_end = jnp.cumsum(tiles)
    row_off = (tile_end - tiles) * tm
    idx = route[:, 0:2].astype(jnp.int32)
    pos = row_off[idx] + rank[:, 0:2].astype(jnp.int32)
    tile_ids = jnp.arange(n_tiles_max, dtype=jnp.int32)
    tile_expert = jnp.sum(tile_ids[:, None] >= tile_end[None, :], axis=1).astype(jnp.int32)
    tile_expert = jnp.minimum(tile_expert, N_EXPERTS - 1)
    n_tiles = tile_end[-1:].astype(jnp.int32)
    is_last = jnp.any((tile_ids[:, None] == tile_end[None, :] - 1) & (tiles[None, :] > 0), axis=1)
    zero_flag = (is_last | (tile_ids >= n_tiles[0])).astype(jnp.int32)
    return pos.reshape(-1), tile_expert, n_tiles, zero_flag


def kernel(x, mem, norm_mix_w, w_in, mlstm_conv_w, mlstm_conv_b, mlstm_b_i, mlstm_b_f, mlstm_norm_w,
           conf_conv_w, conf_conv_b, conf_ln_w, conf_ln_b, fox_b_f, w_out, norm_xattn_w, norm_mem_w,
           xattn_w_q, xattn_w_kv, xattn_w_o, norm_ffn_w, ffn_w_gate, ffn_w_up, ffn_w_down, router_w,
           moe_w_gate, moe_w_up, moe_w_down, norm_final_w):
    batch, seq, d = x.shape
    depth = w_in.shape[0]
    t = batch * seq
    xf = x.reshape(t, d)
    row = lambda a: a.reshape(1, -1).astype(F32)
    out = None
    for l in range(depth):
        w_main, w_gate = _split_w_in(w_in[l])
        z, gates = _in_proj(xf, row(norm_mix_w[l]), w_main, w_gate)
        hmc, c2 = _seq_mix(
            z, gates, _gate_bias(mlstm_b_i[l], mlstm_b_f[l], fox_b_f[l]),
            mlstm_conv_w[l].astype(F32), row(mlstm_conv_b[l]), row(mlstm_norm_w[l]),
            conf_conv_w[l].astype(F32), row(conf_conv_b[l]), row(conf_ln_w[l]), row(conf_ln_b[l]),
            batch)
        ka, qta, vta, fox_stats = _fox_prep(z, c2)
        visits, cref = _fox_tile_plan(fox_stats, batch)
        dense = l % 2 == 0
        j = l // 2
        if dense:
            ffn_f32 = (ffn_w_gate[j][None], ffn_w_up[j][None], ffn_w_down[j][None])
        else:
            ffn_f32 = (moe_w_gate[j], moe_w_up[j], moe_w_down[j])
        hf, (wg_b, wu_b, wd_b) = _fox_attn(ka, qta, vta, visits, cref, batch, ffn_f32)
        kv = _mem_kv(mem, row(norm_mem_w[l]), xattn_w_kv[l].astype(BF16))
        rw = None
        if not dense:
            rw = jnp.pad(router_w[j].astype(F32).T, ((0, ROUTER_ROWS - N_EXPERTS), (0, 0)))
        res = _post_mix(hmc, hf, xf, w_out[l].astype(BF16), row(norm_xattn_w[l]),
                        xattn_w_q[l].astype(BF16), kv, xattn_w_o[l].astype(BF16),
                        row(norm_ffn_w[l]), rw, batch)
        if dense:
            x2, h3 = res
            nt = t // min(TM_FFN, t)
            xf = _ffn(h3, wg_b, wu_b, wd_b, jnp.zeros((nt,), jnp.int32),
                      jnp.full((1,), nt, jnp.int32), residual=x2)
            if l == depth - 1:
                raise NotImplementedError("final norm after a dense layer")
        else:
            x2, h3, route = res
            rank, counts = _plan(route)
            tm = min(TM_FFN, t)
            n_tiles_max = (2 * t) // tm + N_EXPERTS
            pos, tile_expert, n_tiles, zero_flag = _route_tables(route, rank, counts, tm, n_tiles_max)
            xs = _dispatch(h3, pos, zero_flag, n_tiles_max * tm, tm)
            ys = _ffn(xs, wg_b, wu_b, wd_b, tile_expert, n_tiles)
            assert l == depth - 1
            out = _combine(x2, route, pos, ys, row(norm_final_w))
    return out.reshape(batch, seq, d)
```

```python
import functools

import jax
import jax.numpy as jnp
from jax import lax
from jax.experimental import pallas as pl
from jax.experimental.pallas import tpu as pltpu

F32 = jnp.float32
BF16 = jnp.bfloat16
EPS = 1e-6

MLSTM_HEADS = 4
HEAD_DIM = 64
MLSTM_WIDTH = MLSTM_HEADS * HEAD_DIM
MLSTM_CONV = 4
CONF_CHANNELS = 256
CONF_KERNEL = 31
FOX_HEADS = 8
FOX_WIDTH = FOX_HEADS * HEAD_DIM
XATTN_HEADS = 4
XATTN_HEAD_DIM = 128
XATTN_WIDTH = XATTN_HEADS * XATTN_HEAD_DIM
N_EXPERTS = 8
ROUTER_ROWS = 16
LANES = 128
SUBLANES = 8

Z_QK = 0
Z_V = 512
Z_O = 768
Z_GLU = 1024
Z_SEQ_WIDTH = 1536
Z_FQ = 1536
Z_FK = 2048
Z_FV = 2560
Z_WIDTH = 3072
G_I = 0
G_F = 4
G_FOX = 8

TM_PROJ = 1024
SEQ_CHUNK = 256
CONV_TAIL = 8
CONF_TAIL = 32
FOX_TQ = 512
TM_FFN = 512
FF_CHUNKS = (1024, 1024, 768)
TD_ROUTE = 512
DMA_UNROLL = 64
VMEM_LIMIT = 56 * 1024 * 1024


def _cparams(sem, vmem=VMEM_LIMIT):
    return pltpu.CompilerParams(dimension_semantics=sem, vmem_limit_bytes=vmem)


def _sigmoid(x):
    return 1.0 / (1.0 + jnp.exp(-x))


def _log_sigmoid(x):
    return jnp.minimum(x, 0.0) - jnp.log(1.0 + jnp.exp(-jnp.abs(x)))


def _rmsnorm(x, w):
    ms = jnp.mean(x * x, axis=-1, keepdims=True)
    return x * lax.rsqrt(ms + EPS) * w


def _in_proj_kernel(x_ref, nw_ref, w_ref, wg_ref, z_ref, g_ref):
    h = _rmsnorm(x_ref[...], nw_ref[...]).astype(BF16)
    for n in range(0, Z_WIDTH, 512):
        z_ref[:, n:n + 512] = jnp.dot(h, w_ref[:, n:n + 512],
                                      preferred_element_type=F32).astype(BF16)
    g_ref[...] = jnp.dot(h, wg_ref[...], preferred_element_type=F32)


def _in_proj(x, norm_w, w_main, w_gate):
    t, d = x.shape
    tm = min(TM_PROJ, t)
    return pl.pallas_call(
        _in_proj_kernel,
        grid=(t // tm,),
        in_specs=[
            pl.BlockSpec((tm, d), lambda i: (i, 0)),
            pl.BlockSpec((1, d), lambda i: (0, 0)),
            pl.BlockSpec((d, Z_WIDTH), lambda i: (0, 0)),
            pl.BlockSpec((d, LANES), lambda i: (0, 0)),
        ],
        out_specs=[
            pl.BlockSpec((tm, Z_WIDTH), lambda i: (i, 0)),
            pl.BlockSpec((tm, LANES), lambda i: (i, 0)),
        ],
        out_shape=[jax.ShapeDtypeStruct((t, Z_WIDTH), BF16),
                   jax.ShapeDtypeStruct((t, LANES), F32)],
        compiler_params=_cparams(("parallel",)),
        name="in_proj",
    )(x, norm_w, w_main, w_gate)


def _seq_kernel(z_ref, g_ref, gb_ref, cw_ref, cb_ref, mnw_ref, ccw_ref, ccb_ref, lnw_ref, lnb_ref,
                out_ref, c_ref,
                qk_buf, u_buf, ush_buf, cstate, nstate, mstate, carry):
    L = z_ref.shape[0]
    W = MLSTM_WIDTH

    @pl.when(pl.program_id(1) == 0)
    def _():
        qk_buf[0:CONV_TAIL, :] = jnp.zeros((CONV_TAIL, 2 * W), F32)
        u_buf[0:CONF_TAIL, :] = jnp.zeros((CONF_TAIL, CONF_CHANNELS), F32)
        cstate[...] = jnp.zeros_like(cstate)
        nstate[...] = jnp.zeros_like(nstate)
        mstate[...] = jnp.zeros_like(mstate)
        carry[...] = jnp.zeros_like(carry)

    lane_g = lax.broadcasted_iota(jnp.int32, (1, LANES), 1)
    lane_w = lax.broadcasted_iota(jnp.int32, (1, W), 1)

    g = g_ref[...] + gb_ref[...]
    is_forget = (lane_g >= G_F) & (lane_g < G_FOX + FOX_HEADS)
    lsg = jnp.where(is_forget, _log_sigmoid(g), 0.0)
    row = lax.broadcasted_iota(jnp.int32, (L, L), 0)
    col = lax.broadcasted_iota(jnp.int32, (L, L), 1)
    causal = row >= col
    tri = causal.astype(F32)
    cs = jnp.dot(tri, lsg, preferred_element_type=F32, precision=lax.Precision.HIGHEST)
    c_all = cs + carry[...]
    c_ref[...] = c_all * 1.4426950408889634
    carry[...] = c_all[L - 1:L, :]

    qk_buf[CONV_TAIL:CONV_TAIL + L, :] = z_ref[:, Z_QK:Z_QK + 2 * W].astype(F32)
    qk = jnp.zeros((L, 2 * W), F32) + cb_ref[...]
    for j in range(MLSTM_CONV):
        off = CONV_TAIL - (MLSTM_CONV - 1) + j
        qk = qk + qk_buf[off:off + L, :] * cw_ref[j:j + 1, :]
    qk_buf[0:CONV_TAIL, :] = qk_buf[L:L + CONV_TAIL, :]
    qk = qk * _sigmoid(qk)
    q = qk[:, 0:W]
    k = qk[:, W:2 * W] * (HEAD_DIM ** -0.5)
    v = z_ref[:, Z_V:Z_V + W]
    q_b = q.astype(BF16)
    k_b = k.astype(BF16)

    cs_t = cs.T
    g_t = g.T
    m_prev = mstate[...]
    n_prev = nstate[...]
    c_prev = cstate[...]
    qn = q * n_prev
    q_c = jnp.dot(q_b, c_prev.astype(BF16), preferred_element_type=F32)

    num = jnp.zeros((L, W), F32)
    w_inter_l = jnp.zeros((L, W), F32)
    denom_l = jnp.ones((L, W), F32)
    wg_l = jnp.zeros((L, W), F32)
    decay_l = jnp.zeros((1, W), F32)
    m_new_row = m_prev
    for h in range(MLSTM_HEADS):
        hmask = (lane_w // HEAD_DIM) == h
        b_col = cs[:, G_F + h:G_F + h + 1]
        b_row = cs_t[G_F + h:G_F + h + 1, :]
        li_row = g_t[G_I + h:G_I + h + 1, :]
        li_col = g[:, G_I + h:G_I + h + 1]
        m_h = m_prev[:, h:h + 1]
        d_log = jnp.where(causal, b_col - b_row + li_row, -jnp.inf)
        inter = b_col + m_h
        m_t = jnp.maximum(jnp.max(d_log, axis=1, keepdims=True), inter)
        dmat = jnp.exp(d_log - m_t)
        q_h = jnp.where(hmask, q_b, jnp.zeros_like(q_b))
        s = lax.dot_general(q_h, k_b, (((1,), (1,)), ((), ())), preferred_element_type=F32) * dmat
        w_inter = jnp.exp(inter - m_t)
        pv = jnp.dot(s.astype(BF16), v, preferred_element_type=F32)
        num = jnp.where(hmask, pv, num)
        qn_h = jnp.sum(jnp.where(hmask, qn, 0.0), axis=1, keepdims=True)
        den = jnp.sum(s, axis=1, keepdims=True) + w_inter * qn_h
        dn = jnp.maximum(jnp.abs(den), jnp.exp(-m_t))
        w_inter_l = jnp.where(hmask, w_inter, w_inter_l)
        denom_l = jnp.where(hmask, dn, denom_l)
        b_tot = cs[L - 1:L, G_F + h:G_F + h + 1]
        g_h = b_tot - b_col + li_col
        m_new = jnp.maximum(b_tot + m_h, jnp.max(g_h, axis=0, keepdims=True))
        wg_l = jnp.where(hmask, jnp.exp(g_h - m_new), wg_l)
        decay_l = jnp.where(hmask, jnp.exp(b_tot + m_h - m_new), decay_l)
        m_new_row = jnp.where(lane_g == h, m_new, m_new_row)

    hout = (num + w_inter_l * q_c) / denom_l
    hsq = hout * hout
    rs_l = jnp.zeros((L, W), F32)
    for h in range(MLSTM_HEADS):
        hmask = (lane_w // HEAD_DIM) == h
        ms = jnp.sum(jnp.where(hmask, hsq, 0.0), axis=1, keepdims=True) * (1.0 / HEAD_DIM)
        rs_l = jnp.where(hmask, lax.rsqrt(ms + EPS), rs_l)
    o_gate = _sigmoid(z_ref[:, Z_O:Z_O + W].astype(F32))
    out_ref[:, 0:W] = (hout * rs_l * mnw_ref[...] * o_gate).astype(BF16)

    kw = k * wg_l
    upd = lax.dot_general(kw.astype(BF16), v, (((0,), (0,)), ((), ())), preferred_element_type=F32)
    rk = lax.broadcasted_iota(jnp.int32, (W, W), 0) // HEAD_DIM
    rv = lax.broadcasted_iota(jnp.int32, (W, W), 1) // HEAD_DIM
    cstate[...] = decay_l * c_prev + jnp.where(rk == rv, upd, 0.0)
    nstate[...] = decay_l * n_prev + jnp.sum(kw, axis=0, keepdims=True)
    mstate[...] = m_new_row

    a = z_ref[:, Z_GLU:Z_GLU + CONF_CHANNELS].astype(F32)
    gg = z_ref[:, Z_GLU + CONF_CHANNELS:Z_GLU + 2 * CONF_CHANNELS].astype(F32)
    u_buf[CONF_TAIL:CONF_TAIL + L, :] = a * _sigmoid(gg)
    span = L + CONF_TAIL - SUBLANES
    for r in range(1, SUBLANES):
        ush_buf[r - 1] = u_buf[r:r + span, :]
    hc = jnp.zeros((L, CONF_CHANNELS), F32) + ccb_ref[...]
    for j in range(CONF_KERNEL):
        off = CONF_TAIL - (CONF_KERNEL - 1) + j
        base, r = off - off % SUBLANES, off % SUBLANES
        src = u_buf[base:base + L, :] if r == 0 else ush_buf[r - 1, base:base + L, :]
        hc = hc + src * ccw_ref[j:j + 1, :]
    u_buf[0:CONF_TAIL, :] = u_buf[L:L + CONF_TAIL, :]
    mu = jnp.mean(hc, axis=1, keepdims=True)
    xc = hc - mu
    var = jnp.mean(xc * xc, axis=1, keepdims=True)
    y = xc * lax.rsqrt(var + EPS) * lnw_ref[...] + lnb_ref[...]
    out_ref[:, W:W + CONF_CHANNELS] = (y * _sigmoid(y)).astype(BF16)


def _seq_mix(z, gates, gate_bias, conv_w, conv_b, mnorm_w, cconv_w, cconv_b, ln_w, ln_b, batch):
    t = z.shape[0]
    s = t // batch
    L = min(SEQ_CHUNK, s)
    nc = s // L
    W = MLSTM_WIDTH
    full = lambda a: pl.BlockSpec(a.shape, lambda b, c: (0,) * a.ndim)
    return pl.pallas_call(
        _seq_kernel,
        grid=(batch, nc),
        in_specs=[
            pl.BlockSpec((L, Z_SEQ_WIDTH), lambda b, c: (b * nc + c, 0)),
            pl.BlockSpec((L, LANES), lambda b, c: (b * nc + c, 0)),
            full(gate_bias), full(conv_w), full(conv_b), full(mnorm_w),
            full(cconv_w), full(cconv_b), full(ln_w), full(ln_b),
        ],
        out_specs=[
            pl.BlockSpec((L, W + CONF_CHANNELS), lambda b, c: (b * nc + c, 0)),
            pl.BlockSpec((L, LANES), lambda b, c: (b * nc + c, 0)),
        ],
        out_shape=[jax.ShapeDtypeStruct((t, W + CONF_CHANNELS), BF16),
                   jax.ShapeDtypeStruct((t, LANES), F32)],
        scratch_shapes=[
            pltpu.VMEM((CONV_TAIL + L + CONV_TAIL, 2 * W), F32),
            pltpu.VMEM((CONF_TAIL + L + CONF_TAIL, CONF_CHANNELS), F32),
            pltpu.VMEM((SUBLANES - 1, L + CONF_TAIL - SUBLANES, CONF_CHANNELS), F32),
            pltpu.VMEM((W, W), F32),
            pltpu.VMEM((1, W), F32),
            pltpu.VMEM((1, LANES), F32),
            pltpu.VMEM((1, LANES), F32),
        ],
        compiler_params=_cparams(("arbitrary", "arbitrary")),
        name="seq_mix",
    )(z, gates, gate_bias, conv_w, conv_b, mnorm_w, cconv_w, cconv_b, ln_w, ln_b)


AUG_NEG_A = 0
AUG_ONE = 3
ST_ROWS = 8
FOX_KSUB = 1
ST_NORMS = 0
ST_FIRST = 1
ST_LAST = 2
FOX_SKIP_LOG2 = 160.0
NORM_SLACK = 1.01


def _split3(x):
    hi = x.astype(BF16).astype(F32)
    mid = (x - hi).astype(BF16).astype(F32)
    lo = (x - hi - mid).astype(BF16).astype(F32)
    return hi, mid, lo


def _fox_prep_kernel(zq_ref, zk_ref, zv_ref, c2_ref, ka_ref, qta_ref, vta_ref, st_ref):
    tp = zq_ref.shape[0]
    c2 = c2_ref[...]
    a = c2 - c2[0:1, :]
    c2_t = c2.T
    b_t = c2_t - c2_t[:, 0:1]
    lane = lax.broadcasted_iota(jnp.int32, (1, LANES), 1)
    row = lax.broadcasted_iota(jnp.int32, (LANES, 1), 0)
    q_scale = (HEAD_DIM ** -0.5) * 1.4426950408889634
    norms = jnp.zeros((1, LANES), F32)
    for p in range(FOX_HEADS // 2):
        sl = slice(p * LANES, (p + 1) * LANES)
        kp = zk_ref[:, sl].astype(F32)
        q_r = (zq_ref[:, sl].astype(F32) * q_scale).astype(BF16).astype(F32)
        q_t = q_r.T
        v_t = zv_ref[:, sl].astype(F32).T
        sq = jnp.concatenate([q_r * q_r, kp * kp], axis=1).astype(BF16)
        grp = lax.broadcasted_iota(jnp.int32, (2 * LANES, LANES), 0) // HEAD_DIM
        tks = tp // FOX_KSUB
        for r in range(FOX_KSUB):
            dst = jnp.where(grp < 2, 2 * p + grp, (1 + r) * FOX_HEADS + 2 * p + grp - 2)
            sel = (lax.broadcasted_iota(jnp.int32, (2 * LANES, LANES), 1) == dst).astype(BF16)
            n2 = jnp.max(jnp.dot(sq[r * tks:(r + 1) * tks], sel, preferred_element_type=F32),
                         axis=0, keepdims=True)
            norms = jnp.maximum(norms, jnp.sqrt(n2) * NORM_SLACK)
        for hh in range(2):
            h = 2 * p + hh
            own_lo = hh * HEAD_DIM
            o = (1 - hh) * HEAD_DIM
            a_hi, a_mid, a_lo = _split3(a[:, G_FOX + h:G_FOX + h + 1])
            b_hi, b_mid, b_lo = _split3(b_t[G_FOX + h:G_FOX + h + 1, :])
            own_lane = (lane >= own_lo) & (lane < own_lo + HEAD_DIM)
            own_row = (row >= own_lo) & (row < own_lo + HEAD_DIM)
            ones_l = ((lane >= o + AUG_ONE) & (lane < o + AUG_ONE + 3)).astype(F32)
            ka = jnp.where(own_lane, kp, ones_l)
            ka = jnp.where(lane == o + AUG_NEG_A, -a_hi, ka)
            ka = jnp.where(lane == o + AUG_NEG_A + 1, -a_mid, ka)
            ka = jnp.where(lane == o + AUG_NEG_A + 2, -a_lo, ka)
            ones_r = ((row >= o + AUG_NEG_A) & (row < o + AUG_NEG_A + 3)).astype(F32)
            qa = jnp.where(own_row, q_t, ones_r)
            qa = jnp.where(row == o + AUG_ONE, b_hi, qa)
            qa = jnp.where(row == o + AUG_ONE + 1, b_mid, qa)
            qa = jnp.where(row == o + AUG_ONE + 2, b_lo, qa)
            va = jnp.where(own_row, v_t, (row == o).astype(F32))
            hs = slice(h * LANES, (h + 1) * LANES)
            ka_ref[:, hs] = ka.astype(BF16)
            qta_ref[hs, :] = qa.astype(BF16)
            vta_ref[hs, :] = va.astype(BF16)
    st_ref[ST_NORMS:ST_NORMS + 1, :] = norms
    st_ref[ST_FIRST:ST_FIRST + 1, :] = c2[0:1, :]
    tks = tp // FOX_KSUB
    for r in range(FOX_KSUB):
        st_ref[ST_LAST + r:ST_LAST + r + 1, :] = c2[(r + 1) * tks - 1:(r + 1) * tks, :]
    st_ref[ST_LAST + FOX_KSUB:, :] = jnp.zeros((ST_ROWS - ST_LAST - FOX_KSUB, LANES), F32)


def _fox_prep(z, c2):
    t = z.shape[0]
    tp = min(FOX_TQ, t)
    width = FOX_HEADS * LANES
    return pl.pallas_call(
        _fox_prep_kernel,
        grid=(t // tp,),
        in_specs=[
            pl.BlockSpec((tp, FOX_WIDTH), lambda i: (i, Z_FQ // FOX_WIDTH)),
            pl.BlockSpec((tp, FOX_WIDTH), lambda i: (i, Z_FK // FOX_WIDTH)),
            pl.BlockSpec((tp, FOX_WIDTH), lambda i: (i, Z_FV // FOX_WIDTH)),
            pl.BlockSpec((tp, LANES), lambda i: (i, 0)),
        ],
        out_specs=[
            pl.BlockSpec((tp, width), lambda i: (i, 0)),
            pl.BlockSpec((width, tp), lambda i: (0, i)),
            pl.BlockSpec((width, tp), lambda i: (0, i)),
            pl.BlockSpec((ST_ROWS, LANES), lambda i: (i, 0)),
        ],
        out_shape=[jax.ShapeDtypeStruct((t, width), BF16),
                   jax.ShapeDtypeStruct((width, t), BF16),
                   jax.ShapeDtypeStruct((width, t), BF16),
                   jax.ShapeDtypeStruct((t // tp * ST_ROWS, LANES), F32)],
        compiler_params=_cparams(("parallel",)),
        name="fox_prep",
    )(z, z, z, c2)


def _fox_tile_plan(stats, batch):
    st = stats.reshape(batch, -1, ST_ROWS, LANES)
    nq = st.shape[1]
    nk = nq * FOX_KSUB
    qmax = st[:, :, ST_NORMS, 0:FOX_HEADS]
    kmax = st[:, :, ST_NORMS, FOX_HEADS:(1 + FOX_KSUB) * FOX_HEADS]
    kmax = kmax.reshape(batch, nk, FOX_HEADS)
    kdiag = jnp.max(kmax.reshape(batch, nq, FOX_KSUB, FOX_HEADS), axis=2)
    first = st[:, :, ST_FIRST, G_FOX:G_FOX + FOX_HEADS]
    last = st[:, :, ST_LAST:ST_LAST + FOX_KSUB, G_FOX:G_FOX + FOX_HEADS].reshape(batch, nk, FOX_HEADS)
    ub = (qmax[:, :, None, :] * (kmax[:, None, :, :] + kdiag[:, :, None, :])
          + first[:, :, None, :] - last[:, None, :, :])
    ti = jnp.arange(nq)[:, None]
    tj = jnp.arange(nk)[None, :]
    diag_lo = FOX_KSUB * ti
    need = (tj < diag_lo) & ~(ub < -FOX_SKIP_LOG2).transpose(0, 3, 1, 2)
    jmin = jnp.min(jnp.where(need, tj, diag_lo), axis=-1)
    count = FOX_KSUB * (jnp.arange(nq)[None, None, :] + 1) - jmin
    count = jnp.max(count.reshape(batch, FOX_HEADS // 2, 2, nq), axis=2)
    cref = first.transpose(0, 2, 1).reshape(-1)
    return count.reshape(-1).astype(jnp.int32), cref


def _fox_kernel(n_cast, cnt_ref, cref_ref, qta_ref, ka_ref, vta_ref, *rest):
    cast_in = rest[:n_cast]
    o_ref = rest[n_cast]
    cast_out = rest[n_cast + 1:2 * n_cast + 1]
    m_sc, acc_sc, s0_sc, cm0_sc, s1_sc, cm1_sc = rest[2 * n_cast + 1:]
    for src_ref, dst_ref in zip(cast_in, cast_out):
        dst_ref[...] = src_ref[...].astype(BF16)
    tq = qta_ref.shape[1]
    tk = tq // FOX_KSUB
    b = pl.program_id(0)
    p = pl.program_id(1)
    i = pl.program_id(2)
    nq = pl.num_programs(2)
    for hh in range(2):
        m_sc[hh] = jnp.full((1, tq), -jnp.inf, F32)
        acc_sc[hh] = jnp.zeros((LANES, tq), F32)

    slots = ((s0_sc, cm0_sc), (s1_sc, cm1_sc))
    diag0 = FOX_KSUB * i

    def sub_tile(v):
        return jnp.where(v < FOX_KSUB, diag0 + v, diag0 + FOX_KSUB - 1 - v)

    def scores(v, diag_sub, slot):
        s_sc, cm_sc = slots[slot]
        start = pl.multiple_of(sub_tile(v) * tk, tk)
        for hh in range(2):
            hs = slice(hh * LANES, (hh + 1) * LANES)
            s = jnp.dot(ka_ref[pl.ds(start, tk), hs], qta_ref[hs, :],
                        preferred_element_type=F32)
            if diag_sub is not None:
                kr = lax.broadcasted_iota(jnp.int32, (tk, tq), 0) + diag_sub * tk
                qc = lax.broadcasted_iota(jnp.int32, (tk, tq), 1)
                s = jnp.where(kr <= qc, s, -jnp.inf)
            s_sc[hh] = s
            cm_sc[hh] = jnp.max(s, axis=0, keepdims=True)

    def softmax_pv(v, slot):
        s_sc, cm_sc = slots[slot]
        j = sub_tile(v)
        start = pl.multiple_of(j * tk, tk)
        for hh in range(2):
            hs = slice(hh * LANES, (hh + 1) * LANES)
            base = (b * FOX_HEADS + 2 * p + hh) * nq
            delta = cref_ref[base + i] - cref_ref[base + j // FOX_KSUB]
            m_old = m_sc[hh]
            m_new = jnp.maximum(m_old, cm_sc[hh] + delta)
            alpha = jnp.exp2(m_old - m_new)
            pm = jnp.exp2(s_sc[hh] - (m_new - delta)).astype(BF16)
            acc_sc[hh] = alpha * acc_sc[hh] + jnp.dot(vta_ref[hs, pl.ds(start, tk)], pm,
                                                      preferred_element_type=F32)
            m_sc[hh] = m_new

    n_visits = cnt_ref[(b * pl.num_programs(1) + p) * nq + i]
    scores(0, 0, 0)
    for v in range(FOX_KSUB - 1):
        scores(v + 1, v + 1, (v + 1) % 2)
        softmax_pv(v, v % 2)

    def body(v, carry):
        for parity in range(2):
            @pl.when(v % 2 == parity)
            def _():
                scores(v + 1, None, 1 - parity)
                softmax_pv(v, parity)
        return carry

    lax.fori_loop(FOX_KSUB - 1, n_visits - 1, body, 0)
    for parity in range(2):
        @pl.when((n_visits - 1) % 2 == parity)
        def _():
            softmax_pv(n_visits - 1, parity)
    acc0 = acc_sc[0]
    acc1 = acc_sc[1]
    row = lax.broadcasted_iota(jnp.int32, (LANES, 1), 0)
    o_t = jnp.where(row < HEAD_DIM, acc0 / acc0[HEAD_DIM:HEAD_DIM + 1, :], acc1 / acc1[0:1, :])
    o_ref[...] = o_t.T.astype(BF16)


def _row_block_cast_jobs(weights, n_steps):
    jobs = []
    for w in weights:
        lead = w.shape[:-2]
        n_lead = 1
        for v in lead:
            n_lead *= v
        rows, cols = w.shape[-2:]
        per_lead = max(n_steps // n_lead, 1)
        blk = rows
        for cand in range(2 * SUBLANES, rows + 1, 2 * SUBLANES):
            if rows % cand == 0 and rows // cand <= per_lead:
                blk = cand
                break
        nblk = rows // blk
        total = n_lead * nblk

        def index(step, nblk=nblk, total=total, lead=lead):
            k = jnp.minimum(step, total - 1)
            idx = [k % nblk, 0]
            k = k // nblk
            for v in reversed(lead):
                idx.insert(0, k % v)
                k = k // v
            return tuple(idx)

        jobs.append((w, (1,) * len(lead) + (blk, cols), index))
    return jobs


def _fox_attn(ka, qta, vta, visits, cref, batch, cast_weights=()):
    t = ka.shape[0]
    s = t // batch
    tq = min(FOX_TQ, s)
    nq = s // tq
    npairs = FOX_HEADS // 2
    jobs = _row_block_cast_jobs(cast_weights, batch * npairs * nq)
    step_of = lambda b, p, i: (b * npairs + p) * nq + i
    cast_specs = [pl.BlockSpec(blk, lambda b, p, i, nv, cr, f=f: f(step_of(b, p, i)))
                  for _, blk, f in jobs]
    outs = pl.pallas_call(
        functools.partial(_fox_kernel, len(jobs)),
        grid_spec=pltpu.PrefetchScalarGridSpec(
            num_scalar_prefetch=2,
            grid=(batch, npairs, nq),
            in_specs=[
                pl.BlockSpec((2 * LANES, tq), lambda b, p, i, nv, cr: (p, b * nq + i)),
                pl.BlockSpec((s, 2 * LANES), lambda b, p, i, nv, cr: (b, p)),
                pl.BlockSpec((2 * LANES, s), lambda b, p, i, nv, cr: (p, b)),
            ] + cast_specs,
            out_specs=[pl.BlockSpec((tq, LANES), lambda b, p, i, nv, cr: (b * nq + i, p))]
            + cast_specs,
            scratch_shapes=[
                pltpu.VMEM((2, 1, tq), F32),
                pltpu.VMEM((2, LANES, tq), F32),
                pltpu.VMEM((2, tq // FOX_KSUB, tq), F32),
                pltpu.VMEM((2, 1, tq), F32),
                pltpu.VMEM((2, tq // FOX_KSUB, tq), F32),
                pltpu.VMEM((2, 1, tq), F32),
            ],
        ),
        out_shape=[jax.ShapeDtypeStruct((t, FOX_WIDTH), BF16)]
        + [jax.ShapeDtypeStruct(w.shape, BF16) for w, _, _ in jobs],
        compiler_params=_cparams(("arbitrary", "arbitrary", "arbitrary")),
        name="fox_attn",
    )(visits, cref, qta, ka, vta, *[w for w, _, _ in jobs])
    return outs[0], outs[1:]


def _mem_kv_kernel(mem_ref, nw_ref, w_ref, o_ref):
    h = _rmsnorm(mem_ref[0], nw_ref[...]).astype(BF16)
    o_ref[0] = jnp.dot(h, w_ref[...], preferred_element_type=F32).astype(BF16)


def _mem_kv(mem, norm_w, w_kv):
    b, m, d = mem.shape
    return pl.pallas_call(
        _mem_kv_kernel,
        grid=(b,),
        in_specs=[
            pl.BlockSpec((1, m, d), lambda i: (i, 0, 0)),
            pl.BlockSpec((1, d), lambda i: (0, 0)),
            pl.BlockSpec(w_kv.shape, lambda i: (0, 0)),
        ],
        out_specs=pl.BlockSpec((1, m, w_kv.shape[1]), lambda i: (i, 0, 0)),
        out_shape=jax.ShapeDtypeStruct((b, m, w_kv.shape[1]), BF16),
        compiler_params=_cparams(("parallel",)),
        name="mem_kv",
    )(mem, norm_w, w_kv)


def _post_mix_kernel(with_router, hmc_ref, hf_ref, x_ref, wout_ref, nxw_ref, wq_ref, kv_ref, wo_ref,
                     nfw_ref, *rest):
    if with_router:
        rw_ref, x2_ref, h3_ref, route_ref = rest
    else:
        x2_ref, h3_ref = rest
    half = MLSTM_WIDTH + CONF_CHANNELS
    x1 = (x_ref[...]
          + jnp.dot(hmc_ref[...], wout_ref[0:half, :], preferred_element_type=F32)
          + jnp.dot(hf_ref[...], wout_ref[half:, :], preferred_element_type=F32))
    h2 = _rmsnorm(x1, nxw_ref[...]).astype(BF16)
    q = jnp.dot(h2, wq_ref[...], preferred_element_type=F32) * (XATTN_HEAD_DIM ** -0.5)
    q = q.astype(BF16)
    kv = kv_ref[0]
    outs = []
    for h in range(XATTN_HEADS):
        lo = h * XATTN_HEAD_DIM
        kh = kv[:, lo:lo + XATTN_HEAD_DIM]
        vh = kv[:, XATTN_WIDTH + lo:XATTN_WIDTH + lo + XATTN_HEAD_DIM]
        s = lax.dot_general(q[:, lo:lo + XATTN_HEAD_DIM], kh, (((1,), (1,)), ((), ())),
                            preferred_element_type=F32)
        s = s - jnp.max(s, axis=1, keepdims=True)
        e = jnp.exp(s)
        pm = e / jnp.sum(e, axis=1, keepdims=True)
        outs.append(jnp.dot(pm.astype(BF16), vh, preferred_element_type=F32).astype(BF16))
    o = jnp.concatenate(outs, axis=1)
    x2 = x1 + jnp.dot(o, wo_ref[...], preferred_element_type=F32)
    x2_ref[...] = x2
    h3 = _rmsnorm(x2, nfw_ref[...])
    h3_ref[...] = h3.astype(h3_ref.dtype)
    if with_router:
        h_hi = h3.astype(BF16)
        h_lo = (h3 - h_hi.astype(F32)).astype(BF16)
        rw = rw_ref[...]
        w_hi = rw.astype(BF16)
        w_lo = (rw - w_hi.astype(F32)).astype(BF16)
        nt_dims = (((1,), (1,)), ((), ()))
        on_hi = lax.dot_general(jnp.concatenate([w_hi, w_lo], axis=0), h_hi, nt_dims,
                                preferred_element_type=F32)
        on_lo = lax.dot_general(w_hi, h_lo, nt_dims, preferred_element_type=F32)
        logits_t = on_hi[0:ROUTER_ROWS] + on_hi[ROUTER_ROWS:] + on_lo
        le = logits_t[0:N_EXPERTS, :]
        eid = lax.broadcasted_iota(jnp.int32, (N_EXPERTS, 1), 0).astype(F32)
        m1 = jnp.max(le, axis=0, keepdims=True)
        i1 = jnp.min(jnp.where(le == m1, eid, float(N_EXPERTS)), axis=0, keepdims=True)
        rest_l = jnp.where(eid == i1, -jnp.inf, le)
        m2 = jnp.max(rest_l, axis=0, keepdims=True)
        i2 = jnp.min(jnp.where(rest_l == m2, eid, float(N_EXPERTS)), axis=0, keepdims=True)
        e2 = jnp.exp(m2 - m1)
        w1 = 1.0 / (1.0 + e2)
        w2 = e2 * w1
        r8 = jnp.where(eid == 0, i1, jnp.where(eid == 1, i2, jnp.where(eid == 2, w1,
                       jnp.where(eid == 3, w2, 0.0))))
        route_t = jnp.concatenate([r8, jnp.zeros((LANES - N_EXPERTS, r8.shape[1]), F32)], axis=0)
        route_ref[...] = route_t.T


def _post_mix(hmc, hf, x, w_out, nx_w, w_q, kv, w_o, nf_w, router_w, batch):
    t, d = x.shape
    tm = min(TM_PROJ, t // batch)
    per_b = (t // batch) // tm
    with_router = router_w is not None
    const = lambda a: pl.BlockSpec(a.shape, lambda i: (0,) * a.ndim)
    in_specs = [
        pl.BlockSpec((tm, hmc.shape[1]), lambda i: (i, 0)),
        pl.BlockSpec((tm, hf.shape[1]), lambda i: (i, 0)),
        pl.BlockSpec((tm, d), lambda i: (i, 0)),
        const(w_out), const(nx_w), const(w_q),
        pl.BlockSpec((1,) + kv.shape[1:], lambda i: (i // per_b, 0, 0)),
        const(w_o), const(nf_w),
    ]
    args = [hmc, hf, x, w_out, nx_w, w_q, kv, w_o, nf_w]
    out_specs = [pl.BlockSpec((tm, d), lambda i: (i, 0)), pl.BlockSpec((tm, d), lambda i: (i, 0))]
    out_shape = [jax.ShapeDtypeStruct((t, d), F32),
                 jax.ShapeDtypeStruct((t, d), F32 if with_router else BF16)]
    if with_router:
        in_specs.append(const(router_w))
        args.append(router_w)
        out_specs.append(pl.BlockSpec((tm, LANES), lambda i: (i, 0)))
        out_shape.append(jax.ShapeDtypeStruct((t, LANES), F32))
    return pl.pallas_call(
        functools.partial(_post_mix_kernel, with_router),
        grid=(t // tm,),
        in_specs=in_specs, out_specs=out_specs, out_shape=out_shape,
        compiler_params=_cparams(("parallel",)),
        name="post_mix_router" if with_router else "post_mix",
    )(*args)


def _ffn_kernel(with_residual, te_ref, nt_ref, x_ref, wg_ref, wu_ref, wd_ref, *rest):
    if with_residual:
        res_ref, y_ref = rest
    else:
        (y_ref,) = rest

    @pl.when(pl.program_id(0) < nt_ref[0])
    def _():
        xb = x_ref[...].astype(BF16)
        acc = res_ref[...] if with_residual else None
        lo = 0
        for fc in FF_CHUNKS:
            a = jnp.dot(xb, wg_ref[0, :, lo:lo + fc], preferred_element_type=F32)
            u = jnp.dot(xb, wu_ref[0, :, lo:lo + fc], preferred_element_type=F32)
            hcur = (a * _sigmoid(a) * u).astype(BF16)
            part = jnp.dot(hcur, wd_ref[0, lo:lo + fc, :], preferred_element_type=F32)
            acc = part if acc is None else acc + part
            lo += fc
        y_ref[...] = acc

    @pl.when(pl.program_id(0) >= nt_ref[0])
    def _():
        y_ref[...] = jnp.zeros_like(y_ref)


def _ffn(x, w_gate, w_up, w_down, tile_expert, n_tiles, residual=None):
    rows, d = x.shape
    tm = min(TM_FFN, rows)
    nt = rows // tm
    ff = w_gate.shape[2]
    assert sum(FF_CHUNKS) == ff
    with_residual = residual is not None

    def row_map(i, te, ntl):
        return (jnp.minimum(i, ntl[0] - 1), 0)

    def w_map(i, te, ntl):
        return (te[jnp.minimum(i, ntl[0] - 1)], 0, 0)

    in_specs = [
        pl.BlockSpec((tm, d), row_map),
        pl.BlockSpec((1, d, ff), w_map),
        pl.BlockSpec((1, d, ff), w_map),
        pl.BlockSpec((1, ff, d), w_map),
    ]
    args = [x, w_gate, w_up, w_down]
    if with_residual:
        in_specs.append(pl.BlockSpec((tm, d), row_map))
        args.append(residual)
    return pl.pallas_call(
        functools.partial(_ffn_kernel, with_residual),
        grid_spec=pltpu.PrefetchScalarGridSpec(
            num_scalar_prefetch=2,
            grid=(nt,),
            in_specs=in_specs,
            out_specs=pl.BlockSpec((tm, d), lambda i, te, ntl: (i, 0)),
        ),
        out_shape=jax.ShapeDtypeStruct((rows, d), F32),
        compiler_params=_cparams(("arbitrary",)),
        name="ffn_dense" if with_residual else "ffn_experts",
    )(tile_expert, n_tiles, *args)


def _plan_kernel(route_ref, rank_ref, cnt_ref, carry):
    tp = route_ref.shape[0]

    @pl.when(pl.program_id(0) == 0)
    def _():
        carry[...] = jnp.zeros_like(carry)

    lane_i = lax.broadcasted_iota(jnp.int32, (1, LANES), 1)
    lane = lane_i.astype(F32)
    route = route_ref[...]
    i1 = route[:, 0:1]
    i2 = route[:, 1:2]
    onehot = (lane == i1).astype(F32) + (lane == i2).astype(F32)
    row = lax.broadcasted_iota(jnp.int32, (tp, tp), 0)
    col = lax.broadcasted_iota(jnp.int32, (tp, tp), 1)
    strict = (row > col).astype(BF16)
    before = jnp.dot(strict, onehot.astype(BF16), preferred_element_type=F32) + carry[...]
    r1 = jnp.sum(jnp.where(lane == i1, before, 0.0), axis=1, keepdims=True)
    r2 = jnp.sum(jnp.where(lane == i2, before, 0.0), axis=1, keepdims=True)
    rank_ref[...] = jnp.where(lane_i == 0, r1, jnp.where(lane_i == 1, r2, 0.0))
    total = carry[...] + jnp.sum(onehot, axis=0, keepdims=True)
    carry[...] = total
    cnt_ref[...] = total


def _plan(route):
    t = route.shape[0]
    tp = min(512, t)
    return pl.pallas_call(
        _plan_kernel,
        grid=(t // tp,),
        in_specs=[pl.BlockSpec((tp, LANES), lambda i: (i, 0))],
        out_specs=[pl.BlockSpec((tp, LANES), lambda i: (i, 0)),
                   pl.BlockSpec((1, LANES), lambda i: (0, 0))],
        out_shape=[jax.ShapeDtypeStruct((t, LANES), F32), jax.ShapeDtypeStruct((1, LANES), F32)],
        scratch_shapes=[pltpu.VMEM((1, LANES), F32)],
        compiler_params=_cparams(("arbitrary",)),
        name="route_plan",
    )(route)


def _row_copy(src, src_row, dst, dst_row, sem):
    return pltpu.make_async_copy(src.at[pl.ds(src_row, 1), :], dst.at[pl.ds(dst_row, 1), :], sem)


def _wait_rows(src, dst, sem, n):
    def wait(r, carry):
        _row_copy(src, 0, dst, 0, sem).wait()
        return carry
    lax.fori_loop(0, n, wait, 0, unroll=DMA_UNROLL)


def _dispatch_kernel(zflag_ref, pos_ref, h_ref, xs_ref, zero_buf, stage, sems, zsem):
    i = pl.program_id(0)
    nt = pl.num_programs(0)
    td = h_ref.shape[0]
    tm = zero_buf.shape[0]

    @pl.when(i == 0)
    def _():
        zero_buf[...] = jnp.zeros_like(zero_buf)

        def zero_copy(tile):
            row0 = pl.multiple_of(tile * tm, tm)
            return pltpu.make_async_copy(zero_buf, xs_ref.at[pl.ds(row0, tm), :], zsem)

        def zero_start(tile, carry):
            @pl.when(zflag_ref[tile] != 0)
            def _():
                zero_copy(tile).start()
            return carry

        def zero_wait(tile, carry):
            @pl.when(zflag_ref[tile] != 0)
            def _():
                zero_copy(tile).wait()
            return carry

        lax.fori_loop(0, zflag_ref.shape[0], zero_start, 0)
        lax.fori_loop(0, zflag_ref.shape[0], zero_wait, 0)

    def issue(slot):
        src = stage.at[slot]
        for r in range(td):
            _row_copy(src, r, xs_ref, pos_ref[0, 0, 2 * r], sems.at[slot]).start(priority=0)
            _row_copy(src, r, xs_ref, pos_ref[0, 0, 2 * r + 1], sems.at[slot]).start(priority=1)

    for parity in range(2):
        @pl.when(i % 2 == parity)
        def _():
            stage[parity] = h_ref[...]
            issue(parity)

            @pl.when(i > 0)
            def _():
                _wait_rows(stage.at[1 - parity], xs_ref, sems.at[1 - parity], 2 * td)

            @pl.when(i == nt - 1)
            def _():
                _wait_rows(stage.at[parity], xs_ref, sems.at[parity], 2 * td)


def _dispatch(h3, pos, zflag, rows, tm):
    t, d = h3.shape
    td = min(TD_ROUTE, t)
    nt = t // td
    pos3 = pos.reshape(nt, 1, 2 * td)
    return pl.pallas_call(
        _dispatch_kernel,
        grid_spec=pltpu.PrefetchScalarGridSpec(
            num_scalar_prefetch=1,
            grid=(nt,),
            in_specs=[
                pl.BlockSpec((1, 1, 2 * td), lambda i, zf: (i, 0, 0), memory_space=pltpu.SMEM),
                pl.BlockSpec((td, d), lambda i, zf: (i, 0)),
            ],
            out_specs=pl.BlockSpec(memory_space=pl.ANY),
            scratch_shapes=[pltpu.VMEM((tm, d), F32), pltpu.VMEM((2, td, d), F32),
                            pltpu.SemaphoreType.DMA((2,)), pltpu.SemaphoreType.DMA],
        ),
        out_shape=jax.ShapeDtypeStruct((rows, d), F32),
        compiler_params=_cparams(("arbitrary",)),
        name="dispatch",
    )(zflag, pos3, h3)


def _combine_kernel(pos_ref, pos_next_ref, x_ref, route_ref, nw_ref, ys_ref, o_ref, buf, sems):
    i = pl.program_id(0)
    nt = pl.num_programs(0)
    td = x_ref.shape[0]

    def issue(p_ref, slot):
        for r in range(td):
            _row_copy(ys_ref, p_ref[0, 0, 2 * r], buf.at[slot, 0], r, sems.at[slot]).start(
                priority=0)
            _row_copy(ys_ref, p_ref[0, 0, 2 * r + 1], buf.at[slot, 1], r, sems.at[slot]).start(
                priority=1)

    @pl.when(i == 0)
    def _():
        issue(pos_ref, 0)

    for parity in range(2):
        @pl.when(i % 2 == parity)
        def _():
            @pl.when(i + 1 < nt)
            def _():
                issue(pos_next_ref, 1 - parity)

            _wait_rows(ys_ref, buf.at[parity, 0], sems.at[parity], 2 * td)
            route = route_ref[...]
            x = x_ref[...] + route[:, 2:3] * buf[parity, 0] + route[:, 3:4] * buf[parity, 1]
            o_ref[...] = _rmsnorm(x, nw_ref[...])


def _combine(x2, route, pos, ys, norm_w):
    t, d = x2.shape
    td = min(TD_ROUTE, t)
    nt = t // td
    pos3 = pos.reshape(nt, 1, 2 * td)
    return pl.pallas_call(
        _combine_kernel,
        grid=(nt,),
        in_specs=[
            pl.BlockSpec((1, 1, 2 * td), lambda i: (i, 0, 0), memory_space=pltpu.SMEM),
            pl.BlockSpec((1, 1, 2 * td), lambda i: (jnp.minimum(i + 1, nt - 1), 0, 0),
                         memory_space=pltpu.SMEM),
            pl.BlockSpec((td, d), lambda i: (i, 0)),
            pl.BlockSpec((td, LANES), lambda i: (i, 0)),
            pl.BlockSpec((1, d), lambda i: (0, 0)),
            pl.BlockSpec(memory_space=pl.ANY),
        ],
        out_specs=pl.BlockSpec((td, d), lambda i: (i, 0)),
        out_shape=jax.ShapeDtypeStruct((t, d), F32),
        scratch_shapes=[pltpu.VMEM((2, 2, td, d), F32), pltpu.SemaphoreType.DMA((2,))],
        compiler_params=_cparams(("arbitrary",)),
        name="combine",
    )(pos3, pos3, x2, route, norm_w, ys)


def _split_w_in(w_in):
    mw, cw, fw = MLSTM_WIDTH, CONF_CHANNELS, FOX_WIDTH
    sizes = (2 * mw, mw, mw, MLSTM_HEADS, MLSTM_HEADS, 2 * cw, fw, fw, fw, FOX_HEADS)
    parts, off = [], 0
    for sz in sizes:
        parts.append(w_in[:, off:off + sz])
        off += sz
    m_qk, m_v, m_o, m_i, m_f, c_glu, f_q, f_k, f_v, f_f = parts
    w_main = jnp.concatenate([m_qk, m_v, m_o, c_glu, f_q, f_k, f_v], axis=1).astype(BF16)
    w_gate = jnp.concatenate([m_i, m_f, f_f], axis=1)
    w_gate = jnp.pad(w_gate, ((0, 0), (0, LANES - w_gate.shape[1]))).astype(BF16)
    return w_main, w_gate


def _gate_bias(b_i, b_f, fox_b):
    gb = jnp.concatenate([b_i, b_f, fox_b]).astype(F32)
    return jnp.pad(gb, (0, LANES - gb.shape[0])).reshape(1, LANES)


def _route_tables(route, rank, counts, tm, n_tiles_max):
    cnt = counts[0, :N_EXPERTS].astype(jnp.int32)
    tiles = (cnt + tm - 1) // tm
    tile_end = jnp.cumsum(tiles)
    row_off = (tile_end - tiles) * tm
    idx = route[:, 0:2].astype(jnp.int32)
    pos = row_off[idx] + rank[:, 0:2].astype(jnp.int32)
    tile_ids = jnp.arange(n_tiles_max, dtype=jnp.int32)
    tile_expert = jnp.sum(tile_ids[:, None] >= tile_end[None, :], axis=1).astype(jnp.int32)
    tile_expert = jnp.minimum(tile_expert, N_EXPERTS - 1)
    n_tiles = tile_end[-1:].astype(jnp.int32)
    is_last = jnp.any((tile_ids[:, None] == tile_end[None, :] - 1) & (tiles[None, :] > 0), axis=1)
    zero_flag = (is_last | (tile_ids >= n_tiles[0])).astype(jnp.int32)
    return pos.reshape(-1), tile_expert, n_tiles, zero_flag


def kernel(x, mem, norm_mix_w, w_in, mlstm_conv_w, mlstm_conv_b, mlstm_b_i, mlstm_b_f, mlstm_norm_w,
           conf_conv_w, conf_conv_b, conf_ln_w, conf_ln_b, fox_b_f, w_out, norm_xattn_w, norm_mem_w,
           xattn_w_q, xattn_w_kv, xattn_w_o, norm_ffn_w, ffn_w_gate, ffn_w_up, ffn_w_down, router_w,
           moe_w_gate, moe_w_up, moe_w_down, norm_final_w):
    batch, seq, d = x.shape
    depth = w_in.shape[0]
    t = batch * seq
    xf = x.reshape(t, d)
    row = lambda a: a.reshape(1, -1).astype(F32)
    out = None
    for l in range(depth):
        w_main, w_gate = _split_w_in(w_in[l])
        z, gates = _in_proj(xf, row(norm_mix_w[l]), w_main, w_gate)
        hmc, c2 = _seq_mix(
            z, gates, _gate_bias(mlstm_b_i[l], mlstm_b_f[l], fox_b_f[l]),
            mlstm_conv_w[l].astype(F32), row(mlstm_conv_b[l]), row(mlstm_norm_w[l]),
            conf_conv_w[l].astype(F32), row(conf_conv_b[l]), row(conf_ln_w[l]), row(conf_ln_b[l]),
            batch)
        ka, qta, vta, fox_stats = _fox_prep(z, c2)
        visits, cref = _fox_tile_plan(fox_stats, batch)
        dense = l % 2 == 0
        j = l // 2
        if dense:
            ffn_f32 = (ffn_w_gate[j][None], ffn_w_up[j][None], ffn_w_down[j][None])
        else:
            ffn_f32 = (moe_w_gate[j], moe_w_up[j], moe_w_down[j])
        hf, (wg_b, wu_b, wd_b) = _fox_attn(ka, qta, vta, visits, cref, batch, ffn_f32)
        kv = _mem_kv(mem, row(norm_mem_w[l]), xattn_w_kv[l].astype(BF16))
        rw = None
        if not dense:
            rw = jnp.pad(router_w[j].astype(F32).T, ((0, ROUTER_ROWS - N_EXPERTS), (0, 0)))
        res = _post_mix(hmc, hf, xf, w_out[l].astype(BF16), row(norm_xattn_w[l]),
                        xattn_w_q[l].astype(BF16), kv, xattn_w_o[l].astype(BF16),
                        row(norm_ffn_w[l]), rw, batch)
        if dense:
            x2, h3 = res
            nt = t // min(TM_FFN, t)
            xf = _ffn(h3, wg_b, wu_b, wd_b, jnp.zeros((nt,), jnp.int32),
                      jnp.full((1,), nt, jnp.int32), residual=x2)
            if l == depth - 1:
                raise NotImplementedError("final norm after a dense layer")
        else:
            x2, h3, route = res
            rank, counts = _plan(route)
            tm = min(TM_FFN, t)
            n_tiles_max = (2 * t) // tm + N_EXPERTS
            pos, tile_expert, n_tiles, zero_flag = _route_tables(route, rank, counts, tm, n_tiles_max)
            xs = _dispatch(h3, pos, zero_flag, n_tiles_max * tm, tm)
            ys = _ffn(xs, wg_b, wu_b, wd_b, tile_expert, n_tiles)
            assert l == depth - 1
            out = _combine(x2, route, pos, ys, row(norm_final_w))
    return out.reshape(batch, seq, d)
```
